```python
import math
import jax
import jax.numpy as jnp
from jax import lax
import numpy as np

D_MODEL = 2048
BATCH = 4
SEQ = 8192
DEPTH = 4
DEC_BATCH = 2
DEC_SEQ = 4096
PAST_LEN = 128

GRID_W = 64
N_MEM = 256
HY_WIDTH = 3 * D_MODEL // 4
S5_WIDTH = D_MODEL - HY_WIDTH
S5_GROUP = 16
S5_GROUPS = S5_WIDTH // S5_GROUP
S5_STATE = 64
S5_DT_MIN = 1e-3
S5_DT_MAX = 1e-1
HY_ORDER = 2
HY_BANDS = 16
HY_EMB = 2 * HY_BANDS + 1
HY_FILTER_HIDDEN = 64
HY_DECAY_TARGET = 1e-2
HY_SHORT_DECAY_PCT = 0.3
HY_LONG_DECAY_PCT = 1.5
RET_WIDTH = D_MODEL // 2
RET_HEADS = 4
RET_HEAD_DIM = RET_WIDTH // RET_HEADS
RET_CHUNK = 128
ROPE_BASE = 10000.0
NA_WIDTH = D_MODEL - RET_WIDTH
NA_HEADS = 16
NA_HEAD_DIM = NA_WIDTH // NA_HEADS
NA_WIN_ROWS = 8
NA_WIN_COLS = 16
XA_HEADS = 4
XA_HEAD_DIM = D_MODEL // XA_HEADS
FFN_HIDDEN = ((8 * D_MODEL + 3 * 256 - 1) // (3 * 256)) * 256
N_EVEN = (DEPTH + 1) // 2
N_ODD = DEPTH // 2
RMS_EPS = 1e-6
GN_EPS = 1e-6

kernel_name = 'hybrid_bidir_hyena_s5_retnet_natten_encoder'


def _rms_norm(x, g):
    xf = x.astype(jnp.float32)
    y = xf * lax.rsqrt(jnp.mean(xf * xf, axis=-1, keepdims=True) + RMS_EPS)
    return (y * g.astype(jnp.float32)).astype(x.dtype)


def _short_conv(u, w, b):
    up = jnp.pad(u, ((0, 0), (1, 1), (0, 0)))
    return up[:, :-2] * w[0] + up[:, 1:-1] * w[1] + up[:, 2:] * w[2] + b


def _hyena_filter_spectra(L, w1, b1, freq, w2, b2, w3):
    f32 = jnp.float32
    t = jnp.arange(L, dtype=f32) / L
    bands = jnp.arange(1, HY_BANDS + 1, dtype=f32)
    ang = 2.0 * math.pi * t[:, None] * bands[None, :]
    feats = jnp.concatenate([t[:, None], jnp.cos(ang), jnp.sin(ang)], axis=-1)
    h = jnp.sin(freq[0].astype(f32) * (feats @ w1.astype(f32) + b1.astype(f32)))
    h = jnp.sin(freq[1].astype(f32) * (h @ w2.astype(f32) + b2.astype(f32)))
    h = (h @ w3.astype(f32)).reshape(L, HY_ORDER, 2, HY_WIDTH)
    deltas = jnp.abs(jnp.linspace(math.log(HY_DECAY_TARGET) / HY_LONG_DECAY_PCT,
                                  math.log(HY_DECAY_TARGET) / HY_SHORT_DECAY_PCT, HY_WIDTH, dtype=f32))
    h = h * jnp.exp(-t[:, None] * deltas[None, :])[:, None, None, :]
    h_fwd, h_bwd = h[:, :, 0], h[:, :, 1]
    h_two = jnp.concatenate([h_fwd, jnp.zeros((1, HY_ORDER, HY_WIDTH), f32), h_bwd[:0:-1]], axis=0)
    return jnp.fft.rfft(h_two, axis=0)


def _fft_long_conv(z, spec):
    L = z.shape[1]
    zf = jnp.fft.rfft(z, n=2 * L, axis=1)
    return jnp.fft.irfft(zf * spec[None], n=2 * L, axis=1)[:, :L]


def _hyena(u, short_w, short_b, w1, b1, freq, w2, b2, w3, skip, out_g):
    L = u.shape[1]
    u = _short_conv(u, short_w.astype(jnp.float32), short_b.astype(jnp.float32))
    v, x1, x2 = jnp.split(u, 3, axis=-1)
    spec = _hyena_filter_spectra(L, w1, b1, freq, w2, b2, w3)
    skip = skip.astype(jnp.float32)
    z = x1 * (_fft_long_conv(v, spec[:, 0]) + skip[0] * v)
    z = x2 * (_fft_long_conv(z, spec[:, 1]) + skip[1] * z)
    return _rms_norm(z, out_g)


def _lin_combine(e1, e2):
    a1, b1 = e1
    a2, b2 = e2
    return a1 * a2, a2 * b1 + b2


def _s5_one_way(u, lam_bar, b_bar, c):
    bu = jnp.einsum('blgi,gpi->blgp', u.astype(jnp.complex64), b_bar)
    a = jnp.broadcast_to(lam_bar, bu.shape)
    _, s = lax.associative_scan(_lin_combine, (a, bu), axis=1)
    return jnp.einsum('blgp,gip->blgi', s, c).real


def _s5(u, a_re, a_im, log_dt, b_re, b_im, c_re, c_im, d, w_glu):
    f32 = jnp.float32
    b, L, _ = u.shape
    ug = u.reshape(b, L, S5_GROUPS, S5_GROUP)
    lam = lax.complex(a_re.astype(f32), a_im.astype(f32))
    step = jnp.exp(log_dt.astype(f32))[..., None]
    lam_bar = jnp.exp(lam * step)
    b_bar = ((lam_bar - 1.0) / lam)[..., None] * lax.complex(b_re.astype(f32), b_im.astype(f32))
    c = lax.complex(c_re.astype(f32), c_im.astype(f32))
    y = _s5_one_way(ug, lam_bar[0], b_bar[0], c[0]) + jnp.flip(
        _s5_one_way(jnp.flip(ug, 1), lam_bar[1], b_bar[1], c[1]), 1)
    y = y.reshape(b, L, S5_WIDTH) + d.astype(f32) * u
    y = jax.nn.gelu(y)
    return y * jax.nn.sigmoid(y @ w_glu.astype(f32))


def _even_mixer(h, w_in, short_w, short_b, w1, b1, freq, w2, b2, w3, skip, out_g,
                a_re, a_im, log_dt, b_re, b_im, c_re, c_im, d, w_glu):
    p = (h @ w_in).astype(jnp.float32)
    hy = _hyena(p[..., :3 * HY_WIDTH], short_w, short_b, w1, b1, freq, w2, b2, w3, skip, out_g)
    ss = _s5(p[..., 3 * HY_WIDTH:], a_re, a_im, log_dt, b_re, b_im, c_re, c_im, d, w_glu)
    return jnp.concatenate([hy, ss], axis=-1).astype(h.dtype)


def _rotary(x):
    L, dim = x.shape[1], x.shape[-1]
    inv = ROPE_BASE ** (-jnp.arange(0, dim, 2, dtype=jnp.float32) / dim)
    ang = jnp.arange(L, dtype=jnp.float32)[:, None] * inv[None, :]
    cos = jnp.cos(ang)[None, :, None, :]
    sin = jnp.sin(ang)[None, :, None, :]
    x1, x2 = x[..., :dim // 2], x[..., dim // 2:]
    return jnp.concatenate([x1 * cos - x2 * sin, x1 * sin + x2 * cos], axis=-1)


def _retention_one_way(q, k, v, log_gamma, strict):
    b, L, H, dk = q.shape
    dv = v.shape[-1]
    nc = L // RET_CHUNK
    qc = q.reshape(b, nc, RET_CHUNK, H, dk)
    kc = k.reshape(b, nc, RET_CHUNK, H, dk)
    vc = v.reshape(b, nc, RET_CHUNK, H, dv)
    pos = jnp.arange(RET_CHUNK, dtype=jnp.float32)
    rel = pos[:, None] - pos[None, :]
    mask = rel > 0 if strict else rel >= 0
    decay = jnp.where(mask[None], jnp.exp(jnp.where(mask, rel, 0.0)[None] * log_gamma[:, None, None]), 0.0)
    s = jnp.einsum('bnihd,bnjhd->bnhij', qc, kc) * decay
    intra = jnp.einsum('bnhij,bnjhe->bnihe', s, vc)
    q_scale = jnp.exp((pos + 1.0)[:, None] * log_gamma[None, :])
    k_scale = jnp.exp((RET_CHUNK - 1.0 - pos)[:, None] * log_gamma[None, :])
    chunk_decay = jnp.exp(RET_CHUNK * log_gamma)

    def step(state, xs):
        q_c, k_c, v_c = xs
        inter = jnp.einsum('bihd,bhde->bihe', q_c * q_scale[None, :, :, None], state)
        state = chunk_decay[None, :, None, None] * state + jnp.einsum(
            'bjhd,bjhe->bhde', k_c * k_scale[None, :, :, None], v_c)
        return state, inter

    s0 = jnp.zeros((b, H, dk, dv), jnp.float32)
    _, inter = lax.scan(step, s0, (jnp.moveaxis(qc, 1, 0), jnp.moveaxis(kc, 1, 0), jnp.moveaxis(vc, 1, 0)))
    return (intra + jnp.moveaxis(inter, 0, 1)).reshape(b, L, H, dv)


def _retention(q, k, v, g, ret_decay):
    b, L, _ = q.shape
    shp = (b, L, RET_HEADS, RET_HEAD_DIM)
    q = _rotary(q.reshape(shp))
    k = _rotary(k.reshape(shp)) * RET_HEAD_DIM ** -0.5
    v = v.reshape(shp)
    log_gamma = -jnp.exp(ret_decay.astype(jnp.float32))
    o = _retention_one_way(q, k, v, log_gamma[0], False) + jnp.flip(
        _retention_one_way(jnp.flip(q, 1), jnp.flip(k, 1), jnp.flip(v, 1), log_gamma[1], True), 1)
    mu = jnp.mean(o, axis=-1, keepdims=True)
    var = jnp.var(o, axis=-1, keepdims=True)
    o = (o - mu) * lax.rsqrt(var + GN_EPS)
    return o.reshape(b, L, RET_WIDTH) * jax.nn.silu(g)


def _neighborhood_attention(q, k, v, rpb):
    b, L, _ = q.shape
    rows = L // GRID_W
    wr = min(NA_WIN_ROWS, rows)
    shp = (b, rows, GRID_W, NA_HEADS, NA_HEAD_DIM)
    qg, kg, vg = q.reshape(shp), k.reshape(shp), v.reshape(shp)
    cols = np.arange(GRID_W)
    col_start = np.clip(cols - NA_WIN_COLS // 2, 0, GRID_W - NA_WIN_COLS)
    col_idx = col_start[:, None] + np.arange(NA_WIN_COLS)[None, :]
    col_off = col_idx - cols[:, None] + (NA_WIN_COLS - 1)
    rpb_c = jnp.transpose(rpb.astype(jnp.float32)[:, :, col_off], (0, 2, 1, 3))
    scale = NA_HEAD_DIM ** -0.5

    def row_fn(r):
        rs = jnp.clip(r - wr // 2, 0, rows - wr)
        q_r = lax.dynamic_index_in_dim(qg, r, axis=1, keepdims=False)
        k_win = lax.dynamic_slice_in_dim(kg, rs, wr, axis=1)[:, :, col_idx]
        v_win = lax.dynamic_slice_in_dim(vg, rs, wr, axis=1)[:, :, col_idx]
        row_off = rs + jnp.arange(wr) - r + (NA_WIN_ROWS - 1)
        bias = jnp.take(rpb_c, row_off, axis=2)
        s = jnp.einsum('bchd,brcwhd->bhcrw', q_r, k_win) * scale + bias[None]
        p = jax.nn.softmax(s.reshape(b, NA_HEADS, GRID_W, wr * NA_WIN_COLS), axis=-1).reshape(s.shape)
        return jnp.einsum('bhcrw,brcwhd->bchd', p, v_win)

    out = lax.map(row_fn, jnp.arange(rows))
    return jnp.moveaxis(out, 0, 1).reshape(b, L, NA_WIDTH)


def _odd_mixer(h, w_in, ret_decay, na_rpb):
    p = (h @ w_in).astype(jnp.float32)
    rw, nw = RET_WIDTH, NA_WIDTH
    q_r, k_r, v_r, g_r, q_n, k_n, v_n = jnp.split(
        p, [rw, 2 * rw, 3 * rw, 4 * rw, 4 * rw + nw, 4 * rw + 2 * nw], axis=-1)
    ret = _retention(q_r, k_r, v_r, g_r, ret_decay)
    na = _neighborhood_attention(q_n, k_n, v_n, na_rpb)
    return jnp.concatenate([ret, na], axis=-1).astype(h.dtype)


def _memory_cross_attention(h, mem, mem_g, wq, wkv, wo):
    b, L, _ = h.shape
    m_len = mem.shape[1]
    m = _rms_norm(mem, mem_g)
    q = (h @ wq).reshape(b, L, XA_HEADS, XA_HEAD_DIM)
    k, v = jnp.split(m @ wkv, 2, axis=-1)
    k = k.reshape(b, m_len, XA_HEADS, XA_HEAD_DIM)
    v = v.reshape(b, m_len, XA_HEADS, XA_HEAD_DIM)
    s = jnp.einsum('blhd,bmhd->bhlm', q, k).astype(jnp.float32) * XA_HEAD_DIM ** -0.5
    p = jax.nn.softmax(s, axis=-1).astype(h.dtype)
    o = jnp.einsum('bhlm,bmhd->blhd', p, v).reshape(b, L, D_MODEL)
    return o @ wo


def _swiglu(h, wg, wu, wd):
    return (jax.nn.silu(h @ wg) * (h @ wu)) @ wd


def _trunk(x, mem, prm):
    for layer in range(DEPTH):
        i = layer // 2
        g = prm['norm_g'][layer]
        h = _rms_norm(x, g[0])
        if layer % 2 == 0:
            h = _even_mixer(h, prm['ev_w_in'][i], prm['hy_short_w'][i], prm['hy_short_b'][i],
                            prm['hy_w1'][i], prm['hy_b1'][i], prm['hy_freq'][i], prm['hy_w2'][i],
                            prm['hy_b2'][i], prm['hy_w3'][i], prm['hy_skip'][i], prm['hy_out_g'][i],
                            prm['s5_a_re'][i], prm['s5_a_im'][i], prm['s5_log_dt'][i],
                            prm['s5_b_re'][i], prm['s5_b_im'][i], prm['s5_c_re'][i], prm['s5_c_im'][i],
                            prm['s5_d'][i], prm['s5_w_glu'][i])
        else:
            h = _odd_mixer(h, prm['od_w_in'][i], prm['ret_decay'][i], prm['na_rpb'][i])
        x = x + _rms_norm(h @ prm['mix_wo'][layer], g[1])
        h = _memory_cross_attention(_rms_norm(x, g[2]), mem, prm['mem_norm_g'][layer],
                                    prm['xa_wq'][layer], prm['xa_wkv'][layer], prm['xa_wo'][layer])
        x = x + _rms_norm(h, g[3])
        h = _swiglu(_rms_norm(x, g[4]), prm['ffn_wg'][layer], prm['ffn_wu'][layer], prm['ffn_wd'][layer])
        x = x + _rms_norm(h, g[5])
    return x


def setup_inputs(seed: int = 0) -> dict:
    key = jax.random.key(seed)
    ks = iter(jax.random.split(key, 48))
    f32 = jnp.float32

    def nrm(shape, scale):
        return scale * jax.random.normal(next(ks), shape, f32)

    def gain(shape):
        return 1.0 + 0.01 * jax.random.normal(next(ks), shape, f32)

    D = D_MODEL
    ne, no = N_EVEN, N_ODD
    G, P, I = S5_GROUPS, S5_STATE, S5_GROUP
    n_state = jnp.arange(P, dtype=f32)
    ret_base = jnp.log(-jnp.log1p(-jnp.exp2(-5.0 - jnp.arange(RET_HEADS, dtype=f32))))
    return {
        'x_prompt': nrm((BATCH, SEQ, D), 1.0),
        'x_sample': nrm((DEC_BATCH, DEC_SEQ, D), 1.0),
        'mem_prompt': nrm((BATCH, N_MEM, D), 1.0),
        'mem_sample': nrm((DEC_BATCH, N_MEM, D), 1.0),
        'norm_g': gain((DEPTH, 6, D)),
        'mix_wo': nrm((DEPTH, D, D), D ** -0.5),
        'ev_w_in': nrm((ne, D, 3 * HY_WIDTH + S5_WIDTH), D ** -0.5),
        'hy_short_w': nrm((ne, 3, 3 * HY_WIDTH), 3 ** -0.5),
        'hy_short_b': nrm((ne, 3 * HY_WIDTH), 0.01),
        'hy_w1': nrm((ne, HY_EMB, HY_FILTER_HIDDEN), HY_EMB ** -0.5),
        'hy_b1': nrm((ne, HY_FILTER_HIDDEN), 0.02),
        'hy_freq': 1.0 + nrm((ne, 2, HY_FILTER_HIDDEN), 0.1),
        'hy_w2': nrm((ne, HY_FILTER_HIDDEN, HY_FILTER_HIDDEN), HY_FILTER_HIDDEN ** -0.5),
        'hy_b2': nrm((ne, HY_FILTER_HIDDEN), 0.02),
        'hy_w3': nrm((ne, HY_FILTER_HIDDEN, HY_ORDER * 2 * HY_WIDTH), HY_FILTER_HIDDEN ** -0.5),
        'hy_skip': nrm((ne, HY_ORDER, HY_WIDTH), 1.0),
        'hy_out_g': gain((ne, HY_WIDTH)),
        's5_a_re': -0.5 + nrm((ne, 2, G, P), 0.01),
        's5_a_im': math.pi * n_state + nrm((ne, 2, G, P), 0.01),
        's5_log_dt': jax.random.uniform(next(ks), (ne, 2, G), f32, math.log(S5_DT_MIN), math.log(S5_DT_MAX)),
        's5_b_re': nrm((ne, 2, G, P, I), (2 * I) ** -0.5),
        's5_b_im': nrm((ne, 2, G, P, I), (2 * I) ** -0.5),
        's5_c_re': nrm((ne, 2, G, I, P), (2 * P) ** -0.5),
        's5_c_im': nrm((ne, 2, G, I, P), (2 * P) ** -0.5),
        's5_d': nrm((ne, S5_WIDTH), 1.0),
        's5_w_glu': nrm((ne, S5_WIDTH, S5_WIDTH), S5_WIDTH ** -0.5),
        'od_w_in': nrm((no, D, 4 * RET_WIDTH + 3 * NA_WIDTH), D ** -0.5),
        'ret_decay': ret_base + nrm((no, 2, RET_HEADS), 0.01),
        'na_rpb': nrm((no, NA_HEADS, 2 * NA_WIN_ROWS - 1, 2 * NA_WIN_COLS - 1), 0.1),
        'mem_norm_g': gain((DEPTH, D)),
        'xa_wq': nrm((DEPTH, D, D), D ** -0.5),
        'xa_wkv': nrm((DEPTH, D, 2 * D), D ** -0.5),
        'xa_wo': nrm((DEPTH, D, D), D ** -0.5),
        'ffn_wg': nrm((DEPTH, D, FFN_HIDDEN), D ** -0.5),
        'ffn_wu': nrm((DEPTH, D, FFN_HIDDEN), D ** -0.5),
        'ffn_wd': nrm((DEPTH, FFN_HIDDEN, D), FFN_HIDDEN ** -0.5),
    }


def reference(x_prompt, x_sample, mem_prompt, mem_sample, norm_g, mix_wo, ev_w_in, hy_short_w, hy_short_b,
              hy_w1, hy_b1, hy_freq, hy_w2, hy_b2, hy_w3, hy_skip, hy_out_g, s5_a_re, s5_a_im, s5_log_dt,
              s5_b_re, s5_b_im, s5_c_re, s5_c_im, s5_d, s5_w_glu, od_w_in, ret_decay, na_rpb, mem_norm_g,
              xa_wq, xa_wkv, xa_wo, ffn_wg, ffn_wu, ffn_wd):
    prm = {
        'norm_g': norm_g, 'mix_wo': mix_wo, 'ev_w_in': ev_w_in, 'hy_short_w': hy_short_w,
        'hy_short_b': hy_short_b, 'hy_w1': hy_w1, 'hy_b1': hy_b1, 'hy_freq': hy_freq, 'hy_w2': hy_w2,
        'hy_b2': hy_b2, 'hy_w3': hy_w3, 'hy_skip': hy_skip, 'hy_out_g': hy_out_g, 's5_a_re': s5_a_re,
        's5_a_im': s5_a_im, 's5_log_dt': s5_log_dt, 's5_b_re': s5_b_re, 's5_b_im': s5_b_im,
        's5_c_re': s5_c_re, 's5_c_im': s5_c_im, 's5_d': s5_d, 's5_w_glu': s5_w_glu, 'od_w_in': od_w_in,
        'ret_decay': ret_decay, 'na_rpb': na_rpb, 'mem_norm_g': mem_norm_g, 'xa_wq': xa_wq,
        'xa_wkv': xa_wkv, 'xa_wo': xa_wo, 'ffn_wg': ffn_wg, 'ffn_wu': ffn_wu, 'ffn_wd': ffn_wd,
    }
    y_prompt = _trunk(x_prompt, mem_prompt, prm)
    y_sample = _trunk(x_sample, mem_sample, prm)
    return (y_prompt, y_sample)
```

```python
import functools
import math

import jax
import jax.numpy as jnp
from jax import lax
from jax.experimental import pallas as pl
from jax.experimental.pallas import tpu as pltpu

F32 = jnp.float32
BF16 = jnp.bfloat16

V7X_VMEM_BYTES = 64 * 1024 * 1024
VMEM_LIMIT = V7X_VMEM_BYTES - 8 * 1024 * 1024
LANES = 128
SUBLANES = 8

D_MODEL = 2048
GRID_W = 64
HY_WIDTH = 3 * D_MODEL // 4
S5_WIDTH = D_MODEL - HY_WIDTH
S5_GROUP = 16
S5_GROUPS = S5_WIDTH // S5_GROUP
S5_STATE = 64
HY_ORDER = 2
HY_BANDS = 16
HY_FILTER_HIDDEN = 64
HY_DECAY_TARGET = 1e-2
HY_SHORT_DECAY_PCT = 0.3
HY_LONG_DECAY_PCT = 1.5
RET_WIDTH = D_MODEL // 2
RET_HEADS = 4
RET_HEAD_DIM = RET_WIDTH // RET_HEADS
ROPE_BASE = 10000.0
NA_WIDTH = D_MODEL - RET_WIDTH
NA_HEADS = 16
NA_HEAD_DIM = NA_WIDTH // NA_HEADS
NA_WIN_ROWS = 8
NA_WIN_COLS = 16
XA_HEADS = 4
XA_HEAD_DIM = D_MODEL // XA_HEADS
RMS_EPS = 1e-6
GN_EPS = 1e-6

FFT_N2 = 128
FFT_RB = SUBLANES
S5_T = 32
RET_CC = 256
NA_HG = 4
NA_RB = 8
NEG_INF = -1e30


def _cp(sem, vmem=VMEM_LIMIT):
    return pltpu.CompilerParams(dimension_semantics=sem, vmem_limit_bytes=vmem)


def _rms(x, g, eps=RMS_EPS):
    return x * lax.rsqrt(jnp.mean(x * x, axis=-1, keepdims=True) + eps) * g


def _norm_mm_kernel(x_ref, g_ref, w_ref, o_ref, xn_ref):
    @pl.when(pl.program_id(1) == 0)
    def _():
        xn_ref[...] = _rms(x_ref[...], g_ref[...]).astype(BF16)

    o_ref[...] = jnp.dot(xn_ref[...], w_ref[...], preferred_element_type=F32).astype(o_ref.dtype)


def norm_matmul(x, g, w, out_dtype, tm=1024, tn=512, name="norm_matmul"):
    M, K = x.shape
    N = w.shape[1]
    tm, tn = min(tm, M), min(tn, N)
    assert M % tm == 0 and N % tn == 0
    return pl.pallas_call(
        _norm_mm_kernel,
        out_shape=jax.ShapeDtypeStruct((M, N), out_dtype),
        grid=(M // tm, N // tn),
        in_specs=[pl.BlockSpec((tm, K), lambda i, j: (i, 0)),
                  pl.BlockSpec((1, K), lambda i, j: (0, 0)),
                  pl.BlockSpec((K, tn), lambda i, j: (0, j))],
        out_specs=pl.BlockSpec((tm, tn), lambda i, j: (i, j)),
        scratch_shapes=[pltpu.VMEM((tm, K), BF16)],
        compiler_params=_cp(("parallel", "arbitrary")),
        name=name,
    )(x, g.reshape(1, K), w)


def _mm_norm_res_kernel(*refs, n_ops, prenorm):
    pos = 0
    y = None
    for t in range(n_ops):
        a = refs[pos][...]
        w_ref = refs[pos + 1]
        pos += 2
        if prenorm[t]:
            a = _rms(a.astype(F32), refs[pos][...])
            pos += 1
        d = jnp.dot(a.astype(BF16), w_ref[...], preferred_element_type=F32)
        y = d if y is None else y + d
    g_ref, x_ref, o_ref = refs[pos], refs[pos + 1], refs[pos + 2]
    o_ref[...] = x_ref[...] + _rms(y, g_ref[...])


def matmul_norm_residual(ops, g, x, tm=512, name="matmul_norm_residual"):
    M, N = x.shape
    tm = min(tm, M)
    assert M % tm == 0
    args, specs, prenorm = [], [], []
    for a, w, pg in ops:
        kt = a.shape[1]
        args += [a, w]
        specs += [pl.BlockSpec((tm, kt), lambda i: (i, 0)), pl.BlockSpec((kt, N), lambda i: (0, 0))]
        prenorm.append(pg is not None)
        if pg is not None:
            args.append(pg.reshape(1, kt))
            specs.append(pl.BlockSpec((1, kt), lambda i: (0, 0)))
    args += [g.reshape(1, N), x]
    specs += [pl.BlockSpec((1, N), lambda i: (0, 0)), pl.BlockSpec((tm, N), lambda i: (i, 0))]
    return pl.pallas_call(
        functools.partial(_mm_norm_res_kernel, n_ops=len(ops), prenorm=tuple(prenorm)),
        out_shape=jax.ShapeDtypeStruct((M, N), F32),
        grid=(M // tm,),
        in_specs=specs,
        out_specs=pl.BlockSpec((tm, N), lambda i: (i, 0)),
        compiler_params=_cp(("parallel",)),
        name=name,
    )(*args)


def _ffn_kernel(x_ref, gi_ref, wg_ref, wu_ref, wd_ref, go_ref, o_ref, xn_ref):
    j = pl.program_id(1)

    @pl.when(j == 0)
    def _():
        xn_ref[...] = _rms(x_ref[...], gi_ref[...]).astype(BF16)

    xn = xn_ref[...]
    a = jnp.dot(xn, wg_ref[...], preferred_element_type=F32)
    u = jnp.dot(xn, wu_ref[...], preferred_element_type=F32)
    h = (a * jax.nn.sigmoid(a) * u).astype(BF16)
    d = jnp.dot(h, wd_ref[...], preferred_element_type=F32)

    @pl.when(j == 0)
    def _():
        o_ref[...] = d

    @pl.when(j > 0)
    def _():
        o_ref[...] += d

    @pl.when(j == pl.num_programs(1) - 1)
    def _():
        o_ref[...] = x_ref[...] + _rms(o_ref[...], go_ref[...])


def ffn_block(x, g_in, wg, wu, wd, g_out, tm=1024, th=512):
    M, D = x.shape
    Hd = wg.shape[1]
    tm, th = min(tm, M), min(th, Hd)
    assert M % tm == 0 and Hd % th == 0
    return pl.pallas_call(
        _ffn_kernel,
        out_shape=jax.ShapeDtypeStruct((M, D), F32),
        grid=(M // tm, Hd // th),
        in_specs=[pl.BlockSpec((tm, D), lambda i, j: (i, 0), pipeline_mode=pl.Buffered(1)),
                  pl.BlockSpec((1, D), lambda i, j: (0, 0)),
                  pl.BlockSpec((D, th), lambda i, j: (0, j)),
                  pl.BlockSpec((D, th), lambda i, j: (0, j)),
                  pl.BlockSpec((th, D), lambda i, j: (j, 0)),
                  pl.BlockSpec((1, D), lambda i, j: (0, 0))],
        out_specs=pl.BlockSpec((tm, D), lambda i, j: (i, 0)),
        scratch_shapes=[pltpu.VMEM((tm, D), BF16)],
        compiler_params=_cp(("parallel", "arbitrary")),
        name="ffn_block",
    )(x, g_in.reshape(1, D), wg, wu, wd, g_out.reshape(1, D))


def _xattn_kernel(q_ref, k_ref, v_ref, o_ref, *, heads, dh):
    scale = dh ** -0.5
    for h in range(heads):
        sl = slice(h * dh, (h + 1) * dh)
        s = lax.dot_general(q_ref[:, sl], k_ref[:, sl], (((1,), (1,)), ((), ())),
                            preferred_element_type=F32) * scale
        m = jnp.max(s, axis=-1, keepdims=True)
        p = jnp.exp(s - m)
        l = jnp.sum(p, axis=-1, keepdims=True)
        o = jnp.dot(p.astype(BF16), v_ref[:, sl], preferred_element_type=F32)
        o_ref[:, sl] = (o / l).astype(o_ref.dtype)


def cross_attention(q, kv, B, L, n_mem, tm=512):
    M, D = q.shape
    tm = min(tm, L)
    assert L % tm == 0
    bpl = L // tm
    return pl.pallas_call(
        functools.partial(_xattn_kernel, heads=XA_HEADS, dh=D // XA_HEADS),
        out_shape=jax.ShapeDtypeStruct((M, D), BF16),
        grid=(M // tm,),
        in_specs=[pl.BlockSpec((tm, D), lambda i: (i, 0)),
                  pl.BlockSpec((n_mem, D), lambda i: (i // bpl, 0)),
                  pl.BlockSpec((n_mem, D), lambda i: (i // bpl, 1))],
        out_specs=pl.BlockSpec((tm, D), lambda i: (i, 0)),
        compiler_params=_cp(("parallel",)),
        name="cross_attention",
    )(q, kv, kv)


def _short_conv_kernel(xp_ref, xc_ref, xn_ref, w_ref, b_ref, o_ref, *, blocks_per_seq):
    li = pl.program_id(0) % blocks_per_seq
    x = xc_ref[...]
    tl = x.shape[0]
    prev = jnp.where(li == 0, 0.0, xp_ref[SUBLANES - 1:SUBLANES, :])
    nxt = jnp.where(li == blocks_per_seq - 1, 0.0, xn_ref[0:1, :])
    row = lax.broadcasted_iota(jnp.int32, x.shape, 0)
    up = jnp.where(row == 0, prev, pltpu.roll(x, 1, axis=0))
    dn = jnp.where(row == tl - 1, nxt, pltpu.roll(x, tl - 1, axis=0))
    o_ref[...] = up * w_ref[0:1, :] + x * w_ref[1:2, :] + dn * w_ref[2:3, :] + b_ref[...]


def short_conv(p, w, b, L, tl=1024, W=512):
    M = p.shape[0]
    C = w.shape[1]
    tl, W = min(tl, L), min(W, C)
    assert L % tl == 0 and C % W == 0 and tl % SUBLANES == 0
    sb = tl // SUBLANES
    nb8 = M // SUBLANES
    return pl.pallas_call(
        functools.partial(_short_conv_kernel, blocks_per_seq=L // tl),
        out_shape=jax.ShapeDtypeStruct((M, C), F32),
        grid=(M // tl, C // W),
        in_specs=[pl.BlockSpec((SUBLANES, W), lambda i, c: (jnp.maximum(i * sb - 1, 0), c)),
                  pl.BlockSpec((tl, W), lambda i, c: (i, c)),
                  pl.BlockSpec((SUBLANES, W), lambda i, c: (jnp.minimum((i + 1) * sb, nb8 - 1), c)),
                  pl.BlockSpec((3, W), lambda i, c: (0, c)),
                  pl.BlockSpec((1, W), lambda i, c: (0, c))],
        out_specs=pl.BlockSpec((tl, W), lambda i, c: (i, c)),
        compiler_params=_cp(("parallel", "parallel")),
        name="hyena_short_conv",
    )(p, p, p, w, b.reshape(1, C))


def _filter_kernel(bands_ref, w1t_ref, w1c_ref, w1s_ref, b1_ref, f_ref, w2_ref, b2_ref, w3_ref, dl_ref, o_ref,
                   *, L):
    hi = lax.Precision.HIGHEST
    d = pl.program_id(1) % 2
    tl = o_ref.shape[0]
    m = pl.program_id(0) * tl + lax.broadcasted_iota(jnp.int32, (tl, 1), 0)
    pos = jnp.where(d == 0, m, L - m)
    t = pos.astype(F32) * (1.0 / L)
    ang = (2.0 * math.pi) * t * bands_ref[...]
    pre = (t * w1t_ref[...]
           + jnp.dot(jnp.cos(ang), w1c_ref[...], preferred_element_type=F32, precision=hi)
           + jnp.dot(jnp.sin(ang), w1s_ref[...], preferred_element_type=F32, precision=hi)
           + b1_ref[...])
    h = jnp.sin(f_ref[0:1, :] * pre)
    h = jnp.sin(f_ref[1:2, :] * (jnp.dot(h, w2_ref[...], preferred_element_type=F32, precision=hi) + b2_ref[...]))
    out = jnp.dot(h, w3_ref[...], preferred_element_type=F32, precision=hi)
    out = out * jnp.exp(-t * dl_ref[...])
    o_ref[...] = jnp.where(m == jnp.where(d == 1, 0, -1), 0.0, out)


def hyena_filters(w1, b1, freq, w2, b2, w3, L, tl=512, W=512):
    C = HY_WIDTH
    Hh = HY_FILTER_HIDDEN
    tl = min(tl, L)
    bands = jnp.zeros((1, LANES), F32).at[0, :HY_BANDS].set(jnp.arange(1, HY_BANDS + 1, dtype=F32))
    w1 = w1.astype(F32)
    w1t = w1[0:1]
    w1c = jnp.zeros((LANES, Hh), F32).at[:HY_BANDS].set(w1[1:1 + HY_BANDS])
    w1s = jnp.zeros((LANES, Hh), F32).at[:HY_BANDS].set(w1[1 + HY_BANDS:])
    deltas = jnp.abs(jnp.linspace(math.log(HY_DECAY_TARGET) / HY_LONG_DECAY_PCT,
                                  math.log(HY_DECAY_TARGET) / HY_SHORT_DECAY_PCT, C, dtype=F32)).reshape(1, C)
    ncb = C // W
    small = lambda shape: pl.BlockSpec(shape, lambda i, od, c: (0, 0))
    return pl.pallas_call(
        functools.partial(_filter_kernel, L=L),
        out_shape=jax.ShapeDtypeStruct((2 * HY_ORDER, L, C), F32),
        grid=(L // tl, 2 * HY_ORDER, ncb),
        in_specs=[small((1, LANES)), small((1, Hh)), small((LANES, Hh)), small((LANES, Hh)), small((1, Hh)),
                  small((2, Hh)), small((Hh, Hh)), small((1, Hh)),
                  pl.BlockSpec((Hh, W), lambda i, od, c: (0, od * ncb + c)),
                  pl.BlockSpec((1, W), lambda i, od, c: (0, c))],
        out_specs=pl.BlockSpec((None, tl, W), lambda i, od, c: (od, i, c)),
        compiler_params=_cp(("parallel", "parallel", "parallel")),
        name="hyena_filters",
    )(bands, w1t, w1c, w1s, b1.astype(F32).reshape(1, Hh), freq.astype(F32), w2.astype(F32),
      b2.astype(F32).reshape(1, Hh), w3.astype(F32), deltas)


def _fft_tables(L):
    N2 = FFT_N2
    N = 2 * L
    N1 = N // N2
    N1h = N1 // 2
    n2 = jnp.arange(N2, dtype=jnp.int32)[:, None, None]
    k1 = jnp.arange(N1, dtype=jnp.int32)[None, :, None]
    n1 = jnp.arange(N1h, dtype=jnp.int32)[None, None, :]
    ph = ((n1 * N2 + n2) * k1) % N
    ang = ph.astype(F32) * (2.0 * math.pi / N)
    gr, gi = jnp.cos(ang), -jnp.sin(ang)
    g_fwd = jnp.concatenate([jnp.concatenate([gr, -gi], 2), jnp.concatenate([gi, gr], 2)], 1).astype(BF16)
    sgn = jnp.where(k1 % 2 == 0, 1.0, -1.0).astype(F32)
    g_flt = jnp.concatenate([jnp.concatenate([gr, sgn * gr], 2), jnp.concatenate([gi, sgn * gi], 2)], 1).astype(BF16)
    er = jnp.swapaxes(gr, 1, 2) / N
    ei = -jnp.swapaxes(gi, 1, 2) / N
    g_inv = jnp.concatenate([jnp.concatenate([er, -ei], 2), jnp.concatenate([ei, er], 2)], 1).astype(BF16)
    a2 = (jnp.arange(N2, dtype=jnp.int32)[:, None] * jnp.arange(N2, dtype=jnp.int32)[None, :]) % N2
    ang2 = a2.astype(F32) * (2.0 * math.pi / N2)
    fr, fi = jnp.cos(ang2), -jnp.sin(ang2)
    f_mid = jnp.concatenate([jnp.concatenate([fr, -fi], 1), jnp.concatenate([fi, fr], 1)], 0).astype(BF16)
    f_mid_inv = jnp.concatenate([jnp.concatenate([fr, fi], 1), jnp.concatenate([-fi, fr], 1)], 0).astype(BF16)
    return dict(g_fwd=g_fwd, g_flt=g_flt, g_inv=g_inv, f_mid=f_mid, f_mid_inv=f_mid_inv, N1=N1)


def _fft_a_kernel(x_ref, g_ref, z_ref):
    n1 = z_ref.shape[1]
    for r in range(FFT_RB):
        xs = jnp.concatenate([x_ref[0, :, r, :], x_ref[1, :, r, :]], axis=0).astype(BF16)
        a = jnp.dot(g_ref[r], xs, preferred_element_type=F32)
        z_ref[0, :, r, :] = a[:n1]
        z_ref[1, :, r, :] = a[n1:]


def fft_stage_a(x4, col0, C, g, W=512):
    B, N1h, N2, _ = x4.shape
    N1 = 2 * N1h
    W = min(W, C)
    assert B % 2 == 0 and C % W == 0 and col0 % W == 0 and N2 % FFT_RB == 0
    cb0 = col0 // W
    return pl.pallas_call(
        _fft_a_kernel,
        out_shape=jax.ShapeDtypeStruct((B // 2, 2, N1, N2, C), F32),
        grid=(B // 2, N2 // FFT_RB, C // W),
        in_specs=[pl.BlockSpec((2, N1h, FFT_RB, W), lambda p, j, c: (p, 0, j, cb0 + c)),
                  pl.BlockSpec((FFT_RB, 2 * N1, N1), lambda p, j, c: (j, 0, 0))],
        out_specs=pl.BlockSpec((None, 2, N1, FFT_RB, W), lambda p, j, c: (p, 0, 0, j, c)),
        compiler_params=_cp(("parallel", "parallel", "parallel")),
        name="hyena_fft_a",
    )(x4, g)


def _fft_mid_kernel(*refs, kb, with_filter):
    if with_filter:
        z_ref, h_ref, f_ref, fi_ref, y_ref = refs
    else:
        z_ref, f_ref, y_ref = refs
    n2 = z_ref.shape[2]
    for k in range(kb):
        zs = jnp.concatenate([z_ref[0, k], z_ref[1, k]], axis=0).astype(BF16)
        x = jnp.dot(f_ref[...], zs, preferred_element_type=F32)
        if with_filter:
            xr, xi = x[:n2], x[n2:]
            hr, hi = h_ref[0, k], h_ref[1, k]
            ys = jnp.concatenate([xr * hr - xi * hi, xr * hi + xi * hr], axis=0).astype(BF16)
            x = jnp.dot(fi_ref[...], ys, preferred_element_type=F32)
        y_ref[0, k] = x[:n2]
        y_ref[1, k] = x[n2:]


def fft_stage_mid(z, tabs, h=None, order=0, kb=8, W=512):
    P, _, N1, N2, C = z.shape
    kb, W = min(kb, N1), min(W, C)
    assert N1 % kb == 0 and C % W == 0
    blk = pl.BlockSpec((None, 2, kb, N2, W), lambda k, c, p: (p, 0, k, 0, c))
    mat = pl.BlockSpec((2 * N2, 2 * N2), lambda k, c, p: (0, 0))
    if h is None:
        args, specs = (z, tabs["f_mid"]), [blk, mat]
    else:
        hblk = pl.BlockSpec((None, 2, kb, N2, W), lambda k, c, p: (order, 0, k, 0, c))
        args, specs = (z, h, tabs["f_mid"], tabs["f_mid_inv"]), [blk, hblk, mat, mat]
    return pl.pallas_call(
        functools.partial(_fft_mid_kernel, kb=kb, with_filter=h is not None),
        out_shape=jax.ShapeDtypeStruct(z.shape, F32),
        grid=(N1 // kb, C // W, P),
        in_specs=specs,
        out_specs=blk,
        compiler_params=_cp(("parallel", "parallel", "parallel")),
        name="hyena_fft_mid",
    )(*args)


def _fft_c_kernel(y_ref, g_ref, v_ref, x_ref, s_ref, o_ref):
    n1h = v_ref.shape[1]
    for r in range(FFT_RB):
        ys = jnp.concatenate([y_ref[0, :, r, :], y_ref[1, :, r, :]], axis=0).astype(BF16)
        c = jnp.dot(g_ref[r], ys, preferred_element_type=F32)
        for b in range(2):
            conv = c[b * n1h:(b + 1) * n1h]
            o_ref[b, :, r, :] = x_ref[b, :, r, :] * (conv + s_ref[...] * v_ref[b, :, r, :])


def fft_stage_c(y, g, v4, vcol0, x4, xcol0, skip, W=512):
    P, _, N1, N2, C = y.shape
    N1h = N1 // 2
    W = min(W, C)
    assert C % W == 0 and vcol0 % W == 0 and xcol0 % W == 0
    vb, xb = vcol0 // W, xcol0 // W
    return pl.pallas_call(
        _fft_c_kernel,
        out_shape=jax.ShapeDtypeStruct((2 * P, N1h, N2, C), F32),
        grid=(P, N2 // FFT_RB, C // W),
        in_specs=[pl.BlockSpec((None, 2, N1, FFT_RB, W), lambda p, j, c: (p, 0, 0, j, c)),
                  pl.BlockSpec((FFT_RB, N1, 2 * N1), lambda p, j, c: (j, 0, 0)),
                  pl.BlockSpec((2, N1h, FFT_RB, W), lambda p, j, c: (p, 0, j, vb + c)),
                  pl.BlockSpec((2, N1h, FFT_RB, W), lambda p, j, c: (p, 0, j, xb + c)),
                  pl.BlockSpec((1, W), lambda p, j, c: (0, c))],
        out_specs=pl.BlockSpec((2, N1h, FFT_RB, W), lambda p, j, c: (p, 0, j, c)),
        compiler_params=_cp(("parallel", "parallel", "parallel")),
        name="hyena_fft_c",
    )(y, g, v4, x4, skip.reshape(1, C))


def hyena_mixer(p, B, L, short_w, short_b, w1, b1, freq, w2, b2, w3, skip):
    C = HY_WIDTH
    N2 = FFT_N2
    N1h = L // N2
    tabs = _fft_tables(L)
    skip = skip.astype(F32)
    uc = short_conv(p, short_w.astype(F32), short_b.astype(F32), L)
    uc4 = uc.reshape(B, N1h, N2, 3 * C)
    filt = hyena_filters(w1, b1, freq, w2, b2, w3, L)
    spec = fft_stage_mid(fft_stage_a(filt.reshape(2 * HY_ORDER, N1h, N2, C), 0, C, tabs["g_flt"]), tabs)
    y = fft_stage_mid(fft_stage_a(uc4, 0, C, tabs["g_fwd"]), tabs, h=spec, order=0)
    z1 = fft_stage_c(y, tabs["g_inv"], uc4, 0, uc4, C, skip[0])
    y = fft_stage_mid(fft_stage_a(z1, 0, C, tabs["g_fwd"]), tabs, h=spec, order=1)
    z = fft_stage_c(y, tabs["g_inv"], z1, 0, uc4, 2 * C, skip[1])
    return z.reshape(B * L, C)


def _s5_tables(a_re, a_im, log_dt, b_re, b_im, c_re, c_im, n_chunks):
    T = S5_T
    G, P, I = S5_GROUPS, S5_STATE, S5_GROUP
    f32 = lambda a: a.astype(F32)
    a_re, a_im, b_re, b_im, c_re, c_im = map(f32, (a_re, a_im, b_re, b_im, c_re, c_im))
    step = jnp.exp(f32(log_dt))[..., None]
    lr, li = a_re * step, a_im * step
    mag = jnp.exp(lr)
    br_, bi_ = mag * jnp.cos(li), mag * jnp.sin(li)
    den = a_re * a_re + a_im * a_im
    qr = ((br_ - 1.0) * a_re + bi_ * a_im) / den
    qi = (bi_ * a_re - (br_ - 1.0) * a_im) / den
    bbr = qr[..., None] * b_re - qi[..., None] * b_im
    bbi = qr[..., None] * b_im + qi[..., None] * b_re

    def lam_pow(k):
        k = k.astype(F32)
        m = jnp.exp(lr[..., None] * k)
        return m * jnp.cos(li[..., None] * k), m * jnp.sin(li[..., None] * k)

    lags = jnp.arange(T + 1)
    pr, pi_ = lam_pow(lags)
    cpr = c_re[..., None] * pr[:, :, None] - c_im[..., None] * pi_[:, :, None]
    cpi = c_re[..., None] * pi_[:, :, None] + c_im[..., None] * pr[:, :, None]
    kern = jnp.einsum("dgjpk,dgpi->dgkji", cpr, bbr) - jnp.einsum("dgjpk,dgpi->dgkji", cpi, bbi)
    tt = jnp.arange(T)
    lag = tt[:, None] - tt[None, :]
    kf = kern[0][:, jnp.clip(lag, 0, T)]
    kb = kern[1][:, jnp.clip(-lag, 0, T)]
    m_f = (lag >= 0)[None, :, :, None, None]
    m_b = (lag <= 0)[None, :, :, None, None]
    toep = jnp.where(m_f, kf, 0.0) + jnp.where(m_b, kb, 0.0)
    toep = toep.transpose(0, 1, 3, 2, 4).reshape(G, T * I, T * I)
    ef = T - 1 - tt
    eb = tt

    def bst(d, e):
        wr, wi = pr[d][..., e], pi_[d][..., e]
        re = wr[..., None] * bbr[d][:, :, None] - wi[..., None] * bbi[d][:, :, None]
        im = wr[..., None] * bbi[d][:, :, None] + wi[..., None] * bbr[d][:, :, None]
        return jnp.concatenate([re, im], axis=1).reshape(G, 2 * P, T * I)

    bst_all = jnp.stack([bst(0, ef), bst(1, eb)], axis=1)

    def cst(d, e):
        xr, xi = cpr[d][..., e], cpi[d][..., e]
        m = jnp.concatenate([xr, -xi], axis=2)
        return m.transpose(0, 3, 1, 2).reshape(G, T * I, 2 * P)

    cst_all = jnp.stack([cst(0, tt + 1), cst(1, T - tt)], axis=1)
    nsteps = max(1, int(math.log2(n_chunks)))
    e2 = T * (2 ** jnp.arange(nsteps))
    ar, ai = lam_pow(e2)
    ap = jnp.stack([ar, ai], axis=-1).transpose(1, 0, 3, 4, 2)
    ap = jnp.broadcast_to(ap[..., None], ap.shape + (LANES,))
    return toep.astype(BF16), bst_all.astype(BF16), cst_all.astype(BF16), ap, nsteps


def _s5_kernel(u_ref, toep_ref, bst_ref, cst_ref, ap_ref, y_ref, *, n_chunks, nsteps):
    T, I, cols = u_ref.shape
    P = S5_STATE
    u = u_ref[...].reshape(T * I, cols).astype(BF16)
    y = jnp.dot(toep_ref[...], u, preferred_element_type=F32)
    cidx = lax.broadcasted_iota(jnp.int32, (P, cols), 1) % n_chunks

    def shifted(x, sh, d):
        if d == 0:
            return jnp.where(cidx >= sh, pltpu.roll(x, sh, axis=1), 0.0)
        return jnp.where(cidx < n_chunks - sh, pltpu.roll(x, cols - sh, axis=1), 0.0)

    for d in range(2):
        v = jnp.dot(bst_ref[d], u, preferred_element_type=F32)
        sr, si = v[:P], v[P:]
        for j in range(nsteps):
            if (1 << j) >= n_chunks:
                break
            ar, ai = ap_ref[d, j, 0][:, :1], ap_ref[d, j, 1][:, :1]
            rr, ri = shifted(sr, 1 << j, d), shifted(si, 1 << j, d)
            sr, si = sr + ar * rr - ai * ri, si + ar * ri + ai * rr
        s_in = jnp.concatenate([shifted(sr, 1, d), shifted(si, 1, d)], axis=0).astype(BF16)
        y = y + jnp.dot(cst_ref[d], s_in, preferred_element_type=F32)
    y_ref[...] = y.reshape(T, I, cols)


def _s5_out_kernel(y_ref, u_ref, d_ref, w_ref, o_ref):
    y = y_ref[...] + d_ref[...] * u_ref[...]
    y = jax.nn.gelu(y, approximate=True)
    z = jnp.dot(y.astype(BF16), w_ref[...], preferred_element_type=F32)
    o_ref[...] = (y * jax.nn.sigmoid(z)).astype(o_ref.dtype)


def s5_mixer(p, col0, B, L, a_re, a_im, log_dt, b_re, b_im, c_re, c_im, d, w_glu, tm=1024):
    T, Wd, I, G = S5_T, S5_WIDTH, S5_GROUP, S5_GROUPS
    nC = L // T
    cols = B * nC
    assert L % T == 0 and nC & (nC - 1) == 0 and cols % LANES == 0
    toep, bst, cst, ap, nsteps = _s5_tables(a_re, a_im, log_dt, b_re, b_im, c_re, c_im, nC)
    u = p[:, col0:col0 + Wd]
    ut = u.reshape(B, nC, T, Wd).transpose(2, 3, 0, 1).reshape(T, Wd, cols)
    blk = pl.BlockSpec((T, I, cols), lambda g: (0, g, 0))
    yt = pl.pallas_call(
        functools.partial(_s5_kernel, n_chunks=nC, nsteps=nsteps),
        out_shape=jax.ShapeDtypeStruct((T, Wd, cols), F32),
        grid=(G,),
        in_specs=[blk,
                  pl.BlockSpec((None, T * I, T * I), lambda g: (g, 0, 0)),
                  pl.BlockSpec((None, 2, 2 * S5_STATE, T * I), lambda g: (g, 0, 0, 0)),
                  pl.BlockSpec((None, 2, T * I, 2 * S5_STATE), lambda g: (g, 0, 0, 0)),
                  pl.BlockSpec((None, 2, nsteps, 2, S5_STATE, LANES), lambda g: (g, 0, 0, 0, 0, 0))],
        out_specs=blk,
        compiler_params=_cp(("parallel",)),
        name="s5_scan",
    )(ut, toep, bst, cst, ap)
    y = yt.reshape(T, Wd, B, nC).transpose(2, 3, 0, 1).reshape(B * L, Wd)
    M = B * L
    tm = min(tm, M)
    cb = col0 // Wd
    assert col0 % Wd == 0 and M % tm == 0
    return pl.pallas_call(
        _s5_out_kernel,
        out_shape=jax.ShapeDtypeStruct((M, Wd), BF16),
        grid=(M // tm,),
        in_specs=[pl.BlockSpec((tm, Wd), lambda i: (i, 0)),
                  pl.BlockSpec((tm, Wd), lambda i: (i, cb)),
                  pl.BlockSpec((1, Wd), lambda i: (0, 0)),
                  pl.BlockSpec((Wd, Wd), lambda i: (0, 0))],
        out_specs=pl.BlockSpec((tm, Wd), lambda i: (i, 0)),
        compiler_params=_cp(("parallel",)),
        name="s5_glu",
    )(y, p, d.astype(F32).reshape(1, Wd), w_glu.astype(BF16))


def _ret_kernel(q_ref, k_ref, v_ref, g_ref, cos_ref, sin_ref, dm_ref, sc_ref, cd_ref, o_ref, acc_ref, st_ref,
                *, n_chunks, L):
    c = pl.program_id(2)
    nC = n_chunks
    Cc, dh = q_ref.shape
    h2 = dh // 2

    @pl.when((c == 0) | (c == nC))
    def _():
        st_ref[...] = jnp.zeros_like(st_ref)

    chunk = jnp.where(c < nC, c, 2 * nC - 1 - c)
    row0 = pl.multiple_of(chunk * Cc, Cc)
    cos, sin = cos_ref[...], sin_ref[...]

    def rot(x):
        x1, x2 = x[:, :h2], x[:, h2:]
        return jnp.concatenate([x1 * cos - x2 * sin, x1 * sin + x2 * cos], axis=-1)

    qr = rot(q_ref[...])
    kr = rot(k_ref[...]) * (dh ** -0.5)
    v = v_ref[...].astype(BF16)
    st = st_ref[...]

    def sweep(qs, ks, cd, intra):
        o = jnp.dot((qr * qs).astype(BF16), st.astype(BF16), preferred_element_type=F32)
        if intra:
            s = lax.dot_general(qr.astype(BF16), kr.astype(BF16), (((1,), (1,)), ((), ())),
                                preferred_element_type=F32) * dm_ref[...]
            o = o + jnp.dot(s.astype(BF16), v, preferred_element_type=F32)
        kt = (kr * ks).T.astype(BF16)
        st_ref[...] = st * cd + jnp.dot(kt, v, preferred_element_type=F32)
        return o

    @pl.when(c < nC)
    def _():
        acc_ref[pl.ds(row0, Cc), :] = sweep(sc_ref[0], sc_ref[1], cd_ref[0, 0:1, 0:1], True)

    @pl.when(c >= nC)
    def _():
        acc_ref[pl.ds(row0, Cc), :] += sweep(sc_ref[2], sc_ref[3], cd_ref[1, 0:1, 0:1], False)

    @pl.when(c == 2 * nC - 1)
    def _():
        def body(i, carry):
            r0 = pl.multiple_of(i * Cc, Cc)
            o = acc_ref[pl.ds(r0, Cc), :]
            mu = jnp.mean(o, axis=-1, keepdims=True)
            oc = o - mu
            var = jnp.mean(oc * oc, axis=-1, keepdims=True)
            g = g_ref[pl.ds(r0, Cc), :]
            o_ref[pl.ds(r0, Cc), :] = (oc * lax.rsqrt(var + GN_EPS) * (g * jax.nn.sigmoid(g))).astype(o_ref.dtype)
            return carry
        lax.fori_loop(0, L // Cc, body, 0)


def retention_mixer(p, B, L, ret_decay):
    H, dh = RET_HEADS, RET_HEAD_DIM
    Cc = min(RET_CC, L)
    nC = L // Cc
    assert L % Cc == 0
    lg = -jnp.exp(ret_decay.astype(F32))
    pos = jnp.arange(Cc, dtype=F32)
    rel = pos[:, None] - pos[None, :]
    lf, lb = lg[0][:, None, None], lg[1][:, None, None]
    dm = jnp.where(rel >= 0, jnp.exp(jnp.maximum(rel, 0.0) * lf), jnp.exp(jnp.maximum(-rel, 0.0) * lb))
    sc = jnp.stack([jnp.exp((pos + 1.0)[None] * lg[0][:, None]),
                    jnp.exp((Cc - 1.0 - pos)[None] * lg[0][:, None]),
                    jnp.exp((Cc - pos)[None] * lg[1][:, None]),
                    jnp.exp(pos[None] * lg[1][:, None])], axis=1)[..., None]
    cd = jnp.broadcast_to(jnp.exp(Cc * lg).T[:, :, None, None], (H, 2, SUBLANES, LANES))
    inv = ROPE_BASE ** (-jnp.arange(0, dh, 2, dtype=F32) / dh)
    ang = jnp.arange(L, dtype=F32)[:, None] * inv[None, :]
    cos, sin = jnp.cos(ang), jnp.sin(ang)

    def cidx(c):
        return jnp.where(c < nC, c, 2 * nC - 1 - c)

    qkv = lambda off: pl.BlockSpec((Cc, dh), lambda b, h, c: (b * nC + cidx(c), off * H + h))
    tab = pl.BlockSpec((Cc, dh // 2), lambda b, h, c: (cidx(c), 0))
    return pl.pallas_call(
        functools.partial(_ret_kernel, n_chunks=nC, L=L),
        out_shape=jax.ShapeDtypeStruct((B * L, H * dh), BF16),
        grid=(B, H, 2 * nC),
        in_specs=[qkv(0), qkv(1), qkv(2),
                  pl.BlockSpec((L, dh), lambda b, h, c: (b, 3 * H + h)),
                  tab, tab,
                  pl.BlockSpec((None, Cc, Cc), lambda b, h, c: (h, 0, 0)),
                  pl.BlockSpec((None, 4, Cc, 1), lambda b, h, c: (h, 0, 0, 0)),
                  pl.BlockSpec((None, 2, SUBLANES, LANES), lambda b, h, c: (h, 0, 0, 0))],
        out_specs=pl.BlockSpec((L, dh), lambda b, h, c: (b, h)),
        scratch_shapes=[pltpu.VMEM((L, dh), F32), pltpu.VMEM((dh, dh), F32)],
        compiler_params=_cp(("parallel", "parallel", "arbitrary")),
        name="retention",
    )(p, p, p, p, cos, sin, dm, sc, cd)


def _na_bias_tables(rpb):
    Wc, WR, WC = GRID_W, NA_WIN_ROWS, NA_WIN_COLS
    rpb = rpb.astype(F32)
    delta = jnp.arange(WR)[:, None, None, None]
    c = jnp.arange(Wc)[None, :, None, None]
    j = jnp.arange(WR)[None, None, :, None]
    kc = jnp.arange(Wc)[None, None, None, :]
    ri = jnp.broadcast_to(j - delta + (WR - 1), (WR, Wc, WR, Wc))
    ci = jnp.broadcast_to(kc - c + (WC - 1), (WR, Wc, WR, Wc))
    cs = jnp.clip(c - WC // 2, 0, Wc - WC)
    valid = jnp.broadcast_to((kc >= cs) & (kc < cs + WC), (WR, Wc, WR, Wc))
    t = rpb[:, ri, jnp.clip(ci, 0, 2 * WC - 2)]
    t = jnp.where(valid[None], t, NEG_INF)
    t = t.reshape(NA_HEADS // NA_HG, NA_HG, WR, Wc, WR * Wc).transpose(0, 2, 1, 3, 4)
    return t.reshape(NA_HEADS // NA_HG, WR, NA_HG * Wc, WR * Wc)


def _na_kernel(q_ref, k_ref, v_ref, b_ref, o_ref, *, rows):
    Wc, WR = GRID_W, NA_WIN_ROWS
    hw = NA_HG * NA_HEAD_DIM
    hq = NA_HG * Wc
    scale = NA_HEAD_DIM ** -0.5
    rb = pl.program_id(2)
    own = (lax.broadcasted_iota(jnp.int32, (hq, hw), 0) // Wc
           == lax.broadcasted_iota(jnp.int32, (hq, hw), 1) // NA_HEAD_DIM)

    def body(i, carry):
        r = rb * NA_RB + i
        rs = jnp.clip(r - WR // 2, 0, rows - WR)
        q = q_ref[pl.ds(pl.multiple_of(i * Wc, Wc), Wc), :]
        qs = jnp.where(own, jnp.concatenate([q] * NA_HG, axis=0), jnp.zeros((), q.dtype))
        k0 = pl.multiple_of(rs * Wc, Wc)
        kw = k_ref[pl.ds(k0, WR * Wc), :]
        vw = v_ref[pl.ds(k0, WR * Wc), :]
        s = lax.dot_general(qs, kw, (((1,), (1,)), ((), ())), preferred_element_type=F32) * scale + b_ref[r - rs]
        m = jnp.max(s, axis=-1, keepdims=True)
        e = jnp.exp(s - m)
        l = jnp.sum(e, axis=-1, keepdims=True)
        o = jnp.dot(e.astype(BF16), vw, preferred_element_type=F32) / l
        o = jnp.where(own, o, 0.0)
        out = o[0:Wc]
        for h in range(1, NA_HG):
            out = out + o[h * Wc:(h + 1) * Wc]
        o_ref[pl.ds(pl.multiple_of(i * Wc, Wc), Wc), :] = out.astype(o_ref.dtype)
        return carry

    lax.fori_loop(0, NA_RB, body, 0)


def neighborhood_mixer(qkv, B, L, rpb):
    Wc = GRID_W
    rows = L // Wc
    assert rows >= NA_WIN_ROWS and rows % NA_RB == 0
    hw = NA_HG * NA_HEAD_DIM
    nhg = NA_HEADS // NA_HG
    bias = _na_bias_tables(rpb)
    nrb = rows // NA_RB
    return pl.pallas_call(
        functools.partial(_na_kernel, rows=rows),
        out_shape=jax.ShapeDtypeStruct((B * L, NA_WIDTH), BF16),
        grid=(B, nhg, nrb),
        in_specs=[pl.BlockSpec((NA_RB * Wc, hw), lambda b, g, r: (b * nrb + r, g)),
                  pl.BlockSpec((L, hw), lambda b, g, r: (b, nhg + g)),
                  pl.BlockSpec((L, hw), lambda b, g, r: (b, 2 * nhg + g)),
                  pl.BlockSpec((None, NA_WIN_ROWS, NA_HG * Wc, NA_WIN_ROWS * Wc), lambda b, g, r: (g, 0, 0, 0))],
        out_specs=pl.BlockSpec((NA_RB * Wc, hw), lambda b, g, r: (b * nrb + r, g)),
        compiler_params=_cp(("parallel", "parallel", "arbitrary")),
        name="neighborhood_attention",
    )(qkv, qkv, qkv, bias)


def _trunk(x, mem, B, L, prm, wb):
    n_mem = mem.shape[0] // B
    depth = prm["norm_g"].shape[0]
    hw3 = 3 * HY_WIDTH
    for layer in range(depth):
        i = layer // 2
        g = prm["norm_g"][layer]
        if layer % 2 == 0:
            p = norm_matmul(x, g[0], wb["ev_w_in"][i], F32, name="even_in_proj")
            z = hyena_mixer(p, B, L, prm["hy_short_w"][i], prm["hy_short_b"][i], prm["hy_w1"][i], prm["hy_b1"][i],
                            prm["hy_freq"][i], prm["hy_w2"][i], prm["hy_b2"][i], prm["hy_w3"][i], prm["hy_skip"][i])
            ss = s5_mixer(p, hw3, B, L, prm["s5_a_re"][i], prm["s5_a_im"][i], prm["s5_log_dt"][i],
                          prm["s5_b_re"][i], prm["s5_b_im"][i], prm["s5_c_re"][i], prm["s5_c_im"][i],
                          prm["s5_d"][i], prm["s5_w_glu"][i])
            wo = wb["mix_wo"][layer]
            ops = [(z, wo[:HY_WIDTH], prm["hy_out_g"][i]), (ss, wo[HY_WIDTH:], None)]
        else:
            w_in = wb["od_w_in"][i]
            pr = norm_matmul(x, g[0], w_in[:, :4 * RET_WIDTH], F32, name="odd_in_proj_ret")
            pn = norm_matmul(x, g[0], w_in[:, 4 * RET_WIDTH:], BF16, name="odd_in_proj_na")
            ret = retention_mixer(pr, B, L, prm["ret_decay"][i])
            na = neighborhood_mixer(pn, B, L, prm["na_rpb"][i])
            wo = wb["mix_wo"][layer]
            ops = [(ret, wo[:RET_WIDTH], None), (na, wo[RET_WIDTH:], None)]
        x = matmul_norm_residual(ops, g[1], x, name="mix_out_proj")
        q = norm_matmul(x, g[2], wb["xa_wq"][layer], BF16, name="xattn_q_proj")
        kv = norm_matmul(mem, prm["mem_norm_g"][layer], wb["xa_wkv"][layer], BF16, name="xattn_kv_proj")
        o = cross_attention(q, kv, B, L, n_mem)
        x = matmul_norm_residual([(o, wb["xa_wo"][layer], None)], g[3], x, name="xattn_out_proj")
        x = ffn_block(x, g[4], wb["ffn_wg"][layer], wb["ffn_wu"][layer], wb["ffn_wd"][layer], g[5])
    return x


def kernel(x_prompt, x_sample, mem_prompt, mem_sample, norm_g, mix_wo, ev_w_in, hy_short_w, hy_short_b, hy_w1, hy_b1, hy_freq, hy_w2, hy_b2, hy_w3, hy_skip, hy_out_g, s5_a_re, s5_a_im, s5_log_dt, s5_b_re, s5_b_im, s5_c_re, s5_c_im, s5_d, s5_w_glu, od_w_in, ret_decay, na_rpb, mem_norm_g, xa_wq, xa_wkv, xa_wo, ffn_wg, ffn_wu, ffn_wd):
    prm = dict(norm_g=norm_g, hy_short_w=hy_short_w, hy_short_b=hy_short_b, hy_w1=hy_w1, hy_b1=hy_b1,
               hy_freq=hy_freq, hy_w2=hy_w2, hy_b2=hy_b2, hy_w3=hy_w3, hy_skip=hy_skip, hy_out_g=hy_out_g,
               s5_a_re=s5_a_re, s5_a_im=s5_a_im, s5_log_dt=s5_log_dt, s5_b_re=s5_b_re, s5_b_im=s5_b_im,
               s5_c_re=s5_c_re, s5_c_im=s5_c_im, s5_d=s5_d, s5_w_glu=s5_w_glu, ret_decay=ret_decay,
               na_rpb=na_rpb, mem_norm_g=mem_norm_g)
    wb = {k: v.astype(BF16) for k, v in dict(mix_wo=mix_wo, ev_w_in=ev_w_in, od_w_in=od_w_in, xa_wq=xa_wq,
                                             xa_wkv=xa_wkv, xa_wo=xa_wo, ffn_wg=ffn_wg, ffn_wu=ffn_wu,
                                             ffn_wd=ffn_wd).items()}
    outs = []
    for x, mem in ((x_prompt, mem_prompt), (x_sample, mem_sample)):
        B, L, D = x.shape
        y = _trunk(x.reshape(B * L, D), mem.reshape(-1, D), B, L, prm, wb)
        outs.append(y.reshape(B, L, D))
    return tuple(outs)
```

```python
import functools
import math

import jax
import jax.numpy as jnp
from jax import lax
from jax.experimental import pallas as pl
from jax.experimental.pallas import tpu as pltpu

F32 = jnp.float32
BF16 = jnp.bfloat16

V7X_VMEM_BYTES = 64 * 1024 * 1024
VMEM_LIMIT = V7X_VMEM_BYTES - 8 * 1024 * 1024
LANES = 128
SUBLANES = 8

D_MODEL = 2048
GRID_W = 64
HY_WIDTH = 3 * D_MODEL // 4
S5_WIDTH = D_MODEL - HY_WIDTH
S5_GROUP = 16
S5_GROUPS = S5_WIDTH // S5_GROUP
S5_STATE = 64
HY_ORDER = 2
HY_BANDS = 16
HY_FILTER_HIDDEN = 64
HY_DECAY_TARGET = 1e-2
HY_SHORT_DECAY_PCT = 0.3
HY_LONG_DECAY_PCT = 1.5
RET_WIDTH = D_MODEL // 2
RET_HEADS = 4
RET_HEAD_DIM = RET_WIDTH // RET_HEADS
ROPE_BASE = 10000.0
NA_WIDTH = D_MODEL - RET_WIDTH
NA_HEADS = 16
NA_HEAD_DIM = NA_WIDTH // NA_HEADS
NA_WIN_ROWS = 8
NA_WIN_COLS = 16
XA_HEADS = 4
XA_HEAD_DIM = D_MODEL // XA_HEADS
RMS_EPS = 1e-6
GN_EPS = 1e-6

FFT_N2 = 128
FFT_RB = SUBLANES
S5_T = 32
RET_CC = 256
RET_SUB = 2
NA_HG = 4
NA_RB = 8
NEG_INF = -1e30


def _cp(sem, vmem=VMEM_LIMIT):
    return pltpu.CompilerParams(dimension_semantics=sem, vmem_limit_bytes=vmem)


def _rms(x, g, eps=RMS_EPS):
    return x * lax.rsqrt(jnp.mean(x * x, axis=-1, keepdims=True) + eps) * g


def _norm_mm_kernel(x_ref, g_ref, w_ref, o_ref, xn_ref):
    @pl.when(pl.program_id(1) == 0)
    def _():
        xn_ref[...] = _rms(x_ref[...], g_ref[...]).astype(BF16)

    o_ref[...] = jnp.dot(xn_ref[...], w_ref[...], preferred_element_type=F32).astype(o_ref.dtype)


def norm_matmul(x, g, w, out_dtype, col0=0, ncols=None, tm=1024, tn=512, name="norm_matmul"):
    M, K = x.shape
    N = w.shape[1] - col0 if ncols is None else ncols
    tm, tn = min(tm, M), min(tn, N)
    assert M % tm == 0 and N % tn == 0 and col0 % tn == 0
    cb0 = col0 // tn
    return pl.pallas_call(
        _norm_mm_kernel,
        out_shape=jax.ShapeDtypeStruct((M, N), out_dtype),
        grid=(M // tm, N // tn),
        in_specs=[pl.BlockSpec((tm, K), lambda i, j: (i, 0)),
                  pl.BlockSpec((1, K), lambda i, j: (0, 0)),
                  pl.BlockSpec((K, tn), lambda i, j: (0, cb0 + j))],
        out_specs=pl.BlockSpec((tm, tn), lambda i, j: (i, j)),
        scratch_shapes=[pltpu.VMEM((tm, K), BF16)],
        compiler_params=_cp(("parallel", "arbitrary")),
        name=name,
    )(x, g.reshape(1, K), w)


def _mm_norm_res_kernel(*refs, n_ops, prenorm):
    pos = 0
    y = None
    for t in range(n_ops):
        a = refs[pos][...]
        w_ref = refs[pos + 1]
        pos += 2
        if prenorm[t]:
            a = _rms(a.astype(F32), refs[pos][...])
            pos += 1
        d = jnp.dot(a.astype(BF16), w_ref[...], preferred_element_type=F32)
        y = d if y is None else y + d
    g_ref, x_ref, o_ref = refs[pos], refs[pos + 1], refs[pos + 2]
    o_ref[...] = x_ref[...] + _rms(y, g_ref[...])


def matmul_norm_residual(ops, g, x, tm=512, name="matmul_norm_residual"):
    M, N = x.shape
    tm = min(tm, M)
    assert M % tm == 0
    args, specs, prenorm = [], [], []
    for a, w, pg, k0 in ops:
        kt = a.shape[1]
        assert k0 % kt == 0
        args += [a, w]
        specs += [pl.BlockSpec((tm, kt), lambda i: (i, 0)), pl.BlockSpec((kt, N), lambda i, kb=k0 // kt: (kb, 0))]
        prenorm.append(pg is not None)
        if pg is not None:
            args.append(pg.reshape(1, kt))
            specs.append(pl.BlockSpec((1, kt), lambda i: (0, 0)))
    args += [g.reshape(1, N), x]
    specs += [pl.BlockSpec((1, N), lambda i: (0, 0)), pl.BlockSpec((tm, N), lambda i: (i, 0))]
    return pl.pallas_call(
        functools.partial(_mm_norm_res_kernel, n_ops=len(ops), prenorm=tuple(prenorm)),
        out_shape=jax.ShapeDtypeStruct((M, N), F32),
        grid=(M // tm,),
        in_specs=specs,
        out_specs=pl.BlockSpec((tm, N), lambda i: (i, 0)),
        compiler_params=_cp(("parallel",)),
        name=name,
    )(*args)


def _ffn_kernel(x_ref, gi_ref, wg_ref, wu_ref, wd_ref, go_ref, o_ref, xn_ref):
    j = pl.program_id(1)

    @pl.when(j == 0)
    def _():
        xn_ref[...] = _rms(x_ref[...], gi_ref[...]).astype(BF16)
        o_ref[...] = jnp.zeros_like(o_ref)

    xn = xn_ref[...]
    a = jnp.dot(xn, wg_ref[...], preferred_element_type=F32)
    u = jnp.dot(xn, wu_ref[...], preferred_element_type=F32)
    h = (a * jax.nn.sigmoid(a) * u).astype(BF16)
    o_ref[...] += jnp.dot(h, wd_ref[...], preferred_element_type=F32)

    @pl.when(j == pl.num_programs(1) - 1)
    def _():
        o_ref[...] = x_ref[...] + _rms(o_ref[...], go_ref[...])


def ffn_block(x, g_in, wg, wu, wd, g_out, tm=1024, th=512):
    M, D = x.shape
    Hd = wg.shape[1]
    tm, th = min(tm, M), min(th, Hd)
    assert M % tm == 0 and Hd % th == 0
    return pl.pallas_call(
        _ffn_kernel,
        out_shape=jax.ShapeDtypeStruct((M, D), F32),
        grid=(M // tm, Hd // th),
        in_specs=[pl.BlockSpec((tm, D), lambda i, j: (i, 0), pipeline_mode=pl.Buffered(1)),
                  pl.BlockSpec((1, D), lambda i, j: (0, 0)),
                  pl.BlockSpec((D, th), lambda i, j: (0, j)),
                  pl.BlockSpec((D, th), lambda i, j: (0, j)),
                  pl.BlockSpec((th, D), lambda i, j: (j, 0)),
                  pl.BlockSpec((1, D), lambda i, j: (0, 0))],
        out_specs=pl.BlockSpec((tm, D), lambda i, j: (i, 0)),
        scratch_shapes=[pltpu.VMEM((tm, D), BF16)],
        compiler_params=_cp(("parallel", "arbitrary")),
        name="ffn_block",
    )(x, g_in.reshape(1, D), wg, wu, wd, g_out.reshape(1, D))


def _xattn_kernel(q_ref, k_ref, v_ref, o_ref, *, heads, dh):
    scale = dh ** -0.5
    for h in range(heads):
        sl = slice(h * dh, (h + 1) * dh)
        s = lax.dot_general(q_ref[:, sl], k_ref[:, sl], (((1,), (1,)), ((), ())),
                            preferred_element_type=F32) * scale
        m = jnp.max(s, axis=-1, keepdims=True)
        p = jnp.exp(s - m)
        l = jnp.sum(p, axis=-1, keepdims=True)
        o = jnp.dot(p.astype(BF16), v_ref[:, sl], preferred_element_type=F32)
        o_ref[:, sl] = (o / l).astype(o_ref.dtype)


def cross_attention(q, kv, B, L, n_mem, tm=512):
    M, D = q.shape
    tm = min(tm, L)
    assert L % tm == 0
    bpl = L // tm
    return pl.pallas_call(
        functools.partial(_xattn_kernel, heads=XA_HEADS, dh=D // XA_HEADS),
        out_shape=jax.ShapeDtypeStruct((M, D), BF16),
        grid=(M // tm,),
        in_specs=[pl.BlockSpec((tm, D), lambda i: (i, 0)),
                  pl.BlockSpec((n_mem, D), lambda i: (i // bpl, 0)),
                  pl.BlockSpec((n_mem, D), lambda i: (i // bpl, 1))],
        out_specs=pl.BlockSpec((tm, D), lambda i: (i, 0)),
        compiler_params=_cp(("parallel",)),
        name="cross_attention",
    )(q, kv, kv)


def _short_conv_kernel(xp_ref, xc_ref, xn_ref, w_ref, b_ref, o_ref, *, blocks_per_seq):
    li = pl.program_id(0) % blocks_per_seq
    x = xc_ref[...]
    tl = x.shape[0]
    prev = jnp.where(li == 0, 0.0, xp_ref[SUBLANES - 1:SUBLANES, :])
    nxt = jnp.where(li == blocks_per_seq - 1, 0.0, xn_ref[0:1, :])
    row = lax.broadcasted_iota(jnp.int32, x.shape, 0)
    up = jnp.where(row == 0, prev, pltpu.roll(x, 1, axis=0))
    dn = jnp.where(row == tl - 1, nxt, pltpu.roll(x, tl - 1, axis=0))
    o_ref[...] = up * w_ref[0:1, :] + x * w_ref[1:2, :] + dn * w_ref[2:3, :] + b_ref[...]


def short_conv(p, w, b, L, tl=1024, W=512):
    M = p.shape[0]
    C = w.shape[1]
    tl, W = min(tl, L), min(W, C)
    assert L % tl == 0 and C % W == 0 and tl % SUBLANES == 0
    sb = tl // SUBLANES
    nb8 = M // SUBLANES
    return pl.pallas_call(
        functools.partial(_short_conv_kernel, blocks_per_seq=L // tl),
        out_shape=jax.ShapeDtypeStruct((M, C), F32),
        grid=(M // tl, C // W),
        in_specs=[pl.BlockSpec((SUBLANES, W), lambda i, c: (jnp.maximum(i * sb - 1, 0), c)),
                  pl.BlockSpec((tl, W), lambda i, c: (i, c)),
                  pl.BlockSpec((SUBLANES, W), lambda i, c: (jnp.minimum((i + 1) * sb, nb8 - 1), c)),
                  pl.BlockSpec((3, W), lambda i, c: (0, c)),
                  pl.BlockSpec((1, W), lambda i, c: (0, c))],
        out_specs=pl.BlockSpec((tl, W), lambda i, c: (i, c)),
        compiler_params=_cp(("parallel", "parallel")),
        name="hyena_short_conv",
    )(p, p, p, w, b.reshape(1, C))


def _filter_kernel(bands_ref, w1t_ref, w1c_ref, w1s_ref, b1_ref, f_ref, w2_ref, b2_ref, w3_ref, dl_ref, o_ref,
                   hid_ref, *, L, ncb):
    hi = lax.Precision.HIGHEST
    s = pl.program_id(1)
    tl = o_ref.shape[0]
    m = pl.program_id(0) * tl + lax.broadcasted_iota(jnp.int32, (tl, 1), 0)

    @pl.when(s == 0)
    def _():
        for d, pos in enumerate((m, L - m)):
            t = pos.astype(F32) * (1.0 / L)
            ang = (2.0 * math.pi) * t * bands_ref[...]
            pre = (t * w1t_ref[...]
                   + jnp.dot(jnp.cos(ang), w1c_ref[...], preferred_element_type=F32, precision=hi)
                   + jnp.dot(jnp.sin(ang), w1s_ref[...], preferred_element_type=F32, precision=hi)
                   + b1_ref[...])
            h = jnp.sin(f_ref[0:1, :] * pre)
            hid_ref[d] = jnp.sin(f_ref[1:2, :] * (jnp.dot(h, w2_ref[...], preferred_element_type=F32, precision=hi)
                                                  + b2_ref[...]))

    d = (s // ncb) % 2
    t = jnp.where(d == 0, m, L - m).astype(F32) * (1.0 / L)
    out = jnp.dot(hid_ref[d], w3_ref[...], preferred_element_type=F32, precision=hi)
    out = out * jnp.exp(-t * dl_ref[...])
    o_ref[...] = jnp.where(m == jnp.where(d == 1, 0, -1), 0.0, out)


def hyena_filters(w1, b1, freq, w2, b2, w3, L, tl=512, W=512):
    C = HY_WIDTH
    Hh = HY_FILTER_HIDDEN
    tl = min(tl, L)
    bands = jnp.zeros((1, LANES), F32).at[0, :HY_BANDS].set(jnp.arange(1, HY_BANDS + 1, dtype=F32))
    w1 = w1.astype(F32)
    w1t = w1[0:1]
    w1c = jnp.zeros((LANES, Hh), F32).at[:HY_BANDS].set(w1[1:1 + HY_BANDS])
    w1s = jnp.zeros((LANES, Hh), F32).at[:HY_BANDS].set(w1[1 + HY_BANDS:])
    deltas = jnp.abs(jnp.linspace(math.log(HY_DECAY_TARGET) / HY_LONG_DECAY_PCT,
                                  math.log(HY_DECAY_TARGET) / HY_SHORT_DECAY_PCT, C, dtype=F32)).reshape(1, C)
    ncb = C // W
    small = lambda shape: pl.BlockSpec(shape, lambda i, s: (0, 0))
    return pl.pallas_call(
        functools.partial(_filter_kernel, L=L, ncb=ncb),
        out_shape=jax.ShapeDtypeStruct((2 * HY_ORDER, L, C), F32),
        grid=(L // tl, 2 * HY_ORDER * ncb),
        in_specs=[small((1, LANES)), small((1, Hh)), small((LANES, Hh)), small((LANES, Hh)), small((1, Hh)),
                  small((2, Hh)), small((Hh, Hh)), small((1, Hh)),
                  pl.BlockSpec((Hh, W), lambda i, s: (0, s)),
                  pl.BlockSpec((1, W), lambda i, s: (0, s % ncb))],
        out_specs=pl.BlockSpec((None, tl, W), lambda i, s: (s // ncb, i, s % ncb)),
        scratch_shapes=[pltpu.VMEM((2, tl, Hh), F32)],
        compiler_params=_cp(("parallel", "arbitrary")),
        name="hyena_filters",
    )(bands, w1t, w1c, w1s, b1.astype(F32).reshape(1, Hh), freq.astype(F32), w2.astype(F32),
      b2.astype(F32).reshape(1, Hh), w3.astype(F32), deltas)


def _fft_tables(L):
    N2 = FFT_N2
    N = 2 * L
    N1 = N // N2
    N1h = N1 // 2
    n2 = jnp.arange(N2, dtype=jnp.int32)[:, None, None]
    k1 = jnp.arange(N1, dtype=jnp.int32)[None, :, None]
    n1 = jnp.arange(N1h, dtype=jnp.int32)[None, None, :]
    ph = ((n1 * N2 + n2) * k1) % N
    ang = ph.astype(F32) * (2.0 * math.pi / N)
    gr, gi = jnp.cos(ang), -jnp.sin(ang)
    g_fwd = jnp.concatenate([jnp.concatenate([gr, -gi], 2), jnp.concatenate([gi, gr], 2)], 1).astype(BF16)
    sgn = jnp.where(k1 % 2 == 0, 1.0, -1.0).astype(F32)
    g_flt = jnp.concatenate([jnp.concatenate([gr, sgn * gr], 2), jnp.concatenate([gi, sgn * gi], 2)], 1).astype(BF16)
    er = jnp.swapaxes(gr, 1, 2) / N
    ei = -jnp.swapaxes(gi, 1, 2) / N
    g_inv = jnp.concatenate([jnp.concatenate([er, -ei], 2), jnp.concatenate([ei, er], 2)], 1).astype(BF16)
    a2 = (jnp.arange(N2, dtype=jnp.int32)[:, None] * jnp.arange(N2, dtype=jnp.int32)[None, :]) % N2
    ang2 = a2.astype(F32) * (2.0 * math.pi / N2)
    fr, fi = jnp.cos(ang2), -jnp.sin(ang2)
    f_mid = jnp.concatenate([jnp.concatenate([fr, -fi], 1), jnp.concatenate([fi, fr], 1)], 0).astype(BF16)
    f_mid_inv = jnp.concatenate([jnp.concatenate([fr, fi], 1), jnp.concatenate([-fi, fr], 1)], 0).astype(BF16)
    return dict(g_fwd=g_fwd, g_flt=g_flt, g_inv=g_inv, f_mid=f_mid, f_mid_inv=f_mid_inv, N1=N1)


def _pack_complex(re, im):
    r = lax.bitcast_convert_type(re.astype(BF16).astype(F32), jnp.uint32)
    i = lax.bitcast_convert_type(im.astype(BF16).astype(F32), jnp.uint32)
    return r | (i >> 16)


def _unpack_complex(w):
    re = lax.bitcast_convert_type(w & jnp.uint32(0xFFFF0000), F32)
    im = lax.bitcast_convert_type(w << 16, F32)
    return re, im


def _fft_a_kernel(x_ref, g_ref, z_ref):
    n1 = z_ref.shape[0]
    for r in range(FFT_RB):
        xs = jnp.concatenate([x_ref[0, :, r, :], x_ref[1, :, r, :]], axis=0).astype(BF16)
        a = jnp.dot(g_ref[r], xs, preferred_element_type=F32)
        z_ref[:, r, :] = _pack_complex(a[:n1], a[n1:])


def fft_stage_a(x4, col0, C, g, W=512):
    B, N1h, N2, _ = x4.shape
    N1 = 2 * N1h
    W = min(W, C)
    assert B % 2 == 0 and C % W == 0 and col0 % W == 0 and N2 % FFT_RB == 0
    cb0 = col0 // W
    return pl.pallas_call(
        _fft_a_kernel,
        out_shape=jax.ShapeDtypeStruct((B // 2, N1, N2, C), jnp.uint32),
        grid=(B // 2, N2 // FFT_RB, C // W),
        in_specs=[pl.BlockSpec((2, N1h, FFT_RB, W), lambda p, j, c: (p, 0, j, cb0 + c)),
                  pl.BlockSpec((FFT_RB, 2 * N1, N1), lambda p, j, c: (j, 0, 0))],
        out_specs=pl.BlockSpec((None, N1, FFT_RB, W), lambda p, j, c: (p, 0, j, c)),
        compiler_params=_cp(("parallel", "parallel", "parallel")),
        name="hyena_fft_a",
    )(x4, g)


def _fft_mid_kernel(*refs, kb, with_filter):
    if with_filter:
        z_ref, h_ref, f_ref, fi_ref, y_ref = refs
    else:
        z_ref, f_ref, y_ref = refs
    n2 = z_ref.shape[1]
    for k in range(kb):
        zr, zi = _unpack_complex(z_ref[k])
        zs = jnp.concatenate([zr, zi], axis=0).astype(BF16)
        x = jnp.dot(f_ref[...], zs, preferred_element_type=F32)
        if with_filter:
            xr, xi = x[:n2], x[n2:]
            hr, hi = _unpack_complex(h_ref[k])
            ys = jnp.concatenate([xr * hr - xi * hi, xr * hi + xi * hr], axis=0).astype(BF16)
            x = jnp.dot(fi_ref[...], ys, preferred_element_type=F32)
        y_ref[k] = _pack_complex(x[:n2], x[n2:])


def fft_stage_mid(z, tabs, h=None, order=0, kb=8, W=512):
    P, N1, N2, C = z.shape
    kb, W = min(kb, N1), min(W, C)
    assert N1 % kb == 0 and C % W == 0
    blk = pl.BlockSpec((None, kb, N2, W), lambda k, c, p: (p, k, 0, c))
    mat = pl.BlockSpec((2 * N2, 2 * N2), lambda k, c, p: (0, 0))
    if h is None:
        args, specs = (z, tabs["f_mid"]), [blk, mat]
    else:
        hblk = pl.BlockSpec((None, kb, N2, W), lambda k, c, p: (order, k, 0, c))
        args, specs = (z, h, tabs["f_mid"], tabs["f_mid_inv"]), [blk, hblk, mat, mat]
    return pl.pallas_call(
        functools.partial(_fft_mid_kernel, kb=kb, with_filter=h is not None),
        out_shape=jax.ShapeDtypeStruct(z.shape, jnp.uint32),
        grid=(N1 // kb, C // W, P),
        in_specs=specs,
        out_specs=blk,
        compiler_params=_cp(("parallel", "parallel", "parallel")),
        name="hyena_fft_mid",
    )(*args)


def _fft_c_kernel(y_ref, g_ref, v_ref, x_ref, s_ref, o_ref):
    n1h = v_ref.shape[1]
    for r in range(FFT_RB):
        yr, yi = _unpack_complex(y_ref[:, r, :])
        ys = jnp.concatenate([yr, yi], axis=0).astype(BF16)
        c = jnp.dot(g_ref[r], ys, preferred_element_type=F32)
        for b in range(2):
            conv = c[b * n1h:(b + 1) * n1h]
            o_ref[b, :, r, :] = x_ref[b, :, r, :] * (conv + s_ref[...] * v_ref[b, :, r, :])


def fft_stage_c(y, g, v4, vcol0, x4, xcol0, skip, W=512):
    P, N1, N2, C = y.shape
    N1h = N1 // 2
    W = min(W, C)
    assert C % W == 0 and vcol0 % W == 0 and xcol0 % W == 0
    vb, xb = vcol0 // W, xcol0 // W
    return pl.pallas_call(
        _fft_c_kernel,
        out_shape=jax.ShapeDtypeStruct((2 * P, N1h, N2, C), F32),
        grid=(P, N2 // FFT_RB, C // W),
        in_specs=[pl.BlockSpec((None, N1, FFT_RB, W), lambda p, j, c: (p, 0, j, c)),
                  pl.BlockSpec((FFT_RB, N1, 2 * N1), lambda p, j, c: (j, 0, 0)),
                  pl.BlockSpec((2, N1h, FFT_RB, W), lambda p, j, c: (p, 0, j, vb + c)),
                  pl.BlockSpec((2, N1h, FFT_RB, W), lambda p, j, c: (p, 0, j, xb + c)),
                  pl.BlockSpec((1, W), lambda p, j, c: (0, c))],
        out_specs=pl.BlockSpec((2, N1h, FFT_RB, W), lambda p, j, c: (p, 0, j, c)),
        compiler_params=_cp(("parallel", "parallel", "parallel")),
        name="hyena_fft_c",
    )(y, g, v4, x4, skip.reshape(1, C))


def hyena_mixer(p, B, L, short_w, short_b, w1, b1, freq, w2, b2, w3, skip):
    C = HY_WIDTH
    N2 = FFT_N2
    N1h = L // N2
    tabs = _fft_tables(L)
    skip = skip.astype(F32)
    uc = short_conv(p, short_w.astype(F32), short_b.astype(F32), L)
    uc4 = uc.reshape(B, N1h, N2, 3 * C)
    filt = hyena_filters(w1, b1, freq, w2, b2, w3, L)
    spec = fft_stage_mid(fft_stage_a(filt.reshape(2 * HY_ORDER, N1h, N2, C), 0, C, tabs["g_flt"]), tabs)
    y = fft_stage_mid(fft_stage_a(uc4, 0, C, tabs["g_fwd"]), tabs, h=spec, order=0)
    z1 = fft_stage_c(y, tabs["g_inv"], uc4, 0, uc4, C, skip[0])
    y = fft_stage_mid(fft_stage_a(z1, 0, C, tabs["g_fwd"]), tabs, h=spec, order=1)
    z = fft_stage_c(y, tabs["g_inv"], z1, 0, uc4, 2 * C, skip[1])
    return z.reshape(B * L, C)


def _s5_tables(a_re, a_im, log_dt, b_re, b_im, c_re, c_im, n_chunks):
    T = S5_T
    G, P, I = S5_GROUPS, S5_STATE, S5_GROUP
    f32 = lambda a: a.astype(F32)
    a_re, a_im, b_re, b_im, c_re, c_im = map(f32, (a_re, a_im, b_re, b_im, c_re, c_im))
    step = jnp.exp(f32(log_dt))[..., None]
    lr, li = a_re * step, a_im * step
    mag = jnp.exp(lr)
    br_, bi_ = mag * jnp.cos(li), mag * jnp.sin(li)
    den = a_re * a_re + a_im * a_im
    qr = ((br_ - 1.0) * a_re + bi_ * a_im) / den
    qi = (bi_ * a_re - (br_ - 1.0) * a_im) / den
    bbr = qr[..., None] * b_re - qi[..., None] * b_im
    bbi = qr[..., None] * b_im + qi[..., None] * b_re

    def lam_pow(k):
        k = k.astype(F32)
        m = jnp.exp(lr[..., None] * k)
        return m * jnp.cos(li[..., None] * k), m * jnp.sin(li[..., None] * k)

    lags = jnp.arange(T + 1)
    pr, pi_ = lam_pow(lags)
    cpr = c_re[..., None] * pr[:, :, None] - c_im[..., None] * pi_[:, :, None]
    cpi = c_re[..., None] * pi_[:, :, None] + c_im[..., None] * pr[:, :, None]
    kern = jnp.einsum("dgjpk,dgpi->dgkji", cpr, bbr) - jnp.einsum("dgjpk,dgpi->dgkji", cpi, bbi)
    tt = jnp.arange(T)
    lag = tt[:, None] - tt[None, :]
    kf = kern[0][:, jnp.clip(lag, 0, T)]
    kb = kern[1][:, jnp.clip(-lag, 0, T)]
    m_f = (lag >= 0)[None, :, :, None, None]
    m_b = (lag <= 0)[None, :, :, None, None]
    toep = jnp.where(m_f, kf, 0.0) + jnp.where(m_b, kb, 0.0)
    toep = toep.transpose(0, 1, 3, 2, 4).reshape(G, T * I, T * I)
    ef = T - 1 - tt
    eb = tt

    def bst(d, e):
        wr, wi = pr[d][..., e], pi_[d][..., e]
        re = wr[..., None] * bbr[d][:, :, None] - wi[..., None] * bbi[d][:, :, None]
        im = wr[..., None] * bbi[d][:, :, None] + wi[..., None] * bbr[d][:, :, None]
        return jnp.concatenate([re, im], axis=1).reshape(G, 2 * P, T * I)

    bst_all = jnp.stack([bst(0, ef), bst(1, eb)], axis=1)

    def cst(d, e):
        xr, xi = cpr[d][..., e], cpi[d][..., e]
        m = jnp.concatenate([xr, -xi], axis=2)
        return m.transpose(0, 3, 1, 2).reshape(G, T * I, 2 * P)

    cst_all = jnp.stack([cst(0, tt + 1), cst(1, T - tt)], axis=1)
    nsteps = max(1, int(math.log2(n_chunks)))
    e2 = T * (2 ** jnp.arange(nsteps))
    ar, ai = lam_pow(e2)
    ap = jnp.stack([ar, ai], axis=-1).transpose(1, 0, 3, 4, 2)
    ap = jnp.broadcast_to(ap[..., None], ap.shape + (LANES,))
    return toep.astype(BF16), bst_all.astype(BF16), cst_all.astype(BF16), ap, nsteps


def _s5_kernel(u_ref, toep_ref, bst_ref, cst_ref, ap_ref, y_ref, *, n_chunks, nsteps):
    T, I, cols = u_ref.shape
    P = S5_STATE
    u = u_ref[...].reshape(T * I, cols).astype(BF16)
    y = jnp.dot(toep_ref[...], u, preferred_element_type=F32)
    cidx = lax.broadcasted_iota(jnp.int32, (P, cols), 1) % n_chunks

    def shifted(x, sh, d):
        if d == 0:
            return jnp.where(cidx >= sh, pltpu.roll(x, sh, axis=1), 0.0)
        return jnp.where(cidx < n_chunks - sh, pltpu.roll(x, cols - sh, axis=1), 0.0)

    for d in range(2):
        v = jnp.dot(bst_ref[d], u, preferred_element_type=F32)
        sr, si = v[:P], v[P:]
        for j in range(nsteps):
            if (1 << j) >= n_chunks:
                break
            ar, ai = ap_ref[d, j, 0][:, :1], ap_ref[d, j, 1][:, :1]
            rr, ri = shifted(sr, 1 << j, d), shifted(si, 1 << j, d)
            sr, si = sr + ar * rr - ai * ri, si + ar * ri + ai * rr
        s_in = jnp.concatenate([shifted(sr, 1, d), shifted(si, 1, d)], axis=0).astype(BF16)
        y = y + jnp.dot(cst_ref[d], s_in, preferred_element_type=F32)
    y_ref[...] = y.reshape(T, I, cols)


def _s5_out_kernel(y_ref, u_ref, d_ref, w_ref, o_ref):
    y = y_ref[...] + d_ref[...] * u_ref[...]
    y = jax.nn.gelu(y, approximate=True)
    z = jnp.dot(y.astype(BF16), w_ref[...], preferred_element_type=F32)
    o_ref[...] = (y * jax.nn.sigmoid(z)).astype(o_ref.dtype)


def s5_mixer(p, col0, B, L, a_re, a_im, log_dt, b_re, b_im, c_re, c_im, d, w_glu, tm=1024):
    T, Wd, I, G = S5_T, S5_WIDTH, S5_GROUP, S5_GROUPS
    nC = L // T
    cols = B * nC
    assert L % T == 0 and nC & (nC - 1) == 0 and cols % LANES == 0
    toep, bst, cst, ap, nsteps = _s5_tables(a_re, a_im, log_dt, b_re, b_im, c_re, c_im, nC)
    u = p[:, col0:col0 + Wd]
    ut = u.reshape(B, nC, T, Wd).transpose(2, 3, 0, 1).reshape(T, Wd, cols)
    blk = pl.BlockSpec((T, I, cols), lambda g: (0, g, 0))
    yt = pl.pallas_call(
        functools.partial(_s5_kernel, n_chunks=nC, nsteps=nsteps),
        out_shape=jax.ShapeDtypeStruct((T, Wd, cols), F32),
        grid=(G,),
        in_specs=[blk,
                  pl.BlockSpec((None, T * I, T * I), lambda g: (g, 0, 0)),
                  pl.BlockSpec((None, 2, 2 * S5_STATE, T * I), lambda g: (g, 0, 0, 0)),
                  pl.BlockSpec((None, 2, T * I, 2 * S5_STATE), lambda g: (g, 0, 0, 0)),
                  pl.BlockSpec((None, 2, nsteps, 2, S5_STATE, LANES), lambda g: (g, 0, 0, 0, 0, 0))],
        out_specs=blk,
        compiler_params=_cp(("parallel",)),
        name="s5_scan",
    )(ut, toep, bst, cst, ap)
    y = yt.reshape(T, Wd, B, nC).transpose(2, 3, 0, 1).reshape(B * L, Wd)
    M = B * L
    tm = min(tm, M)
    cb = col0 // Wd
    assert col0 % Wd == 0 and M % tm == 0
    return pl.pallas_call(
        _s5_out_kernel,
        out_shape=jax.ShapeDtypeStruct((M, Wd), BF16),
        grid=(M // tm,),
        in_specs=[pl.BlockSpec((tm, Wd), lambda i: (i, 0)),
                  pl.BlockSpec((tm, Wd), lambda i: (i, cb)),
                  pl.BlockSpec((1, Wd), lambda i: (0, 0)),
                  pl.BlockSpec((Wd, Wd), lambda i: (0, 0))],
        out_specs=pl.BlockSpec((tm, Wd), lambda i: (i, 0)),
        compiler_params=_cp(("parallel",)),
        name="s5_glu",
    )(y, p, d.astype(F32).reshape(1, Wd), w_glu.astype(BF16))


def _ret_kernel(q_ref, k_ref, v_ref, g_ref, cos_ref, sin_ref, dm_ref, qs_ref, ks_ref, cd_ref, o_ref, st_ref,
                carry_ref, *, n_chunks):
    c = pl.program_id(2)
    nC = n_chunks
    nS = nC // RET_SUB
    Cc = dm_ref.shape[0]
    dh = k_ref.shape[1]
    h2 = dh // 2

    def rows(ref, u):
        return ref[u * Cc:(u + 1) * Cc, :]

    def rot(x, u):
        cos, sin = rows(cos_ref, u), rows(sin_ref, u)
        x1, x2 = x[:, :h2], x[:, h2:]
        return jnp.concatenate([x1 * cos - x2 * sin, x1 * sin + x2 * cos], axis=-1)

    @pl.when(c < nS)
    def _():
        for u in range(RET_SUB):
            kt = (rot(rows(k_ref, u), u) * (dh ** -0.5)).T
            v = rows(v_ref, u).astype(BF16)
            for d in range(2):
                st_ref[d, c * RET_SUB + u] = jnp.dot((kt * ks_ref[d]).astype(BF16), v, preferred_element_type=F32)

    @pl.when(c == nS)
    def _():
        for d in range(2):
            cd = cd_ref[d, 0:1, 0:1]
            carry_ref[...] = jnp.zeros_like(carry_ref)

            def body(i, carry, d=d, cd=cd):
                idx = i if d == 0 else nC - 1 - i
                t = st_ref[d, idx]
                st_ref[d, idx] = carry_ref[...]
                carry_ref[...] = carry_ref[...] * cd + t
                return carry

            lax.fori_loop(0, nC, body, 0)

    @pl.when(c >= nS)
    def _():
        for u in range(RET_SUB):
            ch = (c - nS) * RET_SUB + u
            qr = rot(rows(q_ref, u), u)
            kr = rot(rows(k_ref, u), u) * (dh ** -0.5)
            v = rows(v_ref, u).astype(BF16)
            s = lax.dot_general(qr.astype(BF16), kr.astype(BF16), (((1,), (1,)), ((), ())),
                                preferred_element_type=F32) * dm_ref[...]
            o = jnp.dot(s.astype(BF16), v, preferred_element_type=F32)
            for d in range(2):
                o = o + jnp.dot((qr * qs_ref[d]).astype(BF16), st_ref[d, ch].astype(BF16),
                                preferred_element_type=F32)
            mu = jnp.mean(o, axis=-1, keepdims=True)
            oc = o - mu
            var = jnp.mean(oc * oc, axis=-1, keepdims=True)
            g = rows(g_ref, u)
            o_ref[u * Cc:(u + 1) * Cc, :] = (oc * lax.rsqrt(var + GN_EPS)
                                             * (g * jax.nn.sigmoid(g))).astype(o_ref.dtype)


def retention_mixer(p, B, L, ret_decay):
    H, dh = RET_HEADS, RET_HEAD_DIM
    Cc = min(RET_CC, L)
    nC = L // Cc
    assert L % Cc == 0
    lg = -jnp.exp(ret_decay.astype(F32))
    pos = jnp.arange(Cc, dtype=F32)
    rel = pos[:, None] - pos[None, :]
    lf, lb = lg[0][:, None, None], lg[1][:, None, None]
    dm = jnp.where(rel >= 0, jnp.exp(jnp.maximum(rel, 0.0) * lf), jnp.exp(jnp.maximum(-rel, 0.0) * lb))
    qs = jnp.stack([jnp.exp((pos + 1.0)[None] * lg[0][:, None]),
                    jnp.exp((Cc - pos)[None] * lg[1][:, None])], axis=1)[..., None]
    ks = jnp.stack([jnp.exp((Cc - 1.0 - pos)[None] * lg[0][:, None]),
                    jnp.exp(pos[None] * lg[1][:, None])], axis=1)[:, :, None, :]
    cd = jnp.broadcast_to(jnp.exp(Cc * lg).T[:, :, None, None], (H, 2, SUBLANES, LANES))
    inv = ROPE_BASE ** (-jnp.arange(0, dh, 2, dtype=F32) / dh)
    ang = jnp.arange(L, dtype=F32)[:, None] * inv[None, :]
    cos, sin = jnp.cos(ang), jnp.sin(ang)

    assert nC % RET_SUB == 0
    nS = nC // RET_SUB
    rb = RET_SUB * Cc

    def kch(c):
        return jnp.where(c < nS, c, c - nS)

    def qch(c):
        return jnp.maximum(c - nS, 0)

    kv = lambda off: pl.BlockSpec((rb, dh), lambda b, h, c: (b * nS + kch(c), off * H + h))
    qo = lambda off: pl.BlockSpec((rb, dh), lambda b, h, c: (b * nS + qch(c), off * H + h))
    tab = pl.BlockSpec((rb, dh // 2), lambda b, h, c: (kch(c), 0))
    return pl.pallas_call(
        functools.partial(_ret_kernel, n_chunks=nC),
        out_shape=jax.ShapeDtypeStruct((B * L, H * dh), BF16),
        grid=(B, H, 2 * nS),
        in_specs=[qo(0), kv(1), kv(2), qo(3), tab, tab,
                  pl.BlockSpec((None, Cc, Cc), lambda b, h, c: (h, 0, 0)),
                  pl.BlockSpec((None, 2, Cc, 1), lambda b, h, c: (h, 0, 0, 0)),
                  pl.BlockSpec((None, 2, 1, Cc), lambda b, h, c: (h, 0, 0, 0)),
                  pl.BlockSpec((None, 2, SUBLANES, LANES), lambda b, h, c: (h, 0, 0, 0))],
        out_specs=qo(0),
        scratch_shapes=[pltpu.VMEM((2, nC, dh, dh), F32), pltpu.VMEM((dh, dh), F32)],
        compiler_params=_cp(("parallel", "parallel", "arbitrary")),
        name="retention",
    )(p, p, p, p, cos, sin, dm, qs, ks, cd)


def _na_bias_tables(rpb):
    Wc, WR, WC = GRID_W, NA_WIN_ROWS, NA_WIN_COLS
    hi = lax.Precision.HIGHEST
    rpb = rpb.astype(F32)
    c = jnp.arange(Wc)[:, None]
    kc = jnp.arange(Wc)[None, :]
    cs = jnp.clip(c - WC // 2, 0, Wc - WC)
    valid = (kc >= cs) & (kc < cs + WC)
    csel = (((kc - c + (WC - 1))[:, :, None] == jnp.arange(2 * WC - 1)) & valid[:, :, None]).astype(F32)
    dj = jnp.arange(WR)[None, :] - jnp.arange(WR)[:, None] + (WR - 1)
    rsel = (dj[:, :, None] == jnp.arange(2 * WR - 1)).astype(F32)
    t = jnp.einsum("hrs,cks->hrck", rpb, csel, precision=hi)
    t = jnp.einsum("djr,hrck->hdcjk", rsel, t, precision=hi)
    t = jnp.where(valid[None, None, :, None, :], t, NEG_INF)
    t = t.reshape(NA_HEADS // NA_HG, NA_HG, WR, Wc, WR * Wc).transpose(0, 2, 1, 3, 4)
    return t.reshape(NA_HEADS // NA_HG, WR, NA_HG * Wc, WR * Wc)


def _na_kernel(q_ref, k_ref, v_ref, b_ref, o_ref, *, rows):
    Wc, WR = GRID_W, NA_WIN_ROWS
    hw = NA_HG * NA_HEAD_DIM
    hq = NA_HG * Wc
    scale = NA_HEAD_DIM ** -0.5
    rb = pl.program_id(2)
    own = (lax.broadcasted_iota(jnp.int32, (hq, hw), 0) // Wc
           == lax.broadcasted_iota(jnp.int32, (hq, hw), 1) // NA_HEAD_DIM)

    for i in range(NA_RB):
        r = rb * NA_RB + i
        rs = jnp.clip(r - WR // 2, 0, rows - WR)
        q = q_ref[i * Wc:(i + 1) * Wc, :]
        qs = jnp.where(own, jnp.concatenate([q] * NA_HG, axis=0), jnp.zeros((), q.dtype))
        k0 = pl.multiple_of(rs * Wc, Wc)
        kw = k_ref[pl.ds(k0, WR * Wc), :]
        vw = v_ref[pl.ds(k0, WR * Wc), :]
        s = lax.dot_general(qs, kw, (((1,), (1,)), ((), ())), preferred_element_type=F32) * scale + b_ref[r - rs]
        m = jnp.max(s, axis=-1, keepdims=True)
        e = jnp.exp(s - m)
        l = jnp.sum(e, axis=-1, keepdims=True)
        o = jnp.dot(e.astype(BF16), vw, preferred_element_type=F32) / l
        o = jnp.where(own, o, 0.0)
        out = o[0:Wc]
        for h in range(1, NA_HG):
            out = out + o[h * Wc:(h + 1) * Wc]
        o_ref[i * Wc:(i + 1) * Wc, :] = out.astype(o_ref.dtype)


def neighborhood_mixer(qkv, B, L, rpb):
    Wc = GRID_W
    rows = L // Wc
    assert rows >= NA_WIN_ROWS and rows % NA_RB == 0
    hw = NA_HG * NA_HEAD_DIM
    nhg = NA_HEADS // NA_HG
    bias = _na_bias_tables(rpb)
    nrb = rows // NA_RB
    return pl.pallas_call(
        functools.partial(_na_kernel, rows=rows),
        out_shape=jax.ShapeDtypeStruct((B * L, NA_WIDTH), BF16),
        grid=(B, nhg, nrb),
        in_specs=[pl.BlockSpec((NA_RB * Wc, hw), lambda b, g, r: (b * nrb + r, g)),
                  pl.BlockSpec((L, hw), lambda b, g, r: (b, nhg + g)),
                  pl.BlockSpec((L, hw), lambda b, g, r: (b, 2 * nhg + g)),
                  pl.BlockSpec((None, NA_WIN_ROWS, NA_HG * Wc, NA_WIN_ROWS * Wc), lambda b, g, r: (g, 0, 0, 0))],
        out_specs=pl.BlockSpec((NA_RB * Wc, hw), lambda b, g, r: (b * nrb + r, g)),
        compiler_params=_cp(("parallel", "parallel", "arbitrary")),
        name="neighborhood_attention",
    )(qkv, qkv, qkv, bias)


def _trunk(x, mem, B, L, prm, wb):
    n_mem = mem.shape[0] // B
    depth = prm["norm_g"].shape[0]
    hw3 = 3 * HY_WIDTH
    for layer in range(depth):
        i = layer // 2
        g = prm["norm_g"][layer]
        if layer % 2 == 0:
            p = norm_matmul(x, g[0], wb["ev_w_in"][i], F32, name="even_in_proj")
            z = hyena_mixer(p, B, L, prm["hy_short_w"][i], prm["hy_short_b"][i], prm["hy_w1"][i], prm["hy_b1"][i],
                            prm["hy_freq"][i], prm["hy_w2"][i], prm["hy_b2"][i], prm["hy_w3"][i], prm["hy_skip"][i])
            ss = s5_mixer(p, hw3, B, L, prm["s5_a_re"][i], prm["s5_a_im"][i], prm["s5_log_dt"][i],
                          prm["s5_b_re"][i], prm["s5_b_im"][i], prm["s5_c_re"][i], prm["s5_c_im"][i],
                          prm["s5_d"][i], prm["s5_w_glu"][i])
            wo = wb["mix_wo"][layer]
            ops = [(z, wo, prm["hy_out_g"][i], 0), (ss, wo, None, HY_WIDTH)]
        else:
            w_in = wb["od_w_in"][i]
            pr = norm_matmul(x, g[0], w_in, F32, 0, 4 * RET_WIDTH, name="odd_in_proj_ret")
            pn = norm_matmul(x, g[0], w_in, BF16, 4 * RET_WIDTH, 3 * NA_WIDTH, name="odd_in_proj_na")
            ret = retention_mixer(pr, B, L, prm["ret_decay"][i])
            na = neighborhood_mixer(pn, B, L, prm["na_rpb"][i])
            wo = wb["mix_wo"][layer]
            ops = [(ret, wo, None, 0), (na, wo, None, RET_WIDTH)]
        x = matmul_norm_residual(ops, g[1], x, name="mix_out_proj")
        q = norm_matmul(x, g[2], wb["xa_wq"][layer], BF16, name="xattn_q_proj")
        kv = norm_matmul(mem, prm["mem_norm_g"][layer], wb["xa_wkv"][layer], BF16, name="xattn_kv_proj")
        o = cross_attention(q, kv, B, L, n_mem)
        x = matmul_norm_residual([(o, wb["xa_wo"][layer], None, 0)], g[3], x, name="xattn_out_proj")
        x = ffn_block(x, g[4], wb["ffn_wg"][layer], wb["ffn_wu"][layer], wb["ffn_wd"][layer], g[5])
    return x


def kernel(x_prompt, x_sample, mem_prompt, mem_sample, norm_g, mix_wo, ev_w_in, hy_short_w, hy_short_b, hy_w1, hy_b1, hy_freq, hy_w2, hy_b2, hy_w3, hy_skip, hy_out_g, s5_a_re, s5_a_im, s5_log_dt, s5_b_re, s5_b_im, s5_c_re, s5_c_im, s5_d, s5_w_glu, od_w_in, ret_decay, na_rpb, mem_norm_g, xa_wq, xa_wkv, xa_wo, ffn_wg, ffn_wu, ffn_wd):
    prm = dict(norm_g=norm_g, hy_short_w=hy_short_w, hy_short_b=hy_short_b, hy_w1=hy_w1, hy_b1=hy_b1,
               hy_freq=hy_freq, hy_w2=hy_w2, hy_b2=hy_b2, hy_w3=hy_w3, hy_skip=hy_skip, hy_out_g=hy_out_g,
               s5_a_re=s5_a_re, s5_a_im=s5_a_im, s5_log_dt=s5_log_dt, s5_b_re=s5_b_re, s5_b_im=s5_b_im,
               s5_c_re=s5_c_re, s5_c_im=s5_c_im, s5_d=s5_d, s5_w_glu=s5_w_glu, ret_decay=ret_decay,
               na_rpb=na_rpb, mem_norm_g=mem_norm_g)
    wb = {k: v.astype(BF16) for k, v in dict(mix_wo=mix_wo, ev_w_in=ev_w_in, od_w_in=od_w_in, xa_wq=xa_wq,
                                             xa_wkv=xa_wkv, xa_wo=xa_wo, ffn_wg=ffn_wg, ffn_wu=ffn_wu,
                                             ffn_wd=ffn_wd).items()}
    outs = []
    for x, mem in ((x_prompt, mem_prompt), (x_sample, mem_sample)):
        B, L, D = x.shape
        y = _trunk(x.reshape(B * L, D), mem.reshape(-1, D), B, L, prm, wb)
        outs.append(y.reshape(B, L, D))
    return tuple(outs)
```

```python
import functools
import math

import jax
import jax.numpy as jnp
from jax import lax
from jax.experimental import pallas as pl
from jax.experimental.pallas import tpu as pltpu

F32 = jnp.float32
BF16 = jnp.bfloat16

V7X_VMEM_BYTES = 64 * 1024 * 1024
VMEM_LIMIT = V7X_VMEM_BYTES - 8 * 1024 * 1024
LANES = 128
SUBLANES = 8

D_MODEL = 2048
GRID_W = 64
HY_WIDTH = 3 * D_MODEL // 4
S5_WIDTH = D_MODEL - HY_WIDTH
S5_GROUP = 16
S5_GROUPS = S5_WIDTH // S5_GROUP
S5_STATE = 64
HY_ORDER = 2
HY_BANDS = 16
HY_FILTER_HIDDEN = 64
HY_DECAY_TARGET = 1e-2
HY_SHORT_DECAY_PCT = 0.3
HY_LONG_DECAY_PCT = 1.5
RET_WIDTH = D_MODEL // 2
RET_HEADS = 4
RET_HEAD_DIM = RET_WIDTH // RET_HEADS
ROPE_BASE = 10000.0
NA_WIDTH = D_MODEL - RET_WIDTH
NA_HEADS = 16
NA_HEAD_DIM = NA_WIDTH // NA_HEADS
NA_WIN_ROWS = 8
NA_WIN_COLS = 16
XA_HEADS = 4
XA_HEAD_DIM = D_MODEL // XA_HEADS
RMS_EPS = 1e-6
GN_EPS = 1e-6

FFT_N2 = 128
FFT_RB = SUBLANES
S5_T = 32
RET_CC = 256
RET_SUB = 2
NA_HG = 4
NA_RB = 8
NEG_INF = -1e30


def _cp(sem, vmem=VMEM_LIMIT):
    return pltpu.CompilerParams(dimension_semantics=sem, vmem_limit_bytes=vmem)


def _rms(x, g, eps=RMS_EPS):
    return x * lax.rsqrt(jnp.mean(x * x, axis=-1, keepdims=True) + eps) * g


def _norm_mm_kernel(x_ref, g_ref, w_ref, o_ref, xn_ref):
    @pl.when(pl.program_id(1) == 0)
    def _():
        xn_ref[...] = _rms(x_ref[...], g_ref[...]).astype(BF16)

    o_ref[...] = jnp.dot(xn_ref[...], w_ref[...], preferred_element_type=F32).astype(o_ref.dtype)


def _stacked(w):
    return w if w.ndim == 3 else w[None]


def norm_matmul(x, g, w, out_dtype, col0=0, ncols=None, layer=0, tm=1024, tn=512, name="norm_matmul"):
    M, K = x.shape
    w = _stacked(w)
    N = w.shape[2] - col0 if ncols is None else ncols
    tm, tn = min(tm, M), min(tn, N)
    assert M % tm == 0 and N % tn == 0 and col0 % tn == 0
    cb0 = col0 // tn
    return pl.pallas_call(
        _norm_mm_kernel,
        out_shape=jax.ShapeDtypeStruct((M, N), out_dtype),
        grid=(M // tm, N // tn),
        in_specs=[pl.BlockSpec((tm, K), lambda i, j: (i, 0)),
                  pl.BlockSpec((1, K), lambda i, j: (0, 0)),
                  pl.BlockSpec((None, K, tn), lambda i, j: (layer, 0, cb0 + j))],
        out_specs=pl.BlockSpec((tm, tn), lambda i, j: (i, j)),
        scratch_shapes=[pltpu.VMEM((tm, K), BF16)],
        compiler_params=_cp(("parallel", "arbitrary")),
        name=name,
    )(x, g.reshape(1, K), w)


def _mm_norm_res_kernel(*refs, n_ops, prenorm):
    pos = 0
    y = None
    for t in range(n_ops):
        a = refs[pos][...]
        w_ref = refs[pos + 1]
        pos += 2
        if prenorm[t]:
            a = _rms(a.astype(F32), refs[pos][...])
            pos += 1
        d = jnp.dot(a.astype(BF16), w_ref[...], preferred_element_type=F32)
        y = d if y is None else y + d
    g_ref, x_ref, o_ref = refs[pos], refs[pos + 1], refs[pos + 2]
    o_ref[...] = x_ref[...] + _rms(y, g_ref[...])


def matmul_norm_residual(ops, g, x, layer=0, tm=512, name="matmul_norm_residual"):
    M, N = x.shape
    tm = min(tm, M)
    assert M % tm == 0
    args, specs, prenorm = [], [], []
    for a, w, pg, k0 in ops:
        kt = a.shape[1]
        assert k0 % kt == 0
        args += [a, _stacked(w)]
        specs += [pl.BlockSpec((tm, kt), lambda i: (i, 0)),
                  pl.BlockSpec((None, kt, N), lambda i, kb=k0 // kt: (layer, kb, 0))]
        prenorm.append(pg is not None)
        if pg is not None:
            args.append(pg.reshape(1, kt))
            specs.append(pl.BlockSpec((1, kt), lambda i: (0, 0)))
    args += [g.reshape(1, N), x]
    specs += [pl.BlockSpec((1, N), lambda i: (0, 0)), pl.BlockSpec((tm, N), lambda i: (i, 0))]
    return pl.pallas_call(
        functools.partial(_mm_norm_res_kernel, n_ops=len(ops), prenorm=tuple(prenorm)),
        out_shape=jax.ShapeDtypeStruct((M, N), F32),
        grid=(M // tm,),
        in_specs=specs,
        out_specs=pl.BlockSpec((tm, N), lambda i: (i, 0)),
        compiler_params=_cp(("parallel",)),
        name=name,
    )(*args)


def _ffn_kernel(x_ref, gi_ref, wg_ref, wu_ref, wd_ref, go_ref, o_ref, xn_ref):
    j = pl.program_id(1)

    @pl.when(j == 0)
    def _():
        xn_ref[...] = _rms(x_ref[...], gi_ref[...]).astype(BF16)
        o_ref[...] = jnp.zeros_like(o_ref)

    xn = xn_ref[...]
    a = jnp.dot(xn, wg_ref[...], preferred_element_type=F32)
    u = jnp.dot(xn, wu_ref[...], preferred_element_type=F32)
    h = (a * jax.nn.sigmoid(a) * u).astype(BF16)
    o_ref[...] += jnp.dot(h, wd_ref[...], preferred_element_type=F32)

    @pl.when(j == pl.num_programs(1) - 1)
    def _():
        o_ref[...] = x_ref[...] + _rms(o_ref[...], go_ref[...])


def ffn_block(x, g_in, wg, wu, wd, g_out, layer=0, tm=1024, th=512):
    M, D = x.shape
    wg, wu, wd = _stacked(wg), _stacked(wu), _stacked(wd)
    Hd = wg.shape[2]
    tm, th = min(tm, M), min(th, Hd)
    assert M % tm == 0 and Hd % th == 0
    return pl.pallas_call(
        _ffn_kernel,
        out_shape=jax.ShapeDtypeStruct((M, D), F32),
        grid=(M // tm, Hd // th),
        in_specs=[pl.BlockSpec((tm, D), lambda i, j: (i, 0), pipeline_mode=pl.Buffered(1)),
                  pl.BlockSpec((1, D), lambda i, j: (0, 0)),
                  pl.BlockSpec((None, D, th), lambda i, j: (layer, 0, j)),
                  pl.BlockSpec((None, D, th), lambda i, j: (layer, 0, j)),
                  pl.BlockSpec((None, th, D), lambda i, j: (layer, j, 0)),
                  pl.BlockSpec((1, D), lambda i, j: (0, 0))],
        out_specs=pl.BlockSpec((tm, D), lambda i, j: (i, 0)),
        scratch_shapes=[pltpu.VMEM((tm, D), BF16)],
        compiler_params=_cp(("parallel", "arbitrary")),
        name="ffn_block",
    )(x, g_in.reshape(1, D), wg, wu, wd, g_out.reshape(1, D))


def _xattn_kernel(q_ref, k_ref, v_ref, o_ref, *, heads, dh):
    scale = dh ** -0.5
    for h in range(heads):
        sl = slice(h * dh, (h + 1) * dh)
        s = lax.dot_general(q_ref[:, sl], k_ref[:, sl], (((1,), (1,)), ((), ())),
                            preferred_element_type=F32) * scale
        m = jnp.max(s, axis=-1, keepdims=True)
        p = jnp.exp(s - m)
        l = jnp.sum(p, axis=-1, keepdims=True)
        o = jnp.dot(p.astype(BF16), v_ref[:, sl], preferred_element_type=F32)
        o_ref[:, sl] = (o / l).astype(o_ref.dtype)


def cross_attention(q, kv, B, L, n_mem, tm=512):
    M, D = q.shape
    tm = min(tm, L)
    assert L % tm == 0
    bpl = L // tm
    return pl.pallas_call(
        functools.partial(_xattn_kernel, heads=XA_HEADS, dh=D // XA_HEADS),
        out_shape=jax.ShapeDtypeStruct((M, D), BF16),
        grid=(M // tm,),
        in_specs=[pl.BlockSpec((tm, D), lambda i: (i, 0)),
                  pl.BlockSpec((n_mem, D), lambda i: (i // bpl, 0)),
                  pl.BlockSpec((n_mem, D), lambda i: (i // bpl, 1))],
        out_specs=pl.BlockSpec((tm, D), lambda i: (i, 0)),
        compiler_params=_cp(("parallel",)),
        name="cross_attention",
    )(q, kv, kv)


HALO = 16


def _norm_mm_conv_kernel(xp_ref, xc_ref, xn_ref, g_ref, w_ref, cw_ref, cb_ref, o_ref, xs_ref, *, blocks_per_seq):
    tm = xc_ref.shape[0]

    @pl.when(pl.program_id(1) == 0)
    def _():
        li = pl.program_id(0) % blocks_per_seq
        g = g_ref[...]
        keep_prev = jnp.where(li == 0, 0.0, 1.0)
        keep_next = jnp.where(li == blocks_per_seq - 1, 0.0, 1.0)
        xs_ref[0:HALO] = (_rms(xp_ref[...], g) * keep_prev).astype(BF16)
        xs_ref[HALO:HALO + tm] = _rms(xc_ref[...], g).astype(BF16)
        xs_ref[HALO + tm:] = (_rms(xn_ref[...], g) * keep_next).astype(BF16)

    y = jnp.dot(xs_ref[...], w_ref[...], preferred_element_type=F32)
    n = tm + 2 * HALO
    up = pltpu.roll(y, 1, axis=0)[HALO:HALO + tm]
    dn = pltpu.roll(y, n - 1, axis=0)[HALO:HALO + tm]
    o_ref[...] = up * cw_ref[0:1, :] + y[HALO:HALO + tm] * cw_ref[1:2, :] + dn * cw_ref[2:3, :] + cb_ref[...]


def norm_matmul_conv(x, g, w, cw, cb, L, layer=0, tm=1024, tn=512):
    M, K = x.shape
    w = _stacked(w)
    N = w.shape[2]
    tm, tn = min(tm, L), min(tn, N)
    assert L % tm == 0 and N % tn == 0 and tm % HALO == 0
    hb = tm // HALO
    nhb = M // HALO
    return pl.pallas_call(
        functools.partial(_norm_mm_conv_kernel, blocks_per_seq=L // tm),
        out_shape=jax.ShapeDtypeStruct((M, N), F32),
        grid=(M // tm, N // tn),
        in_specs=[pl.BlockSpec((HALO, K), lambda i, j: (jnp.maximum(i * hb - 1, 0), 0)),
                  pl.BlockSpec((tm, K), lambda i, j: (i, 0)),
                  pl.BlockSpec((HALO, K), lambda i, j: (jnp.minimum((i + 1) * hb, nhb - 1), 0)),
                  pl.BlockSpec((1, K), lambda i, j: (0, 0)),
                  pl.BlockSpec((None, K, tn), lambda i, j: (layer, 0, j)),
                  pl.BlockSpec((3, tn), lambda i, j: (0, j)),
                  pl.BlockSpec((1, tn), lambda i, j: (0, j))],
        out_specs=pl.BlockSpec((tm, tn), lambda i, j: (i, j)),
        scratch_shapes=[pltpu.VMEM((tm + 2 * HALO, K), BF16)],
        compiler_params=_cp(("parallel", "arbitrary")),
        name="even_in_proj_conv",
    )(x, x, x, g.reshape(1, K), w, cw, cb.reshape(1, N))


def _filter_kernel(bands_ref, w1t_ref, w1c_ref, w1s_ref, b1_ref, f_ref, w2_ref, b2_ref, w3_ref, dl_ref, sk_ref,
                   o_ref, hid_ref, *, L, ncb):
    hi = lax.Precision.HIGHEST
    s = pl.program_id(1)
    tl = o_ref.shape[0]
    m = pl.program_id(0) * tl + lax.broadcasted_iota(jnp.int32, (tl, 1), 0)

    @pl.when(s == 0)
    def _():
        for d, pos in enumerate((m, L - m)):
            t = pos.astype(F32) * (1.0 / L)
            ang = (2.0 * math.pi) * t * bands_ref[...]
            pre = (t * w1t_ref[...]
                   + jnp.dot(jnp.cos(ang), w1c_ref[...], preferred_element_type=F32, precision=hi)
                   + jnp.dot(jnp.sin(ang), w1s_ref[...], preferred_element_type=F32, precision=hi)
                   + b1_ref[...])
            h = jnp.sin(f_ref[0:1, :] * pre)
            hid_ref[d] = jnp.sin(f_ref[1:2, :] * (jnp.dot(h, w2_ref[...], preferred_element_type=F32, precision=hi)
                                                  + b2_ref[...]))

    d = (s // ncb) % 2
    t = jnp.where(d == 0, m, L - m).astype(F32) * (1.0 / L)
    out = jnp.dot(hid_ref[d], w3_ref[...], preferred_element_type=F32, precision=hi)
    out = out * jnp.exp(-t * dl_ref[...])
    out = jnp.where(m == jnp.where(d == 0, 0, -1), out + sk_ref[...], out)
    o_ref[...] = jnp.where(m == jnp.where(d == 1, 0, -1), 0.0, out)


def hyena_filters(w1, b1, freq, w2, b2, w3, skip, L, tl=512, W=512):
    C = HY_WIDTH
    Hh = HY_FILTER_HIDDEN
    tl = min(tl, L)
    bands = jnp.zeros((1, LANES), F32).at[0, :HY_BANDS].set(jnp.arange(1, HY_BANDS + 1, dtype=F32))
    w1 = w1.astype(F32)
    w1t = w1[0:1]
    w1c = jnp.zeros((LANES, Hh), F32).at[:HY_BANDS].set(w1[1:1 + HY_BANDS])
    w1s = jnp.zeros((LANES, Hh), F32).at[:HY_BANDS].set(w1[1 + HY_BANDS:])
    deltas = jnp.abs(jnp.linspace(math.log(HY_DECAY_TARGET) / HY_LONG_DECAY_PCT,
                                  math.log(HY_DECAY_TARGET) / HY_SHORT_DECAY_PCT, C, dtype=F32)).reshape(1, C)
    ncb = C // W
    small = lambda shape: pl.BlockSpec(shape, lambda i, s: (0, 0))
    return pl.pallas_call(
        functools.partial(_filter_kernel, L=L, ncb=ncb),
        out_shape=jax.ShapeDtypeStruct((2 * HY_ORDER, L, C), F32),
        grid=(L // tl, 2 * HY_ORDER * ncb),
        in_specs=[small((1, LANES)), small((1, Hh)), small((LANES, Hh)), small((LANES, Hh)), small((1, Hh)),
                  small((2, Hh)), small((Hh, Hh)), small((1, Hh)),
                  pl.BlockSpec((Hh, W), lambda i, s: (0, s)),
                  pl.BlockSpec((1, W), lambda i, s: (0, s % ncb)),
                  pl.BlockSpec((None, 1, W), lambda i, s: (s // (2 * ncb), 0, s % ncb))],
        out_specs=pl.BlockSpec((None, tl, W), lambda i, s: (s // ncb, i, s % ncb)),
        scratch_shapes=[pltpu.VMEM((2, tl, Hh), F32)],
        compiler_params=_cp(("parallel", "arbitrary")),
        name="hyena_filters",
    )(bands, w1t, w1c, w1s, b1.astype(F32).reshape(1, Hh), freq.astype(F32), w2.astype(F32),
      b2.astype(F32).reshape(1, Hh), w3.astype(F32), deltas, skip.astype(F32).reshape(HY_ORDER, 1, C))


def _fft_tables(L):
    N2 = FFT_N2
    N = 2 * L
    N1 = N // N2
    N1h = N1 // 2
    n2 = jnp.arange(N2, dtype=jnp.int32)[:, None, None]
    k1 = jnp.arange(N1, dtype=jnp.int32)[None, :, None]
    n1 = jnp.arange(N1h, dtype=jnp.int32)[None, None, :]
    ph = ((n1 * N2 + n2) * k1) % N
    ang = ph.astype(F32) * (2.0 * math.pi / N)
    gr, gi = jnp.cos(ang), -jnp.sin(ang)
    g_fwd = jnp.concatenate([jnp.concatenate([gr, -gi], 2), jnp.concatenate([gi, gr], 2)], 1).astype(BF16)
    sgn = jnp.where(k1 % 2 == 0, 1.0, -1.0).astype(F32)
    g_flt = jnp.concatenate([jnp.concatenate([gr, sgn * gr], 2), jnp.concatenate([gi, sgn * gi], 2)], 1).astype(BF16)
    er = jnp.swapaxes(gr, 1, 2) / N
    ei = -jnp.swapaxes(gi, 1, 2) / N
    g_inv = jnp.concatenate([jnp.concatenate([er, -ei], 2), jnp.concatenate([ei, er], 2)], 1).astype(BF16)
    a2 = (jnp.arange(N2, dtype=jnp.int32)[:, None] * jnp.arange(N2, dtype=jnp.int32)[None, :]) % N2
    ang2 = a2.astype(F32) * (2.0 * math.pi / N2)
    fr, fi = jnp.cos(ang2), -jnp.sin(ang2)
    f_mid = jnp.concatenate([jnp.concatenate([fr, -fi], 1), jnp.concatenate([fi, fr], 1)], 0).astype(BF16)
    f_mid_inv = jnp.concatenate([jnp.concatenate([fr, fi], 1), jnp.concatenate([-fi, fr], 1)], 0).astype(BF16)
    return dict(g_fwd=g_fwd, g_flt=g_flt, g_inv=g_inv, f_mid=f_mid, f_mid_inv=f_mid_inv, N1=N1)


def _pack_complex(re, im):
    r = lax.bitcast_convert_type(re.astype(BF16).astype(F32), jnp.uint32)
    i = lax.bitcast_convert_type(im.astype(BF16).astype(F32), jnp.uint32)
    return r | (i >> 16)


def _unpack_complex(w):
    re = lax.bitcast_convert_type(w & jnp.uint32(0xFFFF0000), F32)
    im = lax.bitcast_convert_type(w << 16, F32)
    return re, im


def _to_slabs(src_ref, slab_ref, lead=()):
    ns, rows, _ = slab_ref.shape
    for s in range(ns):
        slab_ref[s] = src_ref[lead + (slice(None), slice(None), slice(s * LANES, (s + 1) * LANES))].reshape(
            rows, LANES)


def _slab_rows(slab_ref, r, n):
    return jnp.concatenate([slab_ref.at[s][pl.ds(r, n, stride=FFT_RB), :] for s in range(slab_ref.shape[0])],
                           axis=1)


def _set_slab_rows(slab_ref, r, val):
    n = val.shape[0]
    for s in range(slab_ref.shape[0]):
        slab_ref.at[s][pl.ds(r, n, stride=FFT_RB), :] = val[:, s * LANES:(s + 1) * LANES]


def _from_slabs(slab_ref, dst_ref, lead=()):
    ns, rows, _ = slab_ref.shape
    for s in range(ns):
        dst_ref[lead + (slice(None), slice(None), slice(s * LANES, (s + 1) * LANES))] = (
            slab_ref[s].reshape(rows // FFT_RB, FFT_RB, LANES))


def _fft_a_kernel(x_ref, g_ref, z_ref, xa_ref, xb_ref, zs_ref):
    n1 = z_ref.shape[0]
    _to_slabs(x_ref, xa_ref, (0,))
    _to_slabs(x_ref, xb_ref, (1,))
    for r in range(FFT_RB):
        xs = jnp.concatenate([_slab_rows(xa_ref, r, n1 // 2), _slab_rows(xb_ref, r, n1 // 2)], axis=0).astype(BF16)
        a = jnp.dot(g_ref[r], xs, preferred_element_type=F32)
        _set_slab_rows(zs_ref, r, _pack_complex(a[:n1], a[n1:]))
    _from_slabs(zs_ref, z_ref)


def fft_stage_a(x4, col0, C, g, W=512):
    B, N1h, N2, _ = x4.shape
    N1 = 2 * N1h
    W = min(W, C)
    assert B % 2 == 0 and C % W == 0 and col0 % W == 0 and N2 % FFT_RB == 0
    cb0 = col0 // W
    return pl.pallas_call(
        _fft_a_kernel,
        out_shape=jax.ShapeDtypeStruct((B // 2, N1, N2, C), jnp.uint32),
        grid=(B // 2, N2 // FFT_RB, C // W),
        in_specs=[pl.BlockSpec((2, N1h, FFT_RB, W), lambda p, j, c: (p, 0, j, cb0 + c)),
                  pl.BlockSpec((FFT_RB, 2 * N1, N1), lambda p, j, c: (j, 0, 0))],
        out_specs=pl.BlockSpec((None, N1, FFT_RB, W), lambda p, j, c: (p, 0, j, c)),
        scratch_shapes=[pltpu.VMEM((W // LANES, N1h * FFT_RB, LANES), F32),
                        pltpu.VMEM((W // LANES, N1h * FFT_RB, LANES), F32),
                        pltpu.VMEM((W // LANES, N1 * FFT_RB, LANES), jnp.uint32)],
        compiler_params=_cp(("parallel", "parallel", "parallel")),
        name="hyena_fft_a",
    )(x4, g)


def _fft_mid_kernel(*refs, kb, with_filter):
    if with_filter:
        z_ref, h_ref, f_ref, fi_ref, y_ref = refs
    else:
        z_ref, f_ref, y_ref = refs
    n2 = z_ref.shape[1]
    for k in range(kb):
        zr, zi = _unpack_complex(z_ref[k])
        zs = jnp.concatenate([zr, zi], axis=0).astype(BF16)
        x = jnp.dot(f_ref[...], zs, preferred_element_type=F32)
        if with_filter:
            xr, xi = x[:n2], x[n2:]
            hr, hi = _unpack_complex(h_ref[k])
            ys = jnp.concatenate([xr * hr - xi * hi, xr * hi + xi * hr], axis=0).astype(BF16)
            x = jnp.dot(fi_ref[...], ys, preferred_element_type=F32)
        y_ref[k] = _pack_complex(x[:n2], x[n2:])


def fft_stage_mid(z, tabs, h=None, order=0, kb=8, W=512):
    P, N1, N2, C = z.shape
    kb, W = min(kb, N1), min(W, C)
    assert N1 % kb == 0 and C % W == 0
    blk = pl.BlockSpec((None, kb, N2, W), lambda k, c, p: (p, k, 0, c))
    mat = pl.BlockSpec((2 * N2, 2 * N2), lambda k, c, p: (0, 0))
    if h is None:
        args, specs = (z, tabs["f_mid"]), [blk, mat]
    else:
        hblk = pl.BlockSpec((None, kb, N2, W), lambda k, c, p: (order, k, 0, c))
        args, specs = (z, h, tabs["f_mid"], tabs["f_mid_inv"]), [blk, hblk, mat, mat]
    return pl.pallas_call(
        functools.partial(_fft_mid_kernel, kb=kb, with_filter=h is not None),
        out_shape=jax.ShapeDtypeStruct(z.shape, jnp.uint32),
        grid=(N1 // kb, C // W, P),
        in_specs=specs,
        out_specs=blk,
        compiler_params=_cp(("parallel", "parallel", "parallel")),
        name="hyena_fft_mid",
    )(*args)


def _fft_c_kernel(y_ref, g_ref, x_ref, o_ref, ys_ref, xa_ref, xb_ref):
    n1h = x_ref.shape[1]
    _to_slabs(y_ref, ys_ref)
    _to_slabs(x_ref, xa_ref, (0,))
    _to_slabs(x_ref, xb_ref, (1,))
    for r in range(FFT_RB):
        yr, yi = _unpack_complex(_slab_rows(ys_ref, r, 2 * n1h))
        ys = jnp.concatenate([yr, yi], axis=0).astype(BF16)
        c = jnp.dot(g_ref[r], ys, preferred_element_type=F32)
        _set_slab_rows(xa_ref, r, _slab_rows(xa_ref, r, n1h) * c[:n1h])
        _set_slab_rows(xb_ref, r, _slab_rows(xb_ref, r, n1h) * c[n1h:])
    _from_slabs(xa_ref, o_ref, (0,))
    _from_slabs(xb_ref, o_ref, (1,))


def fft_stage_c(y, g, x4, xcol0, W=512):
    P, N1, N2, C = y.shape
    N1h = N1 // 2
    W = min(W, C)
    assert C % W == 0 and xcol0 % W == 0
    xb = xcol0 // W
    slab = lambda n, dt: pltpu.VMEM((W // LANES, n * FFT_RB, LANES), dt)
    return pl.pallas_call(
        _fft_c_kernel,
        out_shape=jax.ShapeDtypeStruct((2 * P, N1h, N2, C), F32),
        grid=(P, N2 // FFT_RB, C // W),
        in_specs=[pl.BlockSpec((None, N1, FFT_RB, W), lambda p, j, c: (p, 0, j, c)),
                  pl.BlockSpec((FFT_RB, N1, 2 * N1), lambda p, j, c: (j, 0, 0)),
                  pl.BlockSpec((2, N1h, FFT_RB, W), lambda p, j, c: (p, 0, j, xb + c))],
        out_specs=pl.BlockSpec((2, N1h, FFT_RB, W), lambda p, j, c: (p, 0, j, c)),
        scratch_shapes=[slab(N1, jnp.uint32), slab(N1h, F32), slab(N1h, F32)],
        compiler_params=_cp(("parallel", "parallel", "parallel")),
        name="hyena_fft_c",
    )(y, g, x4)


def hyena_mixer(uc, B, L, w1, b1, freq, w2, b2, w3, skip):
    C = HY_WIDTH
    N2 = FFT_N2
    N1h = L // N2
    tabs = _fft_tables(L)
    uc4 = uc.reshape(B, N1h, N2, uc.shape[1])
    filt = hyena_filters(w1, b1, freq, w2, b2, w3, skip, L)
    spec = fft_stage_mid(fft_stage_a(filt.reshape(2 * HY_ORDER, N1h, N2, C), 0, C, tabs["g_flt"]), tabs)
    y = fft_stage_mid(fft_stage_a(uc4, 0, C, tabs["g_fwd"]), tabs, h=spec, order=0)
    z1 = fft_stage_c(y, tabs["g_inv"], uc4, C)
    y = fft_stage_mid(fft_stage_a(z1, 0, C, tabs["g_fwd"]), tabs, h=spec, order=1)
    z = fft_stage_c(y, tabs["g_inv"], uc4, 2 * C)
    return z.reshape(B * L, C)


def _s5_tables(a_re, a_im, log_dt, b_re, b_im, c_re, c_im, n_chunks):
    T = S5_T
    G, P, I = S5_GROUPS, S5_STATE, S5_GROUP
    f32 = lambda a: a.astype(F32)
    a_re, a_im, b_re, b_im, c_re, c_im = map(f32, (a_re, a_im, b_re, b_im, c_re, c_im))
    step = jnp.exp(f32(log_dt))[..., None]
    lr, li = a_re * step, a_im * step
    mag = jnp.exp(lr)
    br_, bi_ = mag * jnp.cos(li), mag * jnp.sin(li)
    den = a_re * a_re + a_im * a_im
    qr = ((br_ - 1.0) * a_re + bi_ * a_im) / den
    qi = (bi_ * a_re - (br_ - 1.0) * a_im) / den
    bbr = qr[..., None] * b_re - qi[..., None] * b_im
    bbi = qr[..., None] * b_im + qi[..., None] * b_re

    def lam_pow(k):
        k = k.astype(F32)
        m = jnp.exp(lr[..., None] * k)
        return m * jnp.cos(li[..., None] * k), m * jnp.sin(li[..., None] * k)

    lags = jnp.arange(T + 1)
    pr, pi_ = lam_pow(lags)
    cpr = c_re[..., None] * pr[:, :, None] - c_im[..., None] * pi_[:, :, None]
    cpi = c_re[..., None] * pi_[:, :, None] + c_im[..., None] * pr[:, :, None]
    kern = jnp.einsum("dgjpk,dgpi->dgkji", cpr, bbr) - jnp.einsum("dgjpk,dgpi->dgkji", cpi, bbi)
    tt = jnp.arange(T)
    lag = tt[:, None] - tt[None, :]
    kf = kern[0][:, jnp.clip(lag, 0, T)]
    kb = kern[1][:, jnp.clip(-lag, 0, T)]
    m_f = (lag >= 0)[None, :, :, None, None]
    m_b = (lag <= 0)[None, :, :, None, None]
    toep = jnp.where(m_f, kf, 0.0) + jnp.where(m_b, kb, 0.0)
    toep = toep.transpose(0, 1, 3, 2, 4).reshape(G, T * I, T * I)
    ef = T - 1 - tt
    eb = tt

    def bst(d, e):
        wr, wi = pr[d][..., e], pi_[d][..., e]
        re = wr[..., None] * bbr[d][:, :, None] - wi[..., None] * bbi[d][:, :, None]
        im = wr[..., None] * bbi[d][:, :, None] + wi[..., None] * bbr[d][:, :, None]
        return jnp.concatenate([re, im], axis=1).reshape(G, 2 * P, T * I)

    bst_all = jnp.stack([bst(0, ef), bst(1, eb)], axis=1)

    def cst(d, e):
        xr, xi = cpr[d][..., e], cpi[d][..., e]
        m = jnp.concatenate([xr, -xi], axis=2)
        return m.transpose(0, 3, 1, 2).reshape(G, T * I, 2 * P)

    cst_all = jnp.stack([cst(0, tt + 1), cst(1, T - tt)], axis=1)
    nsteps = max(1, int(math.log2(n_chunks)))
    e2 = T * (2 ** jnp.arange(nsteps))
    ar, ai = lam_pow(e2)
    ap = jnp.stack([ar, ai], axis=-1).transpose(1, 0, 3, 4, 2)
    ap = jnp.broadcast_to(ap[..., None], ap.shape + (LANES,))
    return toep.astype(BF16), bst_all.astype(BF16), cst_all.astype(BF16), ap, nsteps


def _s5_kernel(u_ref, toep_ref, bst_ref, cst_ref, ap_ref, y_ref, *, n_chunks, nsteps):
    T, I, cols = u_ref.shape
    P = S5_STATE
    u = u_ref[...].reshape(T * I, cols).astype(BF16)
    y = jnp.dot(toep_ref[...], u, preferred_element_type=F32)
    cidx = lax.broadcasted_iota(jnp.int32, (P, cols), 1) % n_chunks

    def shifted(x, sh, d):
        if d == 0:
            return jnp.where(cidx >= sh, pltpu.roll(x, sh, axis=1), 0.0)
        return jnp.where(cidx < n_chunks - sh, pltpu.roll(x, cols - sh, axis=1), 0.0)

    for d in range(2):
        v = jnp.dot(bst_ref[d], u, preferred_element_type=F32)
        sr, si = v[:P], v[P:]
        for j in range(nsteps):
            if (1 << j) >= n_chunks:
                break
            ar, ai = ap_ref[d, j, 0][:, :1], ap_ref[d, j, 1][:, :1]
            rr, ri = shifted(sr, 1 << j, d), shifted(si, 1 << j, d)
            sr, si = sr + ar * rr - ai * ri, si + ar * ri + ai * rr
        s_in = jnp.concatenate([shifted(sr, 1, d), shifted(si, 1, d)], axis=0).astype(BF16)
        y = y + jnp.dot(cst_ref[d], s_in, preferred_element_type=F32)
    y_ref[...] = y.reshape(T, I, cols)


def _s5_out_kernel(y_ref, u_ref, d_ref, w_ref, o_ref):
    y = y_ref[...] + d_ref[...] * u_ref[...]
    y = jax.nn.gelu(y, approximate=True)
    z = jnp.dot(y.astype(BF16), w_ref[...], preferred_element_type=F32)
    o_ref[...] = (y * jax.nn.sigmoid(z)).astype(o_ref.dtype)


def s5_mixer(p, col0, B, L, a_re, a_im, log_dt, b_re, b_im, c_re, c_im, d, w_glu, tm=1024):
    T, Wd, I, G = S5_T, S5_WIDTH, S5_GROUP, S5_GROUPS
    nC = L // T
    cols = B * nC
    assert L % T == 0 and nC & (nC - 1) == 0 and cols % LANES == 0
    toep, bst, cst, ap, nsteps = _s5_tables(a_re, a_im, log_dt, b_re, b_im, c_re, c_im, nC)
    u = p[:, col0:col0 + Wd]
    ut = u.reshape(B, nC, T, Wd).transpose(2, 3, 0, 1).reshape(T, Wd, cols)
    blk = pl.BlockSpec((T, I, cols), lambda g: (0, g, 0))
    yt = pl.pallas_call(
        functools.partial(_s5_kernel, n_chunks=nC, nsteps=nsteps),
        out_shape=jax.ShapeDtypeStruct((T, Wd, cols), F32),
        grid=(G,),
        in_specs=[blk,
                  pl.BlockSpec((None, T * I, T * I), lambda g: (g, 0, 0)),
                  pl.BlockSpec((None, 2, 2 * S5_STATE, T * I), lambda g: (g, 0, 0, 0)),
                  pl.BlockSpec((None, 2, T * I, 2 * S5_STATE), lambda g: (g, 0, 0, 0)),
                  pl.BlockSpec((None, 2, nsteps, 2, S5_STATE, LANES), lambda g: (g, 0, 0, 0, 0, 0))],
        out_specs=blk,
        compiler_params=_cp(("parallel",)),
        name="s5_scan",
    )(ut, toep, bst, cst, ap)
    y = yt.reshape(T, Wd, B, nC).transpose(2, 3, 0, 1).reshape(B * L, Wd)
    M = B * L
    tm = min(tm, M)
    cb = col0 // Wd
    assert col0 % Wd == 0 and M % tm == 0
    return pl.pallas_call(
        _s5_out_kernel,
        out_shape=jax.ShapeDtypeStruct((M, Wd), BF16),
        grid=(M // tm,),
        in_specs=[pl.BlockSpec((tm, Wd), lambda i: (i, 0)),
                  pl.BlockSpec((tm, Wd), lambda i: (i, cb)),
                  pl.BlockSpec((1, Wd), lambda i: (0, 0)),
                  pl.BlockSpec((Wd, Wd), lambda i: (0, 0))],
        out_specs=pl.BlockSpec((tm, Wd), lambda i: (i, 0)),
        compiler_params=_cp(("parallel",)),
        name="s5_glu",
    )(y, p, d.astype(F32).reshape(1, Wd), w_glu.astype(BF16))


def _ret_kernel(q_ref, k_ref, v_ref, g_ref, cos_ref, sin_ref, dm_ref, qs_ref, ks_ref, cd_ref, o_ref, st_ref,
                carry_ref, *, n_chunks):
    c = pl.program_id(2)
    nC = n_chunks
    nS = nC // RET_SUB
    Cc = dm_ref.shape[0]
    dh = k_ref.shape[1]
    h2 = dh // 2

    def rows(ref, u):
        return ref[u * Cc:(u + 1) * Cc, :]

    def rot(x, u):
        cos, sin = rows(cos_ref, u), rows(sin_ref, u)
        x1, x2 = x[:, :h2], x[:, h2:]
        return jnp.concatenate([x1 * cos - x2 * sin, x1 * sin + x2 * cos], axis=-1)

    @pl.when(c < nS)
    def _():
        for u in range(RET_SUB):
            kt = (rot(rows(k_ref, u), u) * (dh ** -0.5)).T
            v = rows(v_ref, u).astype(BF16)
            for d in range(2):
                st_ref[d, c * RET_SUB + u] = jnp.dot((kt * ks_ref[d]).astype(BF16), v, preferred_element_type=F32)

    @pl.when(c == nS)
    def _():
        for d in range(2):
            cd = cd_ref[d, 0:1, 0:1]
            carry_ref[...] = jnp.zeros_like(carry_ref)

            def body(i, carry, d=d, cd=cd):
                idx = i if d == 0 else nC - 1 - i
                t = st_ref[d, idx]
                st_ref[d, idx] = carry_ref[...]
                carry_ref[...] = carry_ref[...] * cd + t
                return carry

            lax.fori_loop(0, nC, body, 0)

    @pl.when(c >= nS)
    def _():
        for u in range(RET_SUB):
            ch = (c - nS) * RET_SUB + u
            qr = rot(rows(q_ref, u), u)
            kr = rot(rows(k_ref, u), u) * (dh ** -0.5)
            v = rows(v_ref, u).astype(BF16)
            s = lax.dot_general(qr.astype(BF16), kr.astype(BF16), (((1,), (1,)), ((), ())),
                                preferred_element_type=F32) * dm_ref[...]
            o = jnp.dot(s.astype(BF16), v, preferred_element_type=F32)
            for d in range(2):
                o = o + jnp.dot((qr * qs_ref[d]).astype(BF16), st_ref[d, ch].astype(BF16),
                                preferred_element_type=F32)
            mu = jnp.mean(o, axis=-1, keepdims=True)
            oc = o - mu
            var = jnp.mean(oc * oc, axis=-1, keepdims=True)
            g = rows(g_ref, u)
            o_ref[u * Cc:(u + 1) * Cc, :] = (oc * lax.rsqrt(var + GN_EPS)
                                             * (g * jax.nn.sigmoid(g))).astype(o_ref.dtype)


def retention_mixer(p, B, L, ret_decay):
    H, dh = RET_HEADS, RET_HEAD_DIM
    Cc = min(RET_CC, L)
    nC = L // Cc
    assert L % Cc == 0
    lg = -jnp.exp(ret_decay.astype(F32))
    pos = jnp.arange(Cc, dtype=F32)
    rel = pos[:, None] - pos[None, :]
    lf, lb = lg[0][:, None, None], lg[1][:, None, None]
    dm = jnp.where(rel >= 0, jnp.exp(jnp.maximum(rel, 0.0) * lf), jnp.exp(jnp.maximum(-rel, 0.0) * lb))
    qs = jnp.stack([jnp.exp((pos + 1.0)[None] * lg[0][:, None]),
                    jnp.exp((Cc - pos)[None] * lg[1][:, None])], axis=1)[..., None]
    ks = jnp.stack([jnp.exp((Cc - 1.0 - pos)[None] * lg[0][:, None]),
                    jnp.exp(pos[None] * lg[1][:, None])], axis=1)[:, :, None, :]
    cd = jnp.broadcast_to(jnp.exp(Cc * lg).T[:, :, None, None], (H, 2, SUBLANES, LANES))
    inv = ROPE_BASE ** (-jnp.arange(0, dh, 2, dtype=F32) / dh)
    ang = jnp.arange(L, dtype=F32)[:, None] * inv[None, :]
    cos, sin = jnp.cos(ang), jnp.sin(ang)

    assert nC % RET_SUB == 0
    nS = nC // RET_SUB
    rb = RET_SUB * Cc

    def kch(c):
        return jnp.where(c < nS, c, c - nS)

    def qch(c):
        return jnp.maximum(c - nS, 0)

    kv = lambda off: pl.BlockSpec((rb, dh), lambda b, h, c: (b * nS + kch(c), off * H + h))
    qo = lambda off: pl.BlockSpec((rb, dh), lambda b, h, c: (b * nS + qch(c), off * H + h))
    tab = pl.BlockSpec((rb, dh // 2), lambda b, h, c: (kch(c), 0))
    return pl.pallas_call(
        functools.partial(_ret_kernel, n_chunks=nC),
        out_shape=jax.ShapeDtypeStruct((B * L, H * dh), BF16),
        grid=(B, H, 2 * nS),
        in_specs=[qo(0), kv(1), kv(2), qo(3), tab, tab,
                  pl.BlockSpec((None, Cc, Cc), lambda b, h, c: (h, 0, 0)),
                  pl.BlockSpec((None, 2, Cc, 1), lambda b, h, c: (h, 0, 0, 0)),
                  pl.BlockSpec((None, 2, 1, Cc), lambda b, h, c: (h, 0, 0, 0)),
                  pl.BlockSpec((None, 2, SUBLANES, LANES), lambda b, h, c: (h, 0, 0, 0))],
        out_specs=qo(0),
        scratch_shapes=[pltpu.VMEM((2, nC, dh, dh), F32), pltpu.VMEM((dh, dh), F32)],
        compiler_params=_cp(("parallel", "parallel", "arbitrary")),
        name="retention",
    )(p, p, p, p, cos, sin, dm, qs, ks, cd)


def _na_bias_tables(rpb):
    Wc, WR, WC = GRID_W, NA_WIN_ROWS, NA_WIN_COLS
    hi = lax.Precision.HIGHEST
    rpb = rpb.astype(F32)
    c = jnp.arange(Wc)[:, None]
    kc = jnp.arange(Wc)[None, :]
    cs = jnp.clip(c - WC // 2, 0, Wc - WC)
    valid = (kc >= cs) & (kc < cs + WC)
    csel = (((kc - c + (WC - 1))[:, :, None] == jnp.arange(2 * WC - 1)) & valid[:, :, None]).astype(F32)
    dj = jnp.arange(WR)[None, :] - jnp.arange(WR)[:, None] + (WR - 1)
    rsel = (dj[:, :, None] == jnp.arange(2 * WR - 1)).astype(F32)
    t = jnp.einsum("hrs,cks->hrck", rpb, csel, precision=hi)
    t = jnp.einsum("djr,hrck->hdcjk", rsel, t, precision=hi)
    t = jnp.where(valid[None, None, :, None, :], t, NEG_INF)
    t = t.reshape(NA_HEADS // NA_HG, NA_HG, WR, Wc, WR * Wc).transpose(0, 2, 1, 3, 4)
    return t.reshape(NA_HEADS // NA_HG, WR, NA_HG * Wc, WR * Wc)


def _na_kernel(q_ref, k_ref, v_ref, b_ref, o_ref, *, rows):
    Wc, WR = GRID_W, NA_WIN_ROWS
    hw = NA_HG * NA_HEAD_DIM
    hq = NA_HG * Wc
    scale = NA_HEAD_DIM ** -0.5
    rb = pl.program_id(2)
    own = (lax.broadcasted_iota(jnp.int32, (hq, hw), 0) // Wc
           == lax.broadcasted_iota(jnp.int32, (hq, hw), 1) // NA_HEAD_DIM)

    for i in range(NA_RB):
        r = rb * NA_RB + i
        rs = jnp.clip(r - WR // 2, 0, rows - WR)
        q = q_ref[i * Wc:(i + 1) * Wc, :]
        qs = jnp.where(own, jnp.concatenate([q] * NA_HG, axis=0), jnp.zeros((), q.dtype))
        k0 = pl.multiple_of(rs * Wc, Wc)
        kw = k_ref[pl.ds(k0, WR * Wc), :]
        vw = v_ref[pl.ds(k0, WR * Wc), :]
        s = lax.dot_general(qs, kw, (((1,), (1,)), ((), ())), preferred_element_type=F32) * scale + b_ref[r - rs]
        m = jnp.max(s, axis=-1, keepdims=True)
        e = jnp.exp(s - m)
        l = jnp.sum(e, axis=-1, keepdims=True)
        o = jnp.dot(e.astype(BF16), vw, preferred_element_type=F32) / l
        o = jnp.where(own, o, 0.0)
        out = o[0:Wc]
        for h in range(1, NA_HG):
            out = out + o[h * Wc:(h + 1) * Wc]
        o_ref[i * Wc:(i + 1) * Wc, :] = out.astype(o_ref.dtype)


def neighborhood_mixer(qkv, B, L, rpb):
    Wc = GRID_W
    rows = L // Wc
    assert rows >= NA_WIN_ROWS and rows % NA_RB == 0
    hw = NA_HG * NA_HEAD_DIM
    nhg = NA_HEADS // NA_HG
    bias = _na_bias_tables(rpb)
    nrb = rows // NA_RB
    return pl.pallas_call(
        functools.partial(_na_kernel, rows=rows),
        out_shape=jax.ShapeDtypeStruct((B * L, NA_WIDTH), BF16),
        grid=(B, nhg, nrb),
        in_specs=[pl.BlockSpec((NA_RB * Wc, hw), lambda b, g, r: (b * nrb + r, g)),
                  pl.BlockSpec((L, hw), lambda b, g, r: (b, nhg + g)),
                  pl.BlockSpec((L, hw), lambda b, g, r: (b, 2 * nhg + g)),
                  pl.BlockSpec((None, NA_WIN_ROWS, NA_HG * Wc, NA_WIN_ROWS * Wc), lambda b, g, r: (g, 0, 0, 0))],
        out_specs=pl.BlockSpec((NA_RB * Wc, hw), lambda b, g, r: (b * nrb + r, g)),
        compiler_params=_cp(("parallel", "parallel", "arbitrary")),
        name="neighborhood_attention",
    )(qkv, qkv, qkv, bias)


def _trunk(x, mem, B, L, prm, wb):
    n_mem = mem.shape[0] // B
    depth = prm["norm_g"].shape[0]
    hw3 = 3 * HY_WIDTH
    for layer in range(depth):
        i = layer // 2
        g = prm["norm_g"][layer]
        wo = wb["mix_wo"]
        if layer % 2 == 0:
            ident = jnp.zeros((3, S5_WIDTH), F32).at[1].set(1.0)
            cw = jnp.concatenate([prm["hy_short_w"][i].astype(F32), ident], axis=1)
            cb = jnp.concatenate([prm["hy_short_b"][i].astype(F32), jnp.zeros((S5_WIDTH,), F32)])
            uc = norm_matmul_conv(x, g[0], wb["ev_w_in"], cw, cb, L, layer=i)
            z = hyena_mixer(uc, B, L, prm["hy_w1"][i], prm["hy_b1"][i], prm["hy_freq"][i], prm["hy_w2"][i],
                            prm["hy_b2"][i], prm["hy_w3"][i], prm["hy_skip"][i])
            ss = s5_mixer(uc, hw3, B, L, prm["s5_a_re"][i], prm["s5_a_im"][i], prm["s5_log_dt"][i],
                          prm["s5_b_re"][i], prm["s5_b_im"][i], prm["s5_c_re"][i], prm["s5_c_im"][i],
                          prm["s5_d"][i], prm["s5_w_glu"][i])
            ops = [(z, wo, prm["hy_out_g"][i], 0), (ss, wo, None, HY_WIDTH)]
        else:
            w_in = wb["od_w_in"]
            pr = norm_matmul(x, g[0], w_in, F32, 0, 4 * RET_WIDTH, layer=i, name="odd_in_proj_ret")
            pn = norm_matmul(x, g[0], w_in, BF16, 4 * RET_WIDTH, 3 * NA_WIDTH, layer=i, name="odd_in_proj_na")
            ret = retention_mixer(pr, B, L, prm["ret_decay"][i])
            na = neighborhood_mixer(pn, B, L, prm["na_rpb"][i])
            ops = [(ret, wo, None, 0), (na, wo, None, RET_WIDTH)]
        x = matmul_norm_residual(ops, g[1], x, layer=layer, name="mix_out_proj")
        q = norm_matmul(x, g[2], wb["xa_wq"], BF16, layer=layer, name="xattn_q_proj")
        kv = norm_matmul(mem, prm["mem_norm_g"][layer], wb["xa_wkv"], BF16, layer=layer, name="xattn_kv_proj")
        o = cross_attention(q, kv, B, L, n_mem)
        x = matmul_norm_residual([(o, wb["xa_wo"], None, 0)], g[3], x, layer=layer, name="xattn_out_proj")
        x = ffn_block(x, g[4], wb["ffn_wg"], wb["ffn_wu"], wb["ffn_wd"], g[5], layer=layer)
    return x


def kernel(x_prompt, x_sample, mem_prompt, mem_sample, norm_g, mix_wo, ev_w_in, hy_short_w, hy_short_b, hy_w1, hy_b1, hy_freq, hy_w2, hy_b2, hy_w3, hy_skip, hy_out_g, s5_a_re, s5_a_im, s5_log_dt, s5_b_re, s5_b_im, s5_c_re, s5_c_im, s5_d, s5_w_glu, od_w_in, ret_decay, na_rpb, mem_norm_g, xa_wq, xa_wkv, xa_wo, ffn_wg, ffn_wu, ffn_wd):
    prm = dict(norm_g=norm_g, hy_short_w=hy_short_w, hy_short_b=hy_short_b, hy_w1=hy_w1, hy_b1=hy_b1,
               hy_freq=hy_freq, hy_w2=hy_w2, hy_b2=hy_b2, hy_w3=hy_w3, hy_skip=hy_skip, hy_out_g=hy_out_g,
               s5_a_re=s5_a_re, s5_a_im=s5_a_im, s5_log_dt=s5_log_dt, s5_b_re=s5_b_re, s5_b_im=s5_b_im,
               s5_c_re=s5_c_re, s5_c_im=s5_c_im, s5_d=s5_d, s5_w_glu=s5_w_glu, ret_decay=ret_decay,
               na_rpb=na_rpb, mem_norm_g=mem_norm_g)
    wb = {k: v.astype(BF16) for k, v in dict(mix_wo=mix_wo, ev_w_in=ev_w_in, od_w_in=od_w_in, xa_wq=xa_wq,
                                             xa_wkv=xa_wkv, xa_wo=xa_wo, ffn_wg=ffn_wg, ffn_wu=ffn_wu,
                                             ffn_wd=ffn_wd).items()}
    outs = []
    for x, mem in ((x_prompt, mem_prompt), (x_sample, mem_sample)):
        B, L, D = x.shape
        y = _trunk(x.reshape(B * L, D), mem.reshape(-1, D), B, L, prm, wb)
        outs.append(y.reshape(B, L, D))
    return tuple(outs)
```

```python
import functools
import math

import jax
import jax.numpy as jnp
from jax import lax
from jax.experimental import pallas as pl
from jax.experimental.pallas import tpu as pltpu

F32 = jnp.float32
BF16 = jnp.bfloat16

V7X_VMEM_BYTES = 64 * 1024 * 1024
VMEM_LIMIT = V7X_VMEM_BYTES - 8 * 1024 * 1024
LANES = 128
SUBLANES = 8

D_MODEL = 2048
GRID_W = 64
HY_WIDTH = 3 * D_MODEL // 4
S5_WIDTH = D_MODEL - HY_WIDTH
S5_GROUP = 16
S5_GROUPS = S5_WIDTH // S5_GROUP
S5_STATE = 64
HY_ORDER = 2
HY_BANDS = 16
HY_FILTER_HIDDEN = 64
HY_DECAY_TARGET = 1e-2
HY_SHORT_DECAY_PCT = 0.3
HY_LONG_DECAY_PCT = 1.5
RET_WIDTH = D_MODEL // 2
RET_HEADS = 4
RET_HEAD_DIM = RET_WIDTH // RET_HEADS
ROPE_BASE = 10000.0
NA_WIDTH = D_MODEL - RET_WIDTH
NA_HEADS = 16
NA_HEAD_DIM = NA_WIDTH // NA_HEADS
NA_WIN_ROWS = 8
NA_WIN_COLS = 16
XA_HEADS = 4
XA_HEAD_DIM = D_MODEL // XA_HEADS
RMS_EPS = 1e-6
GN_EPS = 1e-6

FFT_N2 = 128
FFT_RB = SUBLANES
S5_T = 32
RET_CC = 256
RET_SUB = 2
NA_HG = 4
NA_RB = 8
NEG_INF = -1e30


def _cp(sem, vmem=VMEM_LIMIT):
    return pltpu.CompilerParams(dimension_semantics=sem, vmem_limit_bytes=vmem)


def _rms(x, g, eps=RMS_EPS):
    return x * lax.rsqrt(jnp.mean(x * x, axis=-1, keepdims=True) + eps) * g


def _norm_mm_kernel(x_ref, g_ref, w_ref, o_ref, xn_ref):
    @pl.when(pl.program_id(1) == 0)
    def _():
        xn_ref[...] = _rms(x_ref[...], g_ref[...]).astype(BF16)

    o_ref[...] = jnp.dot(xn_ref[...], w_ref[...], preferred_element_type=F32).astype(o_ref.dtype)


def _stacked(w):
    return w if w.ndim == 3 else w[None]


def norm_matmul(x, g, w, out_dtype, col0=0, ncols=None, layer=0, tm=1024, tn=512, name="norm_matmul"):
    M, K = x.shape
    w = _stacked(w)
    N = w.shape[2] - col0 if ncols is None else ncols
    tm, tn = min(tm, M), min(tn, N)
    assert M % tm == 0 and N % tn == 0 and col0 % tn == 0
    cb0 = col0 // tn
    return pl.pallas_call(
        _norm_mm_kernel,
        out_shape=jax.ShapeDtypeStruct((M, N), out_dtype),
        grid=(M // tm, N // tn),
        in_specs=[pl.BlockSpec((tm, K), lambda i, j: (i, 0)),
                  pl.BlockSpec((1, K), lambda i, j: (0, 0)),
                  pl.BlockSpec((None, K, tn), lambda i, j: (layer, 0, cb0 + j))],
        out_specs=pl.BlockSpec((tm, tn), lambda i, j: (i, j)),
        scratch_shapes=[pltpu.VMEM((tm, K), BF16)],
        compiler_params=_cp(("parallel", "arbitrary")),
        name=name,
    )(x, g.reshape(1, K), w)


def _mm_norm_res_kernel(*refs, n_ops, prenorm):
    pos = 0
    y = None
    for t in range(n_ops):
        a = refs[pos][...]
        w_ref = refs[pos + 1]
        pos += 2
        if prenorm[t]:
            a = _rms(a.astype(F32), refs[pos][...])
            pos += 1
        d = jnp.dot(a.astype(BF16), w_ref[...], preferred_element_type=F32)
        y = d if y is None else y + d
    g_ref, x_ref, gn_ref, o_ref, on_ref = refs[pos:pos + 5]
    x1 = x_ref[...] + _rms(y, g_ref[...])
    o_ref[...] = x1
    on_ref[...] = _rms(x1, gn_ref[...]).astype(on_ref.dtype)


def matmul_norm_residual(ops, g, x, g_next, layer=0, tm=512, name="matmul_norm_residual"):
    M, N = x.shape
    tm = min(tm, M)
    assert M % tm == 0
    args, specs, prenorm = [], [], []
    for a, w, pg, k0 in ops:
        kt = a.shape[1]
        assert k0 % kt == 0
        args += [a, _stacked(w)]
        specs += [pl.BlockSpec((tm, kt), lambda i: (i, 0)),
                  pl.BlockSpec((None, kt, N), lambda i, kb=k0 // kt: (layer, kb, 0))]
        prenorm.append(pg is not None)
        if pg is not None:
            args.append(pg.reshape(1, kt))
            specs.append(pl.BlockSpec((1, kt), lambda i: (0, 0)))
    row = pl.BlockSpec((1, N), lambda i: (0, 0))
    tile = pl.BlockSpec((tm, N), lambda i: (i, 0))
    args += [g.reshape(1, N), x, g_next.reshape(1, N)]
    specs += [row, tile, row]
    return pl.pallas_call(
        functools.partial(_mm_norm_res_kernel, n_ops=len(ops), prenorm=tuple(prenorm)),
        out_shape=(jax.ShapeDtypeStruct((M, N), F32), jax.ShapeDtypeStruct((M, N), BF16)),
        grid=(M // tm,),
        in_specs=specs,
        out_specs=(tile, tile),
        compiler_params=_cp(("parallel",)),
        name=name,
    )(*args)


def _xattn_block_kernel(xn_ref, x_ref, wq_ref, k_ref, v_ref, wo_ref, g_ref, o_ref, *, heads):
    dh = xn_ref.shape[1] // heads
    scale = dh ** -0.5
    q = jnp.dot(xn_ref[...], wq_ref[...], preferred_element_type=F32).astype(BF16)
    outs = []
    for h in range(heads):
        sl = slice(h * dh, (h + 1) * dh)
        s = lax.dot_general(q[:, sl], k_ref[:, sl], (((1,), (1,)), ((), ())), preferred_element_type=F32) * scale
        m = jnp.max(s, axis=-1, keepdims=True)
        p = jnp.exp(s - m)
        l = jnp.sum(p, axis=-1, keepdims=True)
        outs.append((jnp.dot(p.astype(BF16), v_ref[:, sl], preferred_element_type=F32) / l).astype(BF16))
    o = jnp.concatenate(outs, axis=-1)
    y = jnp.dot(o, wo_ref[...], preferred_element_type=F32)
    o_ref[...] = x_ref[...] + _rms(y, g_ref[...])


def xattn_block(xn, x, wq, kv, wo, g, B, L, n_mem, layer=0, tm=512):
    M, D = x.shape
    tm = min(tm, L)
    assert L % tm == 0
    bpl = L // tm
    wq, wo = _stacked(wq), _stacked(wo)
    wspec = pl.BlockSpec((None, D, D), lambda i: (layer, 0, 0), pipeline_mode=pl.Buffered(1))
    return pl.pallas_call(
        functools.partial(_xattn_block_kernel, heads=XA_HEADS),
        out_shape=jax.ShapeDtypeStruct((M, D), F32),
        grid=(M // tm,),
        in_specs=[pl.BlockSpec((tm, D), lambda i: (i, 0)),
                  pl.BlockSpec((tm, D), lambda i: (i, 0)),
                  wspec,
                  pl.BlockSpec((n_mem, D), lambda i: (i // bpl, 0)),
                  pl.BlockSpec((n_mem, D), lambda i: (i // bpl, 1)),
                  wspec,
                  pl.BlockSpec((1, D), lambda i: (0, 0))],
        out_specs=pl.BlockSpec((tm, D), lambda i: (i, 0)),
        compiler_params=_cp(("parallel",)),
        name="xattn_block",
    )(xn, x, wq, kv, kv, wo, g.reshape(1, D))


def _ffn_kernel(x_ref, gi_ref, wg_ref, wu_ref, wd_ref, go_ref, o_ref, xn_ref):
    j = pl.program_id(1)

    @pl.when(j == 0)
    def _():
        xn_ref[...] = _rms(x_ref[...], gi_ref[...]).astype(BF16)
        o_ref[...] = jnp.zeros_like(o_ref)

    xn = xn_ref[...]
    a = jnp.dot(xn, wg_ref[...], preferred_element_type=F32)
    u = jnp.dot(xn, wu_ref[...], preferred_element_type=F32)
    h = (a * jax.nn.sigmoid(a) * u).astype(BF16)
    o_ref[...] += jnp.dot(h, wd_ref[...], preferred_element_type=F32)

    @pl.when(j == pl.num_programs(1) - 1)
    def _():
        o_ref[...] = x_ref[...] + _rms(o_ref[...], go_ref[...])


def ffn_block(x, g_in, wg, wu, wd, g_out, layer=0, tm=1024, th=512):
    M, D = x.shape
    wg, wu, wd = _stacked(wg), _stacked(wu), _stacked(wd)
    Hd = wg.shape[2]
    tm, th = min(tm, M), min(th, Hd)
    assert M % tm == 0 and Hd % th == 0
    return pl.pallas_call(
        _ffn_kernel,
        out_shape=jax.ShapeDtypeStruct((M, D), F32),
        grid=(M // tm, Hd // th),
        in_specs=[pl.BlockSpec((tm, D), lambda i, j: (i, 0), pipeline_mode=pl.Buffered(1)),
                  pl.BlockSpec((1, D), lambda i, j: (0, 0)),
                  pl.BlockSpec((None, D, th), lambda i, j: (layer, 0, j)),
                  pl.BlockSpec((None, D, th), lambda i, j: (layer, 0, j)),
                  pl.BlockSpec((None, th, D), lambda i, j: (layer, j, 0)),
                  pl.BlockSpec((1, D), lambda i, j: (0, 0))],
        out_specs=pl.BlockSpec((tm, D), lambda i, j: (i, 0)),
        scratch_shapes=[pltpu.VMEM((tm, D), BF16)],
        compiler_params=_cp(("parallel", "arbitrary")),
        name="ffn_block",
    )(x, g_in.reshape(1, D), wg, wu, wd, g_out.reshape(1, D))


HALO = 16


def _norm_mm_conv_kernel(xp_ref, xc_ref, xn_ref, g_ref, w_ref, cw_ref, cb_ref, o_ref, xs_ref, *, blocks_per_seq):
    tm = xc_ref.shape[0]

    @pl.when(pl.program_id(1) == 0)
    def _():
        li = pl.program_id(0) % blocks_per_seq
        g = g_ref[...]
        keep_prev = jnp.where(li == 0, 0.0, 1.0)
        keep_next = jnp.where(li == blocks_per_seq - 1, 0.0, 1.0)
        xs_ref[0:HALO] = (_rms(xp_ref[...], g) * keep_prev).astype(BF16)
        xs_ref[HALO:HALO + tm] = _rms(xc_ref[...], g).astype(BF16)
        xs_ref[HALO + tm:] = (_rms(xn_ref[...], g) * keep_next).astype(BF16)

    y = jnp.dot(xs_ref[...], w_ref[...], preferred_element_type=F32)
    n = tm + 2 * HALO
    up = pltpu.roll(y, 1, axis=0)[HALO:HALO + tm]
    dn = pltpu.roll(y, n - 1, axis=0)[HALO:HALO + tm]
    o_ref[...] = up * cw_ref[0:1, :] + y[HALO:HALO + tm] * cw_ref[1:2, :] + dn * cw_ref[2:3, :] + cb_ref[...]


def norm_matmul_conv(x, g, w, cw, cb, L, layer=0, tm=1024, tn=512):
    M, K = x.shape
    w = _stacked(w)
    N = w.shape[2]
    tm, tn = min(tm, L), min(tn, N)
    assert L % tm == 0 and N % tn == 0 and tm % HALO == 0
    hb = tm // HALO
    nhb = M // HALO
    return pl.pallas_call(
        functools.partial(_norm_mm_conv_kernel, blocks_per_seq=L // tm),
        out_shape=jax.ShapeDtypeStruct((M, N), F32),
        grid=(M // tm, N // tn),
        in_specs=[pl.BlockSpec((HALO, K), lambda i, j: (jnp.maximum(i * hb - 1, 0), 0)),
                  pl.BlockSpec((tm, K), lambda i, j: (i, 0)),
                  pl.BlockSpec((HALO, K), lambda i, j: (jnp.minimum((i + 1) * hb, nhb - 1), 0)),
                  pl.BlockSpec((1, K), lambda i, j: (0, 0)),
                  pl.BlockSpec((None, K, tn), lambda i, j: (layer, 0, j)),
                  pl.BlockSpec((3, tn), lambda i, j: (0, j)),
                  pl.BlockSpec((1, tn), lambda i, j: (0, j))],
        out_specs=pl.BlockSpec((tm, tn), lambda i, j: (i, j)),
        scratch_shapes=[pltpu.VMEM((tm + 2 * HALO, K), BF16)],
        compiler_params=_cp(("parallel", "arbitrary")),
        name="even_in_proj_conv",
    )(x, x, x, g.reshape(1, K), w, cw, cb.reshape(1, N))


def _filter_kernel(bands_ref, w1t_ref, w1c_ref, w1s_ref, b1_ref, f_ref, w2_ref, b2_ref, w3_ref, dl_ref, sk_ref,
                   o_ref, hid_ref, *, L, ncb):
    hi = lax.Precision.HIGHEST
    s = pl.program_id(1)
    tl = o_ref.shape[0]
    m = pl.program_id(0) * tl + lax.broadcasted_iota(jnp.int32, (tl, 1), 0)

    @pl.when(s == 0)
    def _():
        for d, pos in enumerate((m, L - m)):
            t = pos.astype(F32) * (1.0 / L)
            ang = (2.0 * math.pi) * t * bands_ref[...]
            pre = (t * w1t_ref[...]
                   + jnp.dot(jnp.cos(ang), w1c_ref[...], preferred_element_type=F32, precision=hi)
                   + jnp.dot(jnp.sin(ang), w1s_ref[...], preferred_element_type=F32, precision=hi)
                   + b1_ref[...])
            h = jnp.sin(f_ref[0:1, :] * pre)
            hid_ref[d] = jnp.sin(f_ref[1:2, :] * (jnp.dot(h, w2_ref[...], preferred_element_type=F32, precision=hi)
                                                  + b2_ref[...]))

    d = (s // ncb) % 2
    t = jnp.where(d == 0, m, L - m).astype(F32) * (1.0 / L)
    out = jnp.dot(hid_ref[d], w3_ref[...], preferred_element_type=F32, precision=hi)
    out = out * jnp.exp(-t * dl_ref[...])
    out = jnp.where(m == jnp.where(d == 0, 0, -1), out + sk_ref[...], out)
    o_ref[...] = jnp.where(m == jnp.where(d == 1, 0, -1), 0.0, out)


def hyena_filters(w1, b1, freq, w2, b2, w3, skip, L, tl=512, W=512):
    C = HY_WIDTH
    Hh = HY_FILTER_HIDDEN
    tl = min(tl, L)
    bands = jnp.zeros((1, LANES), F32).at[0, :HY_BANDS].set(jnp.arange(1, HY_BANDS + 1, dtype=F32))
    w1 = w1.astype(F32)
    w1t = w1[0:1]
    w1c = jnp.zeros((LANES, Hh), F32).at[:HY_BANDS].set(w1[1:1 + HY_BANDS])
    w1s = jnp.zeros((LANES, Hh), F32).at[:HY_BANDS].set(w1[1 + HY_BANDS:])
    deltas = jnp.abs(jnp.linspace(math.log(HY_DECAY_TARGET) / HY_LONG_DECAY_PCT,
                                  math.log(HY_DECAY_TARGET) / HY_SHORT_DECAY_PCT, C, dtype=F32)).reshape(1, C)
    ncb = C // W
    small = lambda shape: pl.BlockSpec(shape, lambda i, s: (0, 0))
    return pl.pallas_call(
        functools.partial(_filter_kernel, L=L, ncb=ncb),
        out_shape=jax.ShapeDtypeStruct((2 * HY_ORDER, L, C), F32),
        grid=(L // tl, 2 * HY_ORDER * ncb),
        in_specs=[small((1, LANES)), small((1, Hh)), small((LANES, Hh)), small((LANES, Hh)), small((1, Hh)),
                  small((2, Hh)), small((Hh, Hh)), small((1, Hh)),
                  pl.BlockSpec((Hh, W), lambda i, s: (0, s)),
                  pl.BlockSpec((1, W), lambda i, s: (0, s % ncb)),
                  pl.BlockSpec((None, 1, W), lambda i, s: (s // (2 * ncb), 0, s % ncb))],
        out_specs=pl.BlockSpec((None, tl, W), lambda i, s: (s // ncb, i, s % ncb)),
        scratch_shapes=[pltpu.VMEM((2, tl, Hh), F32)],
        compiler_params=_cp(("parallel", "arbitrary")),
        name="hyena_filters",
    )(bands, w1t, w1c, w1s, b1.astype(F32).reshape(1, Hh), freq.astype(F32), w2.astype(F32),
      b2.astype(F32).reshape(1, Hh), w3.astype(F32), deltas, skip.astype(F32).reshape(HY_ORDER, 1, C))


def _fft_tables(L):
    N2 = FFT_N2
    N = 2 * L
    N1 = N // N2
    N1h = N1 // 2
    n2 = jnp.arange(N2, dtype=jnp.int32)[:, None, None]
    k1 = jnp.arange(N1, dtype=jnp.int32)[None, :, None]
    n1 = jnp.arange(N1h, dtype=jnp.int32)[None, None, :]
    ph = ((n1 * N2 + n2) * k1) % N
    ang = ph.astype(F32) * (2.0 * math.pi / N)
    gr, gi = jnp.cos(ang), -jnp.sin(ang)
    g_fwd = jnp.concatenate([jnp.concatenate([gr, -gi], 2), jnp.concatenate([gi, gr], 2)], 1).astype(BF16)
    sgn = jnp.where(k1 % 2 == 0, 1.0, -1.0).astype(F32)
    g_flt = jnp.concatenate([jnp.concatenate([gr, sgn * gr], 2), jnp.concatenate([gi, sgn * gi], 2)], 1).astype(BF16)
    er = jnp.swapaxes(gr, 1, 2) / N
    ei = -jnp.swapaxes(gi, 1, 2) / N
    g_inv = jnp.concatenate([jnp.concatenate([er, -ei], 2), jnp.concatenate([ei, er], 2)], 1).astype(BF16)
    a2 = (jnp.arange(N2, dtype=jnp.int32)[:, None] * jnp.arange(N2, dtype=jnp.int32)[None, :]) % N2
    ang2 = a2.astype(F32) * (2.0 * math.pi / N2)
    fr, fi = jnp.cos(ang2), -jnp.sin(ang2)
    f_mid = jnp.concatenate([jnp.concatenate([fr, -fi], 1), jnp.concatenate([fi, fr], 1)], 0).astype(BF16)
    f_mid_inv = jnp.concatenate([jnp.concatenate([fr, fi], 1), jnp.concatenate([-fi, fr], 1)], 0).astype(BF16)
    return dict(g_fwd=g_fwd, g_flt=g_flt, g_inv=g_inv, f_mid=f_mid, f_mid_inv=f_mid_inv, N1=N1)


def _pack_complex(re, im):
    r = lax.bitcast_convert_type(re.astype(BF16).astype(F32), jnp.uint32)
    i = lax.bitcast_convert_type(im.astype(BF16).astype(F32), jnp.uint32)
    return r | (i >> 16)


def _unpack_complex(w):
    re = lax.bitcast_convert_type(w & jnp.uint32(0xFFFF0000), F32)
    im = lax.bitcast_convert_type(w << 16, F32)
    return re, im


def _to_slabs(src_ref, slab_ref, lead=()):
    ns, rows, _ = slab_ref.shape
    for s in range(ns):
        slab_ref[s] = src_ref[lead + (slice(None), slice(None), slice(s * LANES, (s + 1) * LANES))].reshape(
            rows, LANES)


def _slab_rows(slab_ref, r, n):
    return jnp.concatenate([slab_ref.at[s][pl.ds(r, n, stride=FFT_RB), :] for s in range(slab_ref.shape[0])],
                           axis=1)


def _set_slab_rows(slab_ref, r, val):
    n = val.shape[0]
    for s in range(slab_ref.shape[0]):
        slab_ref.at[s][pl.ds(r, n, stride=FFT_RB), :] = val[:, s * LANES:(s + 1) * LANES]


def _from_slabs(slab_ref, dst_ref, lead=()):
    ns, rows, _ = slab_ref.shape
    for s in range(ns):
        dst_ref[lead + (slice(None), slice(None), slice(s * LANES, (s + 1) * LANES))] = (
            slab_ref[s].reshape(rows // FFT_RB, FFT_RB, LANES))


def _fft_a_kernel(x_ref, g_ref, z_ref, xa_ref, xb_ref, zs_ref):
    n1 = z_ref.shape[0]
    _to_slabs(x_ref, xa_ref, (0,))
    _to_slabs(x_ref, xb_ref, (1,))
    for r in range(FFT_RB):
        xs = jnp.concatenate([_slab_rows(xa_ref, r, n1 // 2), _slab_rows(xb_ref, r, n1 // 2)], axis=0).astype(BF16)
        a = jnp.dot(g_ref[r], xs, preferred_element_type=F32)
        _set_slab_rows(zs_ref, r, _pack_complex(a[:n1], a[n1:]))
    _from_slabs(zs_ref, z_ref)


def fft_stage_a(x4, col0, C, g, W=512):
    B, N1h, N2, _ = x4.shape
    N1 = 2 * N1h
    W = min(W, C)
    assert B % 2 == 0 and C % W == 0 and col0 % W == 0 and N2 % FFT_RB == 0
    cb0 = col0 // W
    return pl.pallas_call(
        _fft_a_kernel,
        out_shape=jax.ShapeDtypeStruct((B // 2, N1, N2, C), jnp.uint32),
        grid=(B // 2, N2 // FFT_RB, C // W),
        in_specs=[pl.BlockSpec((2, N1h, FFT_RB, W), lambda p, j, c: (p, 0, j, cb0 + c)),
                  pl.BlockSpec((FFT_RB, 2 * N1, N1), lambda p, j, c: (j, 0, 0))],
        out_specs=pl.BlockSpec((None, N1, FFT_RB, W), lambda p, j, c: (p, 0, j, c)),
        scratch_shapes=[pltpu.VMEM((W // LANES, N1h * FFT_RB, LANES), F32),
                        pltpu.VMEM((W // LANES, N1h * FFT_RB, LANES), F32),
                        pltpu.VMEM((W // LANES, N1 * FFT_RB, LANES), jnp.uint32)],
        compiler_params=_cp(("parallel", "parallel", "parallel")),
        name="hyena_fft_a",
    )(x4, g)


def _fft_mid_kernel(*refs, kb, with_filter):
    if with_filter:
        z_ref, h_ref, f_ref, fi_ref, y_ref = refs
    else:
        z_ref, f_ref, y_ref = refs
    n2 = z_ref.shape[1]
    for k in range(kb):
        zr, zi = _unpack_complex(z_ref[k])
        zs = jnp.concatenate([zr, zi], axis=0).astype(BF16)
        x = jnp.dot(f_ref[...], zs, preferred_element_type=F32)
        if with_filter:
            xr, xi = x[:n2], x[n2:]
            hr, hi = _unpack_complex(h_ref[k])
            ys = jnp.concatenate([xr * hr - xi * hi, xr * hi + xi * hr], axis=0).astype(BF16)
            x = jnp.dot(fi_ref[...], ys, preferred_element_type=F32)
        y_ref[k] = _pack_complex(x[:n2], x[n2:])


def fft_stage_mid(z, tabs, h=None, order=0, kb=8, W=512):
    P, N1, N2, C = z.shape
    kb, W = min(kb, N1), min(W, C)
    assert N1 % kb == 0 and C % W == 0
    blk = pl.BlockSpec((None, kb, N2, W), lambda k, c, p: (p, k, 0, c))
    mat = pl.BlockSpec((2 * N2, 2 * N2), lambda k, c, p: (0, 0))
    if h is None:
        args, specs = (z, tabs["f_mid"]), [blk, mat]
    else:
        hblk = pl.BlockSpec((None, kb, N2, W), lambda k, c, p: (order, k, 0, c))
        args, specs = (z, h, tabs["f_mid"], tabs["f_mid_inv"]), [blk, hblk, mat, mat]
    return pl.pallas_call(
        functools.partial(_fft_mid_kernel, kb=kb, with_filter=h is not None),
        out_shape=jax.ShapeDtypeStruct(z.shape, jnp.uint32),
        grid=(N1 // kb, C // W, P),
        in_specs=specs,
        out_specs=blk,
        compiler_params=_cp(("parallel", "parallel", "parallel")),
        name="hyena_fft_mid",
    )(*args)


def _fft_c_kernel(y_ref, g_ref, x_ref, o_ref, ys_ref, xa_ref, xb_ref):
    n1h = x_ref.shape[1]
    _to_slabs(y_ref, ys_ref)
    _to_slabs(x_ref, xa_ref, (0,))
    _to_slabs(x_ref, xb_ref, (1,))
    for r in range(FFT_RB):
        yr, yi = _unpack_complex(_slab_rows(ys_ref, r, 2 * n1h))
        ys = jnp.concatenate([yr, yi], axis=0).astype(BF16)
        c = jnp.dot(g_ref[r], ys, preferred_element_type=F32)
        _set_slab_rows(xa_ref, r, _slab_rows(xa_ref, r, n1h) * c[:n1h])
        _set_slab_rows(xb_ref, r, _slab_rows(xb_ref, r, n1h) * c[n1h:])
    _from_slabs(xa_ref, o_ref, (0,))
    _from_slabs(xb_ref, o_ref, (1,))


def fft_stage_c(y, g, x4, xcol0, W=512):
    P, N1, N2, C = y.shape
    N1h = N1 // 2
    W = min(W, C)
    assert C % W == 0 and xcol0 % W == 0
    xb = xcol0 // W
    slab = lambda n, dt: pltpu.VMEM((W // LANES, n * FFT_RB, LANES), dt)
    return pl.pallas_call(
        _fft_c_kernel,
        out_shape=jax.ShapeDtypeStruct((2 * P, N1h, N2, C), F32),
        grid=(P, N2 // FFT_RB, C // W),
        in_specs=[pl.BlockSpec((None, N1, FFT_RB, W), lambda p, j, c: (p, 0, j, c)),
                  pl.BlockSpec((FFT_RB, N1, 2 * N1), lambda p, j, c: (j, 0, 0)),
                  pl.BlockSpec((2, N1h, FFT_RB, W), lambda p, j, c: (p, 0, j, xb + c))],
        out_specs=pl.BlockSpec((2, N1h, FFT_RB, W), lambda p, j, c: (p, 0, j, c)),
        scratch_shapes=[slab(N1, jnp.uint32), slab(N1h, F32), slab(N1h, F32)],
        compiler_params=_cp(("parallel", "parallel", "parallel")),
        name="hyena_fft_c",
    )(y, g, x4)


def hyena_mixer(uc, B, L, w1, b1, freq, w2, b2, w3, skip):
    C = HY_WIDTH
    N2 = FFT_N2
    N1h = L // N2
    tabs = _fft_tables(L)
    uc4 = uc.reshape(B, N1h, N2, uc.shape[1])
    filt = hyena_filters(w1, b1, freq, w2, b2, w3, skip, L)
    spec = fft_stage_mid(fft_stage_a(filt.reshape(2 * HY_ORDER, N1h, N2, C), 0, C, tabs["g_flt"]), tabs)
    y = fft_stage_mid(fft_stage_a(uc4, 0, C, tabs["g_fwd"]), tabs, h=spec, order=0)
    z1 = fft_stage_c(y, tabs["g_inv"], uc4, C)
    y = fft_stage_mid(fft_stage_a(z1, 0, C, tabs["g_fwd"]), tabs, h=spec, order=1)
    z = fft_stage_c(y, tabs["g_inv"], uc4, 2 * C)
    return z.reshape(B * L, C)


def _s5_tables(a_re, a_im, log_dt, b_re, b_im, c_re, c_im, n_chunks):
    T = S5_T
    G, P, I = S5_GROUPS, S5_STATE, S5_GROUP
    f32 = lambda a: a.astype(F32)
    a_re, a_im, b_re, b_im, c_re, c_im = map(f32, (a_re, a_im, b_re, b_im, c_re, c_im))
    step = jnp.exp(f32(log_dt))[..., None]
    lr, li = a_re * step, a_im * step
    mag = jnp.exp(lr)
    br_, bi_ = mag * jnp.cos(li), mag * jnp.sin(li)
    den = a_re * a_re + a_im * a_im
    qr = ((br_ - 1.0) * a_re + bi_ * a_im) / den
    qi = (bi_ * a_re - (br_ - 1.0) * a_im) / den
    bbr = qr[..., None] * b_re - qi[..., None] * b_im
    bbi = qr[..., None] * b_im + qi[..., None] * b_re

    def lam_pow(k):
        k = k.astype(F32)
        m = jnp.exp(lr[..., None] * k)
        return m * jnp.cos(li[..., None] * k), m * jnp.sin(li[..., None] * k)

    lags = jnp.arange(T + 1)
    pr, pi_ = lam_pow(lags)
    cpr = c_re[..., None] * pr[:, :, None] - c_im[..., None] * pi_[:, :, None]
    cpi = c_re[..., None] * pi_[:, :, None] + c_im[..., None] * pr[:, :, None]
    kern = jnp.einsum("dgjpk,dgpi->dgkji", cpr, bbr) - jnp.einsum("dgjpk,dgpi->dgkji", cpi, bbi)
    tt = jnp.arange(T)
    lag = tt[:, None] - tt[None, :]
    sel_f = (lag[:, :, None] == lags).astype(F32)
    sel_b = (-lag[:, :, None] == lags).astype(F32)
    toep = (jnp.einsum("tuk,gkji->gtjui", sel_f, kern[0], precision=lax.Precision.HIGHEST)
            + jnp.einsum("tuk,gkji->gtjui", sel_b, kern[1], precision=lax.Precision.HIGHEST))
    toep = toep.reshape(G, T * I, T * I)
    ef = T - 1 - tt
    eb = tt

    def bst(d, e):
        wr, wi = pr[d][..., e], pi_[d][..., e]
        re = wr[..., None] * bbr[d][:, :, None] - wi[..., None] * bbi[d][:, :, None]
        im = wr[..., None] * bbi[d][:, :, None] + wi[..., None] * bbr[d][:, :, None]
        return jnp.concatenate([re, im], axis=1).reshape(G, 2 * P, T * I)

    bst_all = jnp.stack([bst(0, ef), bst(1, eb)], axis=1)

    def cst(d, e):
        xr, xi = cpr[d][..., e], cpi[d][..., e]
        m = jnp.concatenate([xr, -xi], axis=2)
        return m.transpose(0, 3, 1, 2).reshape(G, T * I, 2 * P)

    cst_all = jnp.stack([cst(0, tt + 1), cst(1, T - tt)], axis=1)
    nsteps = max(1, int(math.log2(n_chunks)))
    e2 = T * (2 ** jnp.arange(nsteps))
    ar, ai = lam_pow(e2)
    ap = jnp.stack([ar, ai], axis=-1).transpose(1, 0, 3, 4, 2)
    ap = jnp.broadcast_to(ap[..., None], ap.shape + (LANES,))
    return toep.astype(BF16), bst_all.astype(BF16), cst_all.astype(BF16), ap, nsteps


def _s5_kernel(u_ref, toep_ref, bst_ref, cst_ref, ap_ref, y_ref, *, n_chunks, nsteps):
    T, I, cols = u_ref.shape
    P = S5_STATE
    u = u_ref[...].reshape(T * I, cols).astype(BF16)
    y = jnp.dot(toep_ref[...], u, preferred_element_type=F32)
    cidx = lax.broadcasted_iota(jnp.int32, (P, cols), 1) % n_chunks

    def shifted(x, sh, d):
        if d == 0:
            return jnp.where(cidx >= sh, pltpu.roll(x, sh, axis=1), 0.0)
        return jnp.where(cidx < n_chunks - sh, pltpu.roll(x, cols - sh, axis=1), 0.0)

    for d in range(2):
        v = jnp.dot(bst_ref[d], u, preferred_element_type=F32)
        sr, si = v[:P], v[P:]
        for j in range(nsteps):
            if (1 << j) >= n_chunks:
                break
            ar, ai = ap_ref[d, j, 0][:, :1], ap_ref[d, j, 1][:, :1]
            rr, ri = shifted(sr, 1 << j, d), shifted(si, 1 << j, d)
            sr, si = sr + ar * rr - ai * ri, si + ar * ri + ai * rr
        s_in = jnp.concatenate([shifted(sr, 1, d), shifted(si, 1, d)], axis=0).astype(BF16)
        y = y + jnp.dot(cst_ref[d], s_in, preferred_element_type=F32)
    y_ref[...] = y.reshape(T, I, cols)


def _s5_out_kernel(y_ref, u_ref, d_ref, w_ref, o_ref):
    y = y_ref[...] + d_ref[...] * u_ref[...]
    y = jax.nn.gelu(y, approximate=True)
    z = jnp.dot(y.astype(BF16), w_ref[...], preferred_element_type=F32)
    o_ref[...] = (y * jax.nn.sigmoid(z)).astype(o_ref.dtype)


def s5_mixer(p, col0, B, L, a_re, a_im, log_dt, b_re, b_im, c_re, c_im, d, w_glu, tm=1024):
    T, Wd, I, G = S5_T, S5_WIDTH, S5_GROUP, S5_GROUPS
    nC = L // T
    cols = B * nC
    assert L % T == 0 and nC & (nC - 1) == 0 and cols % LANES == 0
    toep, bst, cst, ap, nsteps = _s5_tables(a_re, a_im, log_dt, b_re, b_im, c_re, c_im, nC)
    u = p[:, col0:col0 + Wd]
    ut = u.reshape(B, nC, T, Wd).transpose(2, 3, 0, 1).reshape(T, Wd, cols)
    blk = pl.BlockSpec((T, I, cols), lambda g: (0, g, 0))
    yt = pl.pallas_call(
        functools.partial(_s5_kernel, n_chunks=nC, nsteps=nsteps),
        out_shape=jax.ShapeDtypeStruct((T, Wd, cols), F32),
        grid=(G,),
        in_specs=[blk,
                  pl.BlockSpec((None, T * I, T * I), lambda g: (g, 0, 0)),
                  pl.BlockSpec((None, 2, 2 * S5_STATE, T * I), lambda g: (g, 0, 0, 0)),
                  pl.BlockSpec((None, 2, T * I, 2 * S5_STATE), lambda g: (g, 0, 0, 0)),
                  pl.BlockSpec((None, 2, nsteps, 2, S5_STATE, LANES), lambda g: (g, 0, 0, 0, 0, 0))],
        out_specs=blk,
        compiler_params=_cp(("parallel",)),
        name="s5_scan",
    )(ut, toep, bst, cst, ap)
    y = yt.reshape(T, Wd, B, nC).transpose(2, 3, 0, 1).reshape(B * L, Wd)
    M = B * L
    tm = min(tm, M)
    cb = col0 // Wd
    assert col0 % Wd == 0 and M % tm == 0
    return pl.pallas_call(
        _s5_out_kernel,
        out_shape=jax.ShapeDtypeStruct((M, Wd), BF16),
        grid=(M // tm,),
        in_specs=[pl.BlockSpec((tm, Wd), lambda i: (i, 0)),
                  pl.BlockSpec((tm, Wd), lambda i: (i, cb)),
                  pl.BlockSpec((1, Wd), lambda i: (0, 0)),
                  pl.BlockSpec((Wd, Wd), lambda i: (0, 0))],
        out_specs=pl.BlockSpec((tm, Wd), lambda i: (i, 0)),
        compiler_params=_cp(("parallel",)),
        name="s5_glu",
    )(y, p, d.astype(F32).reshape(1, Wd), w_glu.astype(BF16))


def _ret_kernel(q_ref, k_ref, v_ref, g_ref, cos_ref, sin_ref, dm_ref, qs_ref, ks_ref, cd_ref, o_ref, st_ref,
                carry_ref, *, n_chunks):
    c = pl.program_id(2)
    nC = n_chunks
    nS = nC // RET_SUB
    Cc = dm_ref.shape[0]
    dh = k_ref.shape[1]
    h2 = dh // 2

    def rows(ref, u):
        return ref[u * Cc:(u + 1) * Cc, :].astype(F32)

    def rot(x, u):
        cos, sin = rows(cos_ref, u), rows(sin_ref, u)
        x1, x2 = x[:, :h2], x[:, h2:]
        return jnp.concatenate([x1 * cos - x2 * sin, x1 * sin + x2 * cos], axis=-1)

    @pl.when(c < nS)
    def _():
        for u in range(RET_SUB):
            kt = (rot(rows(k_ref, u), u) * (dh ** -0.5)).T
            v = v_ref[u * Cc:(u + 1) * Cc, :].astype(BF16)
            for d in range(2):
                st_ref[d, c * RET_SUB + u] = jnp.dot((kt * ks_ref[d]).astype(BF16), v, preferred_element_type=F32)

    @pl.when(c == nS)
    def _():
        for d in range(2):
            cd = cd_ref[d, 0:1, 0:1]
            carry_ref[...] = jnp.zeros_like(carry_ref)

            def body(i, carry, d=d, cd=cd):
                idx = i if d == 0 else nC - 1 - i
                t = st_ref[d, idx]
                st_ref[d, idx] = carry_ref[...]
                carry_ref[...] = carry_ref[...] * cd + t
                return carry

            lax.fori_loop(0, nC, body, 0)

    @pl.when(c >= nS)
    def _():
        for u in range(RET_SUB):
            ch = (c - nS) * RET_SUB + u
            qr = rot(rows(q_ref, u), u)
            kr = rot(rows(k_ref, u), u) * (dh ** -0.5)
            v = v_ref[u * Cc:(u + 1) * Cc, :].astype(BF16)
            s = lax.dot_general(qr.astype(BF16), kr.astype(BF16), (((1,), (1,)), ((), ())),
                                preferred_element_type=F32) * dm_ref[...]
            o = jnp.dot(s.astype(BF16), v, preferred_element_type=F32)
            for d in range(2):
                o = o + jnp.dot((qr * qs_ref[d]).astype(BF16), st_ref[d, ch].astype(BF16),
                                preferred_element_type=F32)
            mu = jnp.mean(o, axis=-1, keepdims=True)
            oc = o - mu
            var = jnp.mean(oc * oc, axis=-1, keepdims=True)
            g = rows(g_ref, u)
            o_ref[u * Cc:(u + 1) * Cc, :] = (oc * lax.rsqrt(var + GN_EPS)
                                             * (g * jax.nn.sigmoid(g))).astype(o_ref.dtype)


def retention_mixer(p, B, L, ret_decay):
    H, dh = RET_HEADS, RET_HEAD_DIM
    Cc = min(RET_CC, L)
    nC = L // Cc
    assert L % Cc == 0
    lg = -jnp.exp(ret_decay.astype(F32))
    pos = jnp.arange(Cc, dtype=F32)
    rel = pos[:, None] - pos[None, :]
    lf, lb = lg[0][:, None, None], lg[1][:, None, None]
    dm = jnp.where(rel >= 0, jnp.exp(jnp.maximum(rel, 0.0) * lf), jnp.exp(jnp.maximum(-rel, 0.0) * lb))
    qs = jnp.stack([jnp.exp((pos + 1.0)[None] * lg[0][:, None]),
                    jnp.exp((Cc - pos)[None] * lg[1][:, None])], axis=1)[..., None]
    ks = jnp.stack([jnp.exp((Cc - 1.0 - pos)[None] * lg[0][:, None]),
                    jnp.exp(pos[None] * lg[1][:, None])], axis=1)[:, :, None, :]
    cd = jnp.broadcast_to(jnp.exp(Cc * lg).T[:, :, None, None], (H, 2, SUBLANES, LANES))
    inv = ROPE_BASE ** (-jnp.arange(0, dh, 2, dtype=F32) / dh)
    ang = jnp.arange(L, dtype=F32)[:, None] * inv[None, :]
    cos, sin = jnp.cos(ang), jnp.sin(ang)

    assert nC % RET_SUB == 0
    nS = nC // RET_SUB
    rb = RET_SUB * Cc

    def kch(c):
        return jnp.where(c < nS, c, c - nS)

    def qch(c):
        return jnp.maximum(c - nS, 0)

    kv = lambda off: pl.BlockSpec((rb, dh), lambda b, h, c: (b * nS + kch(c), off * H + h))
    qo = lambda off: pl.BlockSpec((rb, dh), lambda b, h, c: (b * nS + qch(c), off * H + h))
    tab = pl.BlockSpec((rb, dh // 2), lambda b, h, c: (kch(c), 0))
    return pl.pallas_call(
        functools.partial(_ret_kernel, n_chunks=nC),
        out_shape=jax.ShapeDtypeStruct((B * L, H * dh), BF16),
        grid=(B, H, 2 * nS),
        in_specs=[qo(0), kv(1), kv(2), qo(3), tab, tab,
                  pl.BlockSpec((None, Cc, Cc), lambda b, h, c: (h, 0, 0)),
                  pl.BlockSpec((None, 2, Cc, 1), lambda b, h, c: (h, 0, 0, 0)),
                  pl.BlockSpec((None, 2, 1, Cc), lambda b, h, c: (h, 0, 0, 0)),
                  pl.BlockSpec((None, 2, SUBLANES, LANES), lambda b, h, c: (h, 0, 0, 0))],
        out_specs=qo(0),
        scratch_shapes=[pltpu.VMEM((2, nC, dh, dh), F32), pltpu.VMEM((dh, dh), F32)],
        compiler_params=_cp(("parallel", "parallel", "arbitrary")),
        name="retention",
    )(p, p, p, p, cos, sin, dm, qs, ks, cd)


def _na_bias_tables(rpb):
    Wc, WR, WC = GRID_W, NA_WIN_ROWS, NA_WIN_COLS
    hi = lax.Precision.HIGHEST
    rpb = rpb.astype(F32)
    c = jnp.arange(Wc)[:, None]
    kc = jnp.arange(Wc)[None, :]
    cs = jnp.clip(c - WC // 2, 0, Wc - WC)
    valid = (kc >= cs) & (kc < cs + WC)
    csel = (((kc - c + (WC - 1))[:, :, None] == jnp.arange(2 * WC - 1)) & valid[:, :, None]).astype(F32)
    dj = jnp.arange(WR)[None, :] - jnp.arange(WR)[:, None] + (WR - 1)
    rsel = (dj[:, :, None] == jnp.arange(2 * WR - 1)).astype(F32)
    t = jnp.einsum("hrs,cks->hrck", rpb, csel, precision=hi)
    t = jnp.einsum("djr,hrck->hdcjk", rsel, t, precision=hi)
    t = jnp.where(valid[None, None, :, None, :], t, NEG_INF)
    t = t.reshape(NA_HEADS // NA_HG, NA_HG, WR, Wc, WR * Wc).transpose(0, 2, 1, 3, 4)
    return t.reshape(NA_HEADS // NA_HG, WR, NA_HG * Wc, WR * Wc)


def _na_kernel(q_ref, k_ref, v_ref, b_ref, o_ref, *, rows):
    Wc, WR = GRID_W, NA_WIN_ROWS
    hw = NA_HG * NA_HEAD_DIM
    hq = NA_HG * Wc
    scale = NA_HEAD_DIM ** -0.5
    rb = pl.program_id(2)
    own = (lax.broadcasted_iota(jnp.int32, (hq, hw), 0) // Wc
           == lax.broadcasted_iota(jnp.int32, (hq, hw), 1) // NA_HEAD_DIM)

    for i in range(NA_RB):
        r = rb * NA_RB + i
        rs = jnp.clip(r - WR // 2, 0, rows - WR)
        q = q_ref[i * Wc:(i + 1) * Wc, :]
        qs = jnp.where(own, jnp.concatenate([q] * NA_HG, axis=0), jnp.zeros((), q.dtype))
        k0 = pl.multiple_of(rs * Wc, Wc)
        kw = k_ref[pl.ds(k0, WR * Wc), :]
        vw = v_ref[pl.ds(k0, WR * Wc), :]
        s = lax.dot_general(qs, kw, (((1,), (1,)), ((), ())), preferred_element_type=F32) * scale + b_ref[r - rs]
        m = jnp.max(s, axis=-1, keepdims=True)
        e = jnp.exp(s - m)
        l = jnp.sum(e, axis=-1, keepdims=True)
        o = jnp.dot(e.astype(BF16), vw, preferred_element_type=F32) / l
        o = jnp.where(own, o, 0.0)
        out = o[0:Wc]
        for h in range(1, NA_HG):
            out = out + o[h * Wc:(h + 1) * Wc]
        o_ref[i * Wc:(i + 1) * Wc, :] = out.astype(o_ref.dtype)


def neighborhood_mixer(qkv, col0, B, L, rpb):
    Wc = GRID_W
    rows = L // Wc
    assert rows >= NA_WIN_ROWS and rows % NA_RB == 0
    hw = NA_HG * NA_HEAD_DIM
    nhg = NA_HEADS // NA_HG
    assert col0 % hw == 0
    c0 = col0 // hw
    bias = _na_bias_tables(rpb)
    nrb = rows // NA_RB
    return pl.pallas_call(
        functools.partial(_na_kernel, rows=rows),
        out_shape=jax.ShapeDtypeStruct((B * L, NA_WIDTH), BF16),
        grid=(B, nhg, nrb),
        in_specs=[pl.BlockSpec((NA_RB * Wc, hw), lambda b, g, r: (b * nrb + r, c0 + g)),
                  pl.BlockSpec((L, hw), lambda b, g, r: (b, c0 + nhg + g)),
                  pl.BlockSpec((L, hw), lambda b, g, r: (b, c0 + 2 * nhg + g)),
                  pl.BlockSpec((None, NA_WIN_ROWS, NA_HG * Wc, NA_WIN_ROWS * Wc), lambda b, g, r: (g, 0, 0, 0))],
        out_specs=pl.BlockSpec((NA_RB * Wc, hw), lambda b, g, r: (b * nrb + r, g)),
        compiler_params=_cp(("parallel", "parallel", "arbitrary")),
        name="neighborhood_attention",
    )(qkv, qkv, qkv, bias)


def _trunk(x, mem, B, L, prm, wb):
    n_mem = mem.shape[0] // B
    depth = prm["norm_g"].shape[0]
    hw3 = 3 * HY_WIDTH
    for layer in range(depth):
        i = layer // 2
        g = prm["norm_g"][layer]
        wo = wb["mix_wo"]
        if layer % 2 == 0:
            ident = jnp.zeros((3, S5_WIDTH), F32).at[1].set(1.0)
            cw = jnp.concatenate([prm["hy_short_w"][i].astype(F32), ident], axis=1)
            cb = jnp.concatenate([prm["hy_short_b"][i].astype(F32), jnp.zeros((S5_WIDTH,), F32)])
            uc = norm_matmul_conv(x, g[0], wb["ev_w_in"], cw, cb, L, layer=i)
            z = hyena_mixer(uc, B, L, prm["hy_w1"][i], prm["hy_b1"][i], prm["hy_freq"][i], prm["hy_w2"][i],
                            prm["hy_b2"][i], prm["hy_w3"][i], prm["hy_skip"][i])
            ss = s5_mixer(uc, hw3, B, L, prm["s5_a_re"][i], prm["s5_a_im"][i], prm["s5_log_dt"][i],
                          prm["s5_b_re"][i], prm["s5_b_im"][i], prm["s5_c_re"][i], prm["s5_c_im"][i],
                          prm["s5_d"][i], prm["s5_w_glu"][i])
            ops = [(z, wo, prm["hy_out_g"][i], 0), (ss, wo, None, HY_WIDTH)]
        else:
            p = norm_matmul(x, g[0], wb["od_w_in"], BF16, layer=i, name="odd_in_proj")
            ret = retention_mixer(p, B, L, prm["ret_decay"][i])
            na = neighborhood_mixer(p, 4 * RET_WIDTH, B, L, prm["na_rpb"][i])
            ops = [(ret, wo, None, 0), (na, wo, None, RET_WIDTH)]
        x, xn = matmul_norm_residual(ops, g[1], x, g[2], layer=layer, name="mix_out_proj")
        kv = norm_matmul(mem, prm["mem_norm_g"][layer], wb["xa_wkv"], BF16, layer=layer, name="xattn_kv_proj")
        x = xattn_block(xn, x, wb["xa_wq"], kv, wb["xa_wo"], g[3], B, L, n_mem, layer=layer)
        x = ffn_block(x, g[4], wb["ffn_wg"], wb["ffn_wu"], wb["ffn_wd"], g[5], layer=layer)
    return x


def kernel(x_prompt, x_sample, mem_prompt, mem_sample, norm_g, mix_wo, ev_w_in, hy_short_w, hy_short_b, hy_w1, hy_b1, hy_freq, hy_w2, hy_b2, hy_w3, hy_skip, hy_out_g, s5_a_re, s5_a_im, s5_log_dt, s5_b_re, s5_b_im, s5_c_re, s5_c_im, s5_d, s5_w_glu, od_w_in, ret_decay, na_rpb, mem_norm_g, xa_wq, xa_wkv, xa_wo, ffn_wg, ffn_wu, ffn_wd):
    prm = dict(norm_g=norm_g, hy_short_w=hy_short_w, hy_short_b=hy_short_b, hy_w1=hy_w1, hy_b1=hy_b1,
               hy_freq=hy_freq, hy_w2=hy_w2, hy_b2=hy_b2, hy_w3=hy_w3, hy_skip=hy_skip, hy_out_g=hy_out_g,
               s5_a_re=s5_a_re, s5_a_im=s5_a_im, s5_log_dt=s5_log_dt, s5_b_re=s5_b_re, s5_b_im=s5_b_im,
               s5_c_re=s5_c_re, s5_c_im=s5_c_im, s5_d=s5_d, s5_w_glu=s5_w_glu, ret_decay=ret_decay,
               na_rpb=na_rpb, mem_norm_g=mem_norm_g)
    wb = {k: v.astype(BF16) for k, v in dict(mix_wo=mix_wo, ev_w_in=ev_w_in, od_w_in=od_w_in, xa_wq=xa_wq,
                                             xa_wkv=xa_wkv, xa_wo=xa_wo, ffn_wg=ffn_wg, ffn_wu=ffn_wu,
                                             ffn_wd=ffn_wd).items()}
    outs = []
    for x, mem in ((x_prompt, mem_prompt), (x_sample, mem_sample)):
        B, L, D = x.shape
        y = _trunk(x.reshape(B * L, D), mem.reshape(-1, D), B, L, prm, wb)
        outs.append(y.reshape(B, L, D))
    return tuple(outs)
```

```python
import functools
import math

import jax
import jax.numpy as jnp
from jax import lax
from jax.experimental import pallas as pl
from jax.experimental.pallas import tpu as pltpu

F32 = jnp.float32
BF16 = jnp.bfloat16

V7X_VMEM_BYTES = 64 * 1024 * 1024
VMEM_LIMIT = V7X_VMEM_BYTES - 8 * 1024 * 1024
LANES = 128
SUBLANES = 8

D_MODEL = 2048
GRID_W = 64
HY_WIDTH = 3 * D_MODEL // 4
S5_WIDTH = D_MODEL - HY_WIDTH
S5_GROUP = 16
S5_GROUPS = S5_WIDTH // S5_GROUP
S5_STATE = 64
HY_ORDER = 2
HY_BANDS = 16
HY_FILTER_HIDDEN = 64
HY_DECAY_TARGET = 1e-2
HY_SHORT_DECAY_PCT = 0.3
HY_LONG_DECAY_PCT = 1.5
RET_WIDTH = D_MODEL // 2
RET_HEADS = 4
RET_HEAD_DIM = RET_WIDTH // RET_HEADS
ROPE_BASE = 10000.0
NA_WIDTH = D_MODEL - RET_WIDTH
NA_HEADS = 16
NA_HEAD_DIM = NA_WIDTH // NA_HEADS
NA_WIN_ROWS = 8
NA_WIN_COLS = 16
XA_HEADS = 4
XA_HEAD_DIM = D_MODEL // XA_HEADS
RMS_EPS = 1e-6
GN_EPS = 1e-6

FFT_N2 = 128
FFT_RB = SUBLANES
S5_T = 32
RET_CC = 256
RET_SUB = 2
FFN_SPLIT = 2
NA_HG = 4
NA_RB = 8
NEG_INF = -1e30


def _cp(sem, vmem=VMEM_LIMIT):
    return pltpu.CompilerParams(dimension_semantics=sem, vmem_limit_bytes=vmem)


def _rms(x, g, eps=RMS_EPS):
    return x * lax.rsqrt(jnp.mean(x * x, axis=-1, keepdims=True) + eps) * g


def _norm_mm_kernel(x_ref, g_ref, w_ref, o_ref, xn_ref):
    @pl.when(pl.program_id(1) == 0)
    def _():
        tm = x_ref.shape[0]
        rc = min(tm, 256)
        for r in range(0, tm, rc):
            xn_ref[r:r + rc, :] = _rms(x_ref[r:r + rc, :], g_ref[...]).astype(BF16)

    o_ref[...] = jnp.dot(xn_ref[...], w_ref[...], preferred_element_type=F32).astype(o_ref.dtype)


def _stacked(w):
    return w if w.ndim == 3 else w[None]


def norm_matmul(x, g, w, out_dtype, col0=0, ncols=None, layer=0, tm=1024, tn=512, name="norm_matmul"):
    M, K = x.shape
    w = _stacked(w)
    N = w.shape[2] - col0 if ncols is None else ncols
    tm, tn = min(tm, M), min(tn, N)
    assert M % tm == 0 and N % tn == 0 and col0 % tn == 0
    cb0 = col0 // tn
    return pl.pallas_call(
        _norm_mm_kernel,
        out_shape=jax.ShapeDtypeStruct((M, N), out_dtype),
        grid=(M // tm, N // tn),
        in_specs=[pl.BlockSpec((tm, K), lambda i, j: (i, 0)),
                  pl.BlockSpec((1, K), lambda i, j: (0, 0)),
                  pl.BlockSpec((None, K, tn), lambda i, j: (layer, 0, cb0 + j))],
        out_specs=pl.BlockSpec((tm, tn), lambda i, j: (i, j)),
        scratch_shapes=[pltpu.VMEM((tm, K), BF16)],
        compiler_params=_cp(("parallel", "arbitrary")),
        name=name,
    )(x, g.reshape(1, K), w)


def _mm_norm_res_kernel(*refs, n_ops, prenorm):
    pos = 0
    y = None
    for t in range(n_ops):
        a = refs[pos][...]
        w_ref = refs[pos + 1]
        pos += 2
        if prenorm[t]:
            a = _rms(a.astype(F32), refs[pos][...])
            pos += 1
        d = jnp.dot(a.astype(BF16), w_ref[...], preferred_element_type=F32)
        y = d if y is None else y + d
    g_ref, x_ref, gn_ref, o_ref, on_ref = refs[pos:pos + 5]
    x1 = x_ref[...] + _rms(y, g_ref[...])
    o_ref[...] = x1
    on_ref[...] = _rms(x1, gn_ref[...]).astype(on_ref.dtype)


def matmul_norm_residual(ops, g, x, g_next, layer=0, tm=512, name="matmul_norm_residual"):
    M, N = x.shape
    tm = min(tm, M)
    assert M % tm == 0
    args, specs, prenorm = [], [], []
    for a, w, pg, k0 in ops:
        kt = a.shape[1]
        assert k0 % kt == 0
        args += [a, _stacked(w)]
        specs += [pl.BlockSpec((tm, kt), lambda i: (i, 0)),
                  pl.BlockSpec((None, kt, N), lambda i, kb=k0 // kt: (layer, kb, 0))]
        prenorm.append(pg is not None)
        if pg is not None:
            args.append(pg.reshape(1, kt))
            specs.append(pl.BlockSpec((1, kt), lambda i: (0, 0)))
    row = pl.BlockSpec((1, N), lambda i: (0, 0))
    tile = pl.BlockSpec((tm, N), lambda i: (i, 0))
    args += [g.reshape(1, N), x, g_next.reshape(1, N)]
    specs += [row, tile, row]
    return pl.pallas_call(
        functools.partial(_mm_norm_res_kernel, n_ops=len(ops), prenorm=tuple(prenorm)),
        out_shape=(jax.ShapeDtypeStruct((M, N), F32), jax.ShapeDtypeStruct((M, N), BF16)),
        grid=(M // tm,),
        in_specs=specs,
        out_specs=(tile, tile),
        compiler_params=_cp(("parallel",)),
        name=name,
    )(*args)


def _xattn_block_kernel(xn_ref, x_ref, wq_ref, k_ref, v_ref, wo_ref, g_ref, o_ref, *, heads):
    dh = xn_ref.shape[1] // heads
    scale = dh ** -0.5
    q = jnp.dot(xn_ref[...], wq_ref[...], preferred_element_type=F32).astype(BF16)
    outs = []
    for h in range(heads):
        sl = slice(h * dh, (h + 1) * dh)
        s = lax.dot_general(q[:, sl], k_ref[:, sl], (((1,), (1,)), ((), ())), preferred_element_type=F32) * scale
        m = jnp.max(s, axis=-1, keepdims=True)
        p = jnp.exp(s - m)
        l = jnp.sum(p, axis=-1, keepdims=True)
        outs.append((jnp.dot(p.astype(BF16), v_ref[:, sl], preferred_element_type=F32) / l).astype(BF16))
    o = jnp.concatenate(outs, axis=-1)
    y = jnp.dot(o, wo_ref[...], preferred_element_type=F32)
    o_ref[...] = x_ref[...] + _rms(y, g_ref[...])


def xattn_block(xn, x, wq, kv, wo, g, B, L, n_mem, layer=0, tm=512):
    M, D = x.shape
    tm = min(tm, L)
    assert L % tm == 0
    bpl = L // tm
    wq, wo = _stacked(wq), _stacked(wo)
    wspec = pl.BlockSpec((None, D, D), lambda i: (layer, 0, 0), pipeline_mode=pl.Buffered(1))
    return pl.pallas_call(
        functools.partial(_xattn_block_kernel, heads=XA_HEADS),
        out_shape=jax.ShapeDtypeStruct((M, D), F32),
        grid=(M // tm,),
        in_specs=[pl.BlockSpec((tm, D), lambda i: (i, 0)),
                  pl.BlockSpec((tm, D), lambda i: (i, 0)),
                  wspec,
                  pl.BlockSpec((n_mem, D), lambda i: (i // bpl, 0)),
                  pl.BlockSpec((n_mem, D), lambda i: (i // bpl, 1)),
                  wspec,
                  pl.BlockSpec((1, D), lambda i: (0, 0))],
        out_specs=pl.BlockSpec((tm, D), lambda i: (i, 0)),
        compiler_params=_cp(("parallel",)),
        name="xattn_block",
    )(xn, x, wq, kv, kv, wo, g.reshape(1, D))


def _ffn_kernel(x_ref, gi_ref, wg_ref, wu_ref, wd_ref, go_ref, o_ref, xn_ref):
    j = pl.program_id(1)

    tm = x_ref.shape[0]
    rc = min(tm, 256)

    @pl.when(j == 0)
    def _():
        for r in range(0, tm, rc):
            xn_ref[r:r + rc, :] = _rms(x_ref[r:r + rc, :], gi_ref[...]).astype(BF16)
        o_ref[...] = jnp.zeros_like(o_ref)

    th = wg_ref.shape[1]
    hh = th // FFN_SPLIT
    for c in range(FFN_SPLIT):
        xn = xn_ref[...]
        a = jnp.dot(xn, wg_ref[:, c * hh:(c + 1) * hh], preferred_element_type=F32)
        u = jnp.dot(xn, wu_ref[:, c * hh:(c + 1) * hh], preferred_element_type=F32)
        h = (a * jax.nn.sigmoid(a) * u).astype(BF16)
        o_ref[...] += jnp.dot(h, wd_ref[c * hh:(c + 1) * hh, :], preferred_element_type=F32)

    @pl.when(j == pl.num_programs(1) - 1)
    def _():
        for r in range(0, tm, rc):
            o_ref[r:r + rc, :] = x_ref[r:r + rc, :] + _rms(o_ref[r:r + rc, :], go_ref[...])


def ffn_block(x, g_in, wg, wu, wd, g_out, layer=0, tm=1024, th=512):
    M, D = x.shape
    wg, wu, wd = _stacked(wg), _stacked(wu), _stacked(wd)
    Hd = wg.shape[2]
    tm, th = min(tm, M), min(th, Hd)
    assert M % tm == 0 and Hd % th == 0
    return pl.pallas_call(
        _ffn_kernel,
        out_shape=jax.ShapeDtypeStruct((M, D), F32),
        grid=(M // tm, Hd // th),
        in_specs=[pl.BlockSpec((tm, D), lambda i, j: (i, 0)),
                  pl.BlockSpec((1, D), lambda i, j: (0, 0)),
                  pl.BlockSpec((None, D, th), lambda i, j: (layer, 0, j)),
                  pl.BlockSpec((None, D, th), lambda i, j: (layer, 0, j)),
                  pl.BlockSpec((None, th, D), lambda i, j: (layer, j, 0)),
                  pl.BlockSpec((1, D), lambda i, j: (0, 0))],
        out_specs=pl.BlockSpec((tm, D), lambda i, j: (i, 0)),
        scratch_shapes=[pltpu.VMEM((tm, D), BF16)],
        compiler_params=_cp(("parallel", "arbitrary")),
        name="ffn_block",
    )(x, g_in.reshape(1, D), wg, wu, wd, g_out.reshape(1, D))


HALO = 16


def _norm_mm_conv_kernel(xp_ref, xc_ref, xn_ref, g_ref, w_ref, cw_ref, cb_ref, o_ref, xs_ref, *, blocks_per_seq):
    tm = xc_ref.shape[0]

    @pl.when(pl.program_id(1) == 0)
    def _():
        li = pl.program_id(0) % blocks_per_seq
        g = g_ref[...]
        keep_prev = jnp.where(li == 0, 0.0, 1.0)
        keep_next = jnp.where(li == blocks_per_seq - 1, 0.0, 1.0)
        xs_ref[0:HALO] = (_rms(xp_ref[...], g) * keep_prev).astype(BF16)
        rc = min(tm, 256)
        for r in range(0, tm, rc):
            xs_ref[HALO + r:HALO + r + rc] = _rms(xc_ref[r:r + rc, :], g).astype(BF16)
        xs_ref[HALO + tm:] = (_rms(xn_ref[...], g) * keep_next).astype(BF16)

    n = tm + 2 * HALO
    tn = o_ref.shape[1]
    hw = tn // 2 if tn % (2 * LANES) == 0 else tn
    for c in range(0, tn, hw):
        cs = slice(c, c + hw)
        y = jnp.dot(xs_ref[...], w_ref[:, cs], preferred_element_type=F32)
        up = pltpu.roll(y, 1, axis=0)[HALO:HALO + tm]
        dn = pltpu.roll(y, n - 1, axis=0)[HALO:HALO + tm]
        o_ref[:, cs] = (up * cw_ref[0:1, cs] + y[HALO:HALO + tm] * cw_ref[1:2, cs] + dn * cw_ref[2:3, cs]
                        + cb_ref[:, cs])


def norm_matmul_conv(x, g, w, cw, cb, L, layer=0, tm=1024, tn=512):
    M, K = x.shape
    w = _stacked(w)
    N = w.shape[2]
    tm, tn = min(tm, L), min(tn, N)
    assert L % tm == 0 and N % tn == 0 and tm % HALO == 0
    hb = tm // HALO
    nhb = M // HALO
    return pl.pallas_call(
        functools.partial(_norm_mm_conv_kernel, blocks_per_seq=L // tm),
        out_shape=jax.ShapeDtypeStruct((M, N), F32),
        grid=(M // tm, N // tn),
        in_specs=[pl.BlockSpec((HALO, K), lambda i, j: (jnp.maximum(i * hb - 1, 0), 0)),
                  pl.BlockSpec((tm, K), lambda i, j: (i, 0)),
                  pl.BlockSpec((HALO, K), lambda i, j: (jnp.minimum((i + 1) * hb, nhb - 1), 0)),
                  pl.BlockSpec((1, K), lambda i, j: (0, 0)),
                  pl.BlockSpec((None, K, tn), lambda i, j: (layer, 0, j)),
                  pl.BlockSpec((3, tn), lambda i, j: (0, j)),
                  pl.BlockSpec((1, tn), lambda i, j: (0, j))],
        out_specs=pl.BlockSpec((tm, tn), lambda i, j: (i, j)),
        scratch_shapes=[pltpu.VMEM((tm + 2 * HALO, K), BF16)],
        compiler_params=_cp(("parallel", "arbitrary")),
        name="even_in_proj_conv",
    )(x, x, x, g.reshape(1, K), w, cw, cb.reshape(1, N))


def _filter_kernel(bands_ref, w1t_ref, w1c_ref, w1s_ref, b1_ref, f_ref, w2_ref, b2_ref, w3h_ref, w3l_ref, dl_ref,
                   sk_ref, o_ref, hid_ref, *, L, ncb):
    hi = lax.Precision.HIGHEST
    s = pl.program_id(1)
    tl = o_ref.shape[0]
    i0 = pl.program_id(0) * tl

    @pl.when(s == 0)
    def _():
        m_row = i0 + lax.broadcasted_iota(jnp.int32, (1, tl), 1)
        for d, pos in enumerate((m_row, L - m_row)):
            t = pos.astype(F32) * (1.0 / L)
            ang = (2.0 * math.pi) * bands_ref[...] * t
            pre = (w1t_ref[...] * t
                   + jnp.dot(w1c_ref[...], jnp.cos(ang), preferred_element_type=F32, precision=hi)
                   + jnp.dot(w1s_ref[...], jnp.sin(ang), preferred_element_type=F32, precision=hi)
                   + b1_ref[...])
            h = jnp.sin(f_ref[0] * pre)
            h = jnp.sin(f_ref[1] * (jnp.dot(w2_ref[...], h, preferred_element_type=F32, precision=hi) + b2_ref[...]))
            h = h.T
            h_hi = h.astype(BF16)
            hid_ref[d, 0] = h_hi
            hid_ref[d, 1] = (h - h_hi.astype(F32)).astype(BF16)

    d = (s // ncb) % 2
    m = i0 + lax.broadcasted_iota(jnp.int32, (tl, 1), 0)
    t = jnp.where(d == 0, m, L - m).astype(F32) * (1.0 / L)
    out = (jnp.dot(hid_ref[d, 0], w3h_ref[...], preferred_element_type=F32)
           + jnp.dot(hid_ref[d, 0], w3l_ref[...], preferred_element_type=F32)
           + jnp.dot(hid_ref[d, 1], w3h_ref[...], preferred_element_type=F32))
    out = out * jnp.exp(-t * dl_ref[...])
    o_ref[...] = out

    @pl.when(pl.program_id(0) == 0)
    def _():
        o_ref[0:1, :] = jnp.where(d == 0, out[0:1, :] + sk_ref[...], 0.0)


def hyena_filters(w1, b1, freq, w2, b2, w3, skip, L, tl=512, W=512):
    C = HY_WIDTH
    Hh = HY_FILTER_HIDDEN
    nb = HY_BANDS
    tl = min(tl, L)
    col = lambda v: v.astype(F32).reshape(-1, 1)
    bands = col(jnp.arange(1, nb + 1, dtype=F32))
    w1 = w1.astype(F32)
    w3_hi = w3.astype(BF16)
    w3_lo = (w3.astype(F32) - w3_hi.astype(F32)).astype(BF16)
    deltas = jnp.abs(jnp.linspace(math.log(HY_DECAY_TARGET) / HY_LONG_DECAY_PCT,
                                  math.log(HY_DECAY_TARGET) / HY_SHORT_DECAY_PCT, C, dtype=F32)).reshape(1, C)
    ncb = C // W
    small = lambda *shape: pl.BlockSpec(shape, lambda i, s: (0,) * len(shape))
    return pl.pallas_call(
        functools.partial(_filter_kernel, L=L, ncb=ncb),
        out_shape=jax.ShapeDtypeStruct((2 * HY_ORDER, L, C), F32),
        grid=(L // tl, 2 * HY_ORDER * ncb),
        in_specs=[small(nb, 1), small(Hh, 1), small(Hh, nb), small(Hh, nb), small(Hh, 1),
                  small(2, Hh, 1), small(Hh, Hh), small(Hh, 1),
                  pl.BlockSpec((Hh, W), lambda i, s: (0, s)),
                  pl.BlockSpec((Hh, W), lambda i, s: (0, s)),
                  pl.BlockSpec((1, W), lambda i, s: (0, s % ncb)),
                  pl.BlockSpec((None, 1, W), lambda i, s: (s // (2 * ncb), 0, s % ncb))],
        out_specs=pl.BlockSpec((None, tl, W), lambda i, s: (s // ncb, i, s % ncb)),
        scratch_shapes=[pltpu.VMEM((2, 2, tl, Hh), BF16)],
        compiler_params=_cp(("parallel", "arbitrary")),
        name="hyena_filters",
    )(bands, col(w1[0]), w1[1:1 + nb].T, w1[1 + nb:].T, col(b1), freq.astype(F32)[:, :, None], w2.astype(F32).T,
      col(b2), w3_hi, w3_lo, deltas, skip.astype(F32).reshape(HY_ORDER, 1, C))


def _fft_tables(L):
    N2 = FFT_N2
    N = 2 * L
    N1 = N // N2
    N1h = N1 // 2
    n2 = jnp.arange(N2, dtype=jnp.int32)[:, None, None]
    k1 = jnp.arange(N1, dtype=jnp.int32)[None, :, None]
    n1 = jnp.arange(N1h, dtype=jnp.int32)[None, None, :]
    ph = ((n1 * N2 + n2) * k1) % N
    ang = ph.astype(F32) * (2.0 * math.pi / N)
    gr, gi = jnp.cos(ang), -jnp.sin(ang)
    g_fwd = jnp.concatenate([jnp.concatenate([gr, -gi], 2), jnp.concatenate([gi, gr], 2)], 1).astype(BF16)
    sgn = jnp.where(k1 % 2 == 0, 1.0, -1.0).astype(F32)
    g_flt = jnp.concatenate([jnp.concatenate([gr, sgn * gr], 2), jnp.concatenate([gi, sgn * gi], 2)], 1).astype(BF16)
    er = jnp.swapaxes(gr, 1, 2) / N
    ei = -jnp.swapaxes(gi, 1, 2) / N
    g_inv = jnp.concatenate([jnp.concatenate([er, -ei], 2), jnp.concatenate([ei, er], 2)], 1).astype(BF16)
    a2 = (jnp.arange(N2, dtype=jnp.int32)[:, None] * jnp.arange(N2, dtype=jnp.int32)[None, :]) % N2
    ang2 = a2.astype(F32) * (2.0 * math.pi / N2)
    fr, fi = jnp.cos(ang2), -jnp.sin(ang2)
    f_mid = jnp.concatenate([jnp.concatenate([fr, -fi], 1), jnp.concatenate([fi, fr], 1)], 0).astype(BF16)
    f_mid_inv = jnp.concatenate([jnp.concatenate([fr, fi], 1), jnp.concatenate([-fi, fr], 1)], 0).astype(BF16)
    return dict(g_fwd=g_fwd, g_flt=g_flt, g_inv=g_inv, f_mid=f_mid, f_mid_inv=f_mid_inv, N1=N1)


def _pack_complex(re, im):
    r = lax.bitcast_convert_type(re.astype(BF16).astype(F32), jnp.uint32)
    i = lax.bitcast_convert_type(im.astype(BF16).astype(F32), jnp.uint32)
    return r | (i >> 16)


def _unpack_complex(w):
    re = lax.bitcast_convert_type(w & jnp.uint32(0xFFFF0000), F32)
    im = lax.bitcast_convert_type(w << 16, F32)
    return re, im


def _to_slabs(src_ref, slab_ref, lead=()):
    ns, rows, _ = slab_ref.shape
    for s in range(ns):
        slab_ref[s] = src_ref[lead + (slice(None), slice(None), slice(s * LANES, (s + 1) * LANES))].reshape(
            rows, LANES)


def _slab_rows(slab_ref, r, n):
    return jnp.concatenate([slab_ref.at[s][pl.ds(r, n, stride=FFT_RB), :] for s in range(slab_ref.shape[0])],
                           axis=1)


def _set_slab_rows(slab_ref, r, val):
    n = val.shape[0]
    for s in range(slab_ref.shape[0]):
        slab_ref.at[s][pl.ds(r, n, stride=FFT_RB), :] = val[:, s * LANES:(s + 1) * LANES]


def _from_slabs(slab_ref, dst_ref, lead=()):
    ns, rows, _ = slab_ref.shape
    for s in range(ns):
        dst_ref[lead + (slice(None), slice(None), slice(s * LANES, (s + 1) * LANES))] = (
            slab_ref[s].reshape(rows // FFT_RB, FFT_RB, LANES))


def _fft_a_kernel(x_ref, g_ref, z_ref, xa_ref, xb_ref, zs_ref):
    n1 = z_ref.shape[0]
    _to_slabs(x_ref, xa_ref, (0,))
    _to_slabs(x_ref, xb_ref, (1,))
    for r in range(FFT_RB):
        xs = jnp.concatenate([_slab_rows(xa_ref, r, n1 // 2), _slab_rows(xb_ref, r, n1 // 2)], axis=0).astype(BF16)
        a = jnp.dot(g_ref[r], xs, preferred_element_type=F32)
        _set_slab_rows(zs_ref, r, _pack_complex(a[:n1], a[n1:]))
    _from_slabs(zs_ref, z_ref)


def fft_stage_a(x4, col0, C, g, W=512):
    B, N1h, N2, _ = x4.shape
    N1 = 2 * N1h
    W = min(W, C)
    assert B % 2 == 0 and C % W == 0 and col0 % W == 0 and N2 % FFT_RB == 0
    cb0 = col0 // W
    return pl.pallas_call(
        _fft_a_kernel,
        out_shape=jax.ShapeDtypeStruct((B // 2, N1, N2, C), jnp.uint32),
        grid=(B // 2, N2 // FFT_RB, C // W),
        in_specs=[pl.BlockSpec((2, N1h, FFT_RB, W), lambda p, j, c: (p, 0, j, cb0 + c)),
                  pl.BlockSpec((FFT_RB, 2 * N1, N1), lambda p, j, c: (j, 0, 0))],
        out_specs=pl.BlockSpec((None, N1, FFT_RB, W), lambda p, j, c: (p, 0, j, c)),
        scratch_shapes=[pltpu.VMEM((W // LANES, N1h * FFT_RB, LANES), F32),
                        pltpu.VMEM((W // LANES, N1h * FFT_RB, LANES), F32),
                        pltpu.VMEM((W // LANES, N1 * FFT_RB, LANES), jnp.uint32)],
        compiler_params=_cp(("parallel", "parallel", "parallel")),
        name="hyena_fft_a",
    )(x4, g)


def _fft_mid_kernel(*refs, kb, with_filter):
    if with_filter:
        z_ref, h_ref, f_ref, fi_ref, y_ref = refs
    else:
        z_ref, f_ref, y_ref = refs
    n2 = z_ref.shape[1]
    for k in range(kb):
        zr, zi = _unpack_complex(z_ref[k])
        zs = jnp.concatenate([zr, zi], axis=0).astype(BF16)
        x = jnp.dot(f_ref[...], zs, preferred_element_type=F32)
        if with_filter:
            xr, xi = x[:n2], x[n2:]
            hr, hi = _unpack_complex(h_ref[k])
            ys = jnp.concatenate([xr * hr - xi * hi, xr * hi + xi * hr], axis=0).astype(BF16)
            x = jnp.dot(fi_ref[...], ys, preferred_element_type=F32)
        y_ref[k] = _pack_complex(x[:n2], x[n2:])


def fft_stage_mid(z, tabs, h=None, order=0, kb=8, W=512):
    P, N1, N2, C = z.shape
    kb, W = min(kb, N1), min(W, C)
    assert N1 % kb == 0 and C % W == 0
    blk = pl.BlockSpec((None, kb, N2, W), lambda k, c, p: (p, k, 0, c))
    mat = pl.BlockSpec((2 * N2, 2 * N2), lambda k, c, p: (0, 0))
    if h is None:
        args, specs = (z, tabs["f_mid"]), [blk, mat]
    else:
        hblk = pl.BlockSpec((None, kb, N2, W), lambda k, c, p: (order, k, 0, c))
        args, specs = (z, h, tabs["f_mid"], tabs["f_mid_inv"]), [blk, hblk, mat, mat]
    return pl.pallas_call(
        functools.partial(_fft_mid_kernel, kb=kb, with_filter=h is not None),
        out_shape=jax.ShapeDtypeStruct(z.shape, jnp.uint32),
        grid=(N1 // kb, C // W, P),
        in_specs=specs,
        out_specs=blk,
        compiler_params=_cp(("parallel", "parallel", "parallel")),
        name="hyena_fft_mid",
    )(*args)


def _fft_c_kernel(y_ref, g_ref, x_ref, o_ref, ys_ref, xa_ref, xb_ref):
    n1h = x_ref.shape[1]
    _to_slabs(y_ref, ys_ref)
    _to_slabs(x_ref, xa_ref, (0,))
    _to_slabs(x_ref, xb_ref, (1,))
    for r in range(FFT_RB):
        yr, yi = _unpack_complex(_slab_rows(ys_ref, r, 2 * n1h))
        ys = jnp.concatenate([yr, yi], axis=0).astype(BF16)
        c = jnp.dot(g_ref[r], ys, preferred_element_type=F32)
        _set_slab_rows(xa_ref, r, _slab_rows(xa_ref, r, n1h) * c[:n1h])
        _set_slab_rows(xb_ref, r, _slab_rows(xb_ref, r, n1h) * c[n1h:])
    _from_slabs(xa_ref, o_ref, (0,))
    _from_slabs(xb_ref, o_ref, (1,))


def fft_stage_c(y, g, x4, xcol0, W=512):
    P, N1, N2, C = y.shape
    N1h = N1 // 2
    W = min(W, C)
    assert C % W == 0 and xcol0 % W == 0
    xb = xcol0 // W
    slab = lambda n, dt: pltpu.VMEM((W // LANES, n * FFT_RB, LANES), dt)
    return pl.pallas_call(
        _fft_c_kernel,
        out_shape=jax.ShapeDtypeStruct((2 * P, N1h, N2, C), F32),
        grid=(P, N2 // FFT_RB, C // W),
        in_specs=[pl.BlockSpec((None, N1, FFT_RB, W), lambda p, j, c: (p, 0, j, c)),
                  pl.BlockSpec((FFT_RB, N1, 2 * N1), lambda p, j, c: (j, 0, 0)),
                  pl.BlockSpec((2, N1h, FFT_RB, W), lambda p, j, c: (p, 0, j, xb + c))],
        out_specs=pl.BlockSpec((2, N1h, FFT_RB, W), lambda p, j, c: (p, 0, j, c)),
        scratch_shapes=[slab(N1, jnp.uint32), slab(N1h, F32), slab(N1h, F32)],
        compiler_params=_cp(("parallel", "parallel", "parallel")),
        name="hyena_fft_c",
    )(y, g, x4)


def hyena_mixer(uc, B, L, w1, b1, freq, w2, b2, w3, skip):
    C = HY_WIDTH
    N2 = FFT_N2
    N1h = L // N2
    tabs = _fft_tables(L)
    uc4 = uc.reshape(B, N1h, N2, uc.shape[1])
    filt = hyena_filters(w1, b1, freq, w2, b2, w3, skip, L)
    spec = fft_stage_mid(fft_stage_a(filt.reshape(2 * HY_ORDER, N1h, N2, C), 0, C, tabs["g_flt"]), tabs)
    y = fft_stage_mid(fft_stage_a(uc4, 0, C, tabs["g_fwd"]), tabs, h=spec, order=0)
    z1 = fft_stage_c(y, tabs["g_inv"], uc4, C)
    y = fft_stage_mid(fft_stage_a(z1, 0, C, tabs["g_fwd"]), tabs, h=spec, order=1)
    z = fft_stage_c(y, tabs["g_inv"], uc4, 2 * C)
    return z.reshape(B * L, C)


def _s5_tables(a_re, a_im, log_dt, b_re, b_im, c_re, c_im, n_chunks):
    T = S5_T
    G, P, I = S5_GROUPS, S5_STATE, S5_GROUP
    f32 = lambda a: a.astype(F32)
    a_re, a_im, b_re, b_im, c_re, c_im = map(f32, (a_re, a_im, b_re, b_im, c_re, c_im))
    step = jnp.exp(f32(log_dt))[..., None]
    lr, li = a_re * step, a_im * step
    mag = jnp.exp(lr)
    br_, bi_ = mag * jnp.cos(li), mag * jnp.sin(li)
    den = a_re * a_re + a_im * a_im
    qr = ((br_ - 1.0) * a_re + bi_ * a_im) / den
    qi = (bi_ * a_re - (br_ - 1.0) * a_im) / den
    bbr = qr[..., None] * b_re - qi[..., None] * b_im
    bbi = qr[..., None] * b_im + qi[..., None] * b_re

    def lam_pow(k):
        k = k.astype(F32)
        m = jnp.exp(lr[..., None] * k)
        return m * jnp.cos(li[..., None] * k), m * jnp.sin(li[..., None] * k)

    lags = jnp.arange(T + 1)
    pr, pi_ = lam_pow(lags)
    cpr = c_re[..., None] * pr[:, :, None] - c_im[..., None] * pi_[:, :, None]
    cpi = c_re[..., None] * pi_[:, :, None] + c_im[..., None] * pr[:, :, None]
    kern = jnp.einsum("dgjpk,dgpi->dgkji", cpr, bbr) - jnp.einsum("dgjpk,dgpi->dgkji", cpi, bbi)
    tt = jnp.arange(T)
    lag = tt[:, None] - tt[None, :]
    sel_f = (lag[:, :, None] == lags).astype(F32)
    sel_b = (-lag[:, :, None] == lags).astype(F32)
    toep = (jnp.einsum("tuk,gkji->gtjui", sel_f, kern[0], precision=lax.Precision.HIGHEST)
            + jnp.einsum("tuk,gkji->gtjui", sel_b, kern[1], precision=lax.Precision.HIGHEST))
    toep = toep.reshape(G, T * I, T * I)
    ef = T - 1 - tt
    eb = tt

    def bst(d, e):
        wr, wi = pr[d][..., e], pi_[d][..., e]
        re = wr[..., None] * bbr[d][:, :, None] - wi[..., None] * bbi[d][:, :, None]
        im = wr[..., None] * bbi[d][:, :, None] + wi[..., None] * bbr[d][:, :, None]
        return jnp.concatenate([re, im], axis=1).reshape(G, 2 * P, T * I)

    bst_all = jnp.stack([bst(0, ef), bst(1, eb)], axis=1)

    def cst(d, e):
        xr, xi = cpr[d][..., e], cpi[d][..., e]
        m = jnp.concatenate([xr, -xi], axis=2)
        return m.transpose(0, 3, 1, 2).reshape(G, T * I, 2 * P)

    cst_all = jnp.stack([cst(0, tt + 1), cst(1, T - tt)], axis=1)
    nsteps = max(1, int(math.log2(n_chunks)))
    e2 = T * (2 ** jnp.arange(nsteps))
    ar, ai = lam_pow(e2)
    ap = jnp.stack([ar, ai], axis=-1).transpose(1, 0, 3, 4, 2)
    ap = jnp.broadcast_to(ap[..., None], ap.shape + (LANES,))
    return toep.astype(BF16), bst_all.astype(BF16), cst_all.astype(BF16), ap, nsteps


def _s5_kernel(u_ref, toep_ref, bst_ref, cst_ref, ap_ref, y_ref, *, n_chunks, nsteps):
    T, I, cols = u_ref.shape
    P = S5_STATE
    u = u_ref[...].reshape(T * I, cols).astype(BF16)
    y = jnp.dot(toep_ref[...], u, preferred_element_type=F32)
    cidx = lax.broadcasted_iota(jnp.int32, (P, cols), 1) % n_chunks

    def shifted(x, sh, d):
        if d == 0:
            return jnp.where(cidx >= sh, pltpu.roll(x, sh, axis=1), 0.0)
        return jnp.where(cidx < n_chunks - sh, pltpu.roll(x, cols - sh, axis=1), 0.0)

    for d in range(2):
        v = jnp.dot(bst_ref[d], u, preferred_element_type=F32)
        sr, si = v[:P], v[P:]
        for j in range(nsteps):
            if (1 << j) >= n_chunks:
                break
            ar, ai = ap_ref[d, j, 0][:, :1], ap_ref[d, j, 1][:, :1]
            rr, ri = shifted(sr, 1 << j, d), shifted(si, 1 << j, d)
            sr, si = sr + ar * rr - ai * ri, si + ar * ri + ai * rr
        s_in = jnp.concatenate([shifted(sr, 1, d), shifted(si, 1, d)], axis=0).astype(BF16)
        y = y + jnp.dot(cst_ref[d], s_in, preferred_element_type=F32)
    y_ref[...] = y.reshape(T, I, cols)


def _s5_out_kernel(y_ref, u_ref, d_ref, w_ref, o_ref):
    y = y_ref[...] + d_ref[...] * u_ref[...]
    y = jax.nn.gelu(y, approximate=True)
    z = jnp.dot(y.astype(BF16), w_ref[...], preferred_element_type=F32)
    o_ref[...] = (y * jax.nn.sigmoid(z)).astype(o_ref.dtype)


def s5_mixer(p, col0, B, L, a_re, a_im, log_dt, b_re, b_im, c_re, c_im, d, w_glu, tm=1024):
    T, Wd, I, G = S5_T, S5_WIDTH, S5_GROUP, S5_GROUPS
    nC = L // T
    cols = B * nC
    assert L % T == 0 and nC & (nC - 1) == 0 and cols % LANES == 0
    toep, bst, cst, ap, nsteps = _s5_tables(a_re, a_im, log_dt, b_re, b_im, c_re, c_im, nC)
    u = p[:, col0:col0 + Wd]
    ut = u.reshape(B, nC, T, Wd).transpose(2, 3, 0, 1).reshape(T, Wd, cols)
    blk = pl.BlockSpec((T, I, cols), lambda g: (0, g, 0))
    yt = pl.pallas_call(
        functools.partial(_s5_kernel, n_chunks=nC, nsteps=nsteps),
        out_shape=jax.ShapeDtypeStruct((T, Wd, cols), F32),
        grid=(G,),
        in_specs=[blk,
                  pl.BlockSpec((None, T * I, T * I), lambda g: (g, 0, 0)),
                  pl.BlockSpec((None, 2, 2 * S5_STATE, T * I), lambda g: (g, 0, 0, 0)),
                  pl.BlockSpec((None, 2, T * I, 2 * S5_STATE), lambda g: (g, 0, 0, 0)),
                  pl.BlockSpec((None, 2, nsteps, 2, S5_STATE, LANES), lambda g: (g, 0, 0, 0, 0, 0))],
        out_specs=blk,
        compiler_params=_cp(("parallel",)),
        name="s5_scan",
    )(ut, toep, bst, cst, ap)
    y = yt.reshape(T, Wd, B, nC).transpose(2, 3, 0, 1).reshape(B * L, Wd)
    M = B * L
    tm = min(tm, M)
    cb = col0 // Wd
    assert col0 % Wd == 0 and M % tm == 0
    return pl.pallas_call(
        _s5_out_kernel,
        out_shape=jax.ShapeDtypeStruct((M, Wd), BF16),
        grid=(M // tm,),
        in_specs=[pl.BlockSpec((tm, Wd), lambda i: (i, 0)),
                  pl.BlockSpec((tm, Wd), lambda i: (i, cb)),
                  pl.BlockSpec((1, Wd), lambda i: (0, 0)),
                  pl.BlockSpec((Wd, Wd), lambda i: (0, 0))],
        out_specs=pl.BlockSpec((tm, Wd), lambda i: (i, 0)),
        compiler_params=_cp(("parallel",)),
        name="s5_glu",
    )(y, p, d.astype(F32).reshape(1, Wd), w_glu.astype(BF16))


def _ret_kernel(q_ref, k_ref, v_ref, g_ref, cos_ref, sin_ref, dm_ref, qs_ref, ks_ref, cd_ref, o_ref, st_ref,
                carry_ref, *, n_chunks):
    c = pl.program_id(2)
    nC = n_chunks
    nS = nC // RET_SUB
    Cc = dm_ref.shape[0]
    dh = k_ref.shape[1]
    h2 = dh // 2

    def rows(ref, u):
        return ref[u * Cc:(u + 1) * Cc, :].astype(F32)

    blk = jnp.where(c < nS, c, c - nS)

    def rot(x, u):
        r0 = pl.multiple_of((blk * RET_SUB + u) * Cc, Cc)
        cos, sin = cos_ref[pl.ds(r0, Cc), :], sin_ref[pl.ds(r0, Cc), :]
        x1, x2 = x[:, :h2], x[:, h2:]
        return jnp.concatenate([x1 * cos - x2 * sin, x1 * sin + x2 * cos], axis=-1)

    @pl.when(c < nS)
    def _():
        for u in range(RET_SUB):
            kt = (rot(rows(k_ref, u), u) * (dh ** -0.5)).T
            v = v_ref[u * Cc:(u + 1) * Cc, :].astype(BF16)
            for d in range(2):
                st_ref[d, c * RET_SUB + u] = jnp.dot((kt * ks_ref[d]).astype(BF16), v, preferred_element_type=F32)

    @pl.when(c == nS)
    def _():
        for d in range(2):
            cd = cd_ref[d, 0:1, 0:1]
            carry_ref[...] = jnp.zeros_like(carry_ref)

            def body(i, carry, d=d, cd=cd):
                idx = i if d == 0 else nC - 1 - i
                t = st_ref[d, idx]
                st_ref[d, idx] = carry_ref[...]
                carry_ref[...] = carry_ref[...] * cd + t
                return carry

            lax.fori_loop(0, nC, body, 0)

    @pl.when(c >= nS)
    def _():
        for u in range(RET_SUB):
            ch = (c - nS) * RET_SUB + u
            qr = rot(rows(q_ref, u), u)
            kr = rot(rows(k_ref, u), u) * (dh ** -0.5)
            v = v_ref[u * Cc:(u + 1) * Cc, :].astype(BF16)
            s = lax.dot_general(qr.astype(BF16), kr.astype(BF16), (((1,), (1,)), ((), ())),
                                preferred_element_type=F32) * dm_ref[...]
            o = jnp.dot(s.astype(BF16), v, preferred_element_type=F32)
            for d in range(2):
                o = o + jnp.dot((qr * qs_ref[d]).astype(BF16), st_ref[d, ch].astype(BF16),
                                preferred_element_type=F32)
            mu = jnp.mean(o, axis=-1, keepdims=True)
            oc = o - mu
            var = jnp.mean(oc * oc, axis=-1, keepdims=True)
            g = rows(g_ref, u)
            o_ref[u * Cc:(u + 1) * Cc, :] = (oc * lax.rsqrt(var + GN_EPS)
                                             * (g * jax.nn.sigmoid(g))).astype(o_ref.dtype)


def retention_mixer(p, B, L, ret_decay):
    H, dh = RET_HEADS, RET_HEAD_DIM
    Cc = min(RET_CC, L)
    nC = L // Cc
    assert L % Cc == 0
    lg = -jnp.exp(ret_decay.astype(F32))
    pos = jnp.arange(Cc, dtype=F32)
    rel = pos[:, None] - pos[None, :]
    lf, lb = lg[0][:, None, None], lg[1][:, None, None]
    dm = jnp.where(rel >= 0, jnp.exp(jnp.maximum(rel, 0.0) * lf), jnp.exp(jnp.maximum(-rel, 0.0) * lb))
    qs = jnp.stack([jnp.exp((pos + 1.0)[None] * lg[0][:, None]),
                    jnp.exp((Cc - pos)[None] * lg[1][:, None])], axis=1)[..., None]
    ks = jnp.stack([jnp.exp((Cc - 1.0 - pos)[None] * lg[0][:, None]),
                    jnp.exp(pos[None] * lg[1][:, None])], axis=1)[:, :, None, :]
    cd = jnp.broadcast_to(jnp.exp(Cc * lg).T[:, :, None, None], (H, 2, SUBLANES, LANES))
    inv = ROPE_BASE ** (-jnp.arange(0, dh, 2, dtype=F32) / dh)
    ang = jnp.arange(L, dtype=F32)[:, None] * inv[None, :]
    cos, sin = jnp.cos(ang), jnp.sin(ang)

    assert nC % RET_SUB == 0
    nS = nC // RET_SUB
    rb = RET_SUB * Cc

    def kch(c):
        return jnp.where(c < nS, c, c - nS)

    def qch(c):
        return jnp.maximum(c - nS, 0)

    kv = lambda off: pl.BlockSpec((rb, dh), lambda b, h, c: (b * nS + kch(c), off * H + h))
    qo = lambda off: pl.BlockSpec((rb, dh), lambda b, h, c: (b * nS + qch(c), off * H + h))
    tab = pl.BlockSpec((L, dh // 2), lambda b, h, c: (0, 0), pipeline_mode=pl.Buffered(1))
    return pl.pallas_call(
        functools.partial(_ret_kernel, n_chunks=nC),
        out_shape=jax.ShapeDtypeStruct((B * L, H * dh), BF16),
        grid=(B, H, 2 * nS),
        in_specs=[qo(0), kv(1), kv(2), qo(3), tab, tab,
                  pl.BlockSpec((None, Cc, Cc), lambda b, h, c: (h, 0, 0)),
                  pl.BlockSpec((None, 2, Cc, 1), lambda b, h, c: (h, 0, 0, 0)),
                  pl.BlockSpec((None, 2, 1, Cc), lambda b, h, c: (h, 0, 0, 0)),
                  pl.BlockSpec((None, 2, SUBLANES, LANES), lambda b, h, c: (h, 0, 0, 0))],
        out_specs=qo(0),
        scratch_shapes=[pltpu.VMEM((2, nC, dh, dh), F32), pltpu.VMEM((dh, dh), F32)],
        compiler_params=_cp(("parallel", "parallel", "arbitrary")),
        name="retention",
    )(p, p, p, p, cos, sin, dm, qs, ks, cd)


def _na_bias_tables(rpb):
    Wc, WR, WC = GRID_W, NA_WIN_ROWS, NA_WIN_COLS
    hi = lax.Precision.HIGHEST
    rpb = rpb.astype(F32)
    c = jnp.arange(Wc)[:, None]
    kc = jnp.arange(Wc)[None, :]
    cs = jnp.clip(c - WC // 2, 0, Wc - WC)
    valid = (kc >= cs) & (kc < cs + WC)
    csel = (((kc - c + (WC - 1))[:, :, None] == jnp.arange(2 * WC - 1)) & valid[:, :, None]).astype(F32)
    dj = jnp.arange(WR)[None, :] - jnp.arange(WR)[:, None] + (WR - 1)
    rsel = (dj[:, :, None] == jnp.arange(2 * WR - 1)).astype(F32)
    t = jnp.einsum("hrs,cks->hrck", rpb, csel, precision=hi)
    t = jnp.einsum("djr,hrck->hdcjk", rsel, t, precision=hi)
    t = jnp.where(valid[None, None, :, None, :], t, NEG_INF)
    t = t.reshape(NA_HEADS // NA_HG, NA_HG, WR, Wc, WR * Wc).transpose(0, 2, 1, 3, 4)
    return t.reshape(NA_HEADS // NA_HG, WR, NA_HG * Wc, WR * Wc)


def _na_kernel(q_ref, k_ref, v_ref, b_ref, o_ref, *, rows):
    Wc, WR = GRID_W, NA_WIN_ROWS
    hw = NA_HG * NA_HEAD_DIM
    hq = NA_HG * Wc
    scale = NA_HEAD_DIM ** -0.5
    rb = pl.program_id(2)
    own = (lax.broadcasted_iota(jnp.int32, (hq, hw), 0) // Wc
           == lax.broadcasted_iota(jnp.int32, (hq, hw), 1) // NA_HEAD_DIM)

    for i in range(NA_RB):
        r = rb * NA_RB + i
        rs = jnp.clip(r - WR // 2, 0, rows - WR)
        q = q_ref[i * Wc:(i + 1) * Wc, :]
        qs = jnp.where(own, jnp.concatenate([q] * NA_HG, axis=0), jnp.zeros((), q.dtype))
        k0 = pl.multiple_of(rs * Wc, Wc)
        kw = k_ref[pl.ds(k0, WR * Wc), :]
        vw = v_ref[pl.ds(k0, WR * Wc), :]
        s = lax.dot_general(qs, kw, (((1,), (1,)), ((), ())), preferred_element_type=F32) * scale + b_ref[r - rs]
        m = jnp.max(s, axis=-1, keepdims=True)
        e = jnp.exp(s - m)
        l = jnp.sum(e, axis=-1, keepdims=True)
        o = jnp.dot(e.astype(BF16), vw, preferred_element_type=F32) / l
        o = jnp.where(own, o, 0.0)
        out = o[0:Wc]
        for h in range(1, NA_HG):
            out = out + o[h * Wc:(h + 1) * Wc]
        o_ref[i * Wc:(i + 1) * Wc, :] = out.astype(o_ref.dtype)


def neighborhood_mixer(qkv, col0, B, L, rpb):
    Wc = GRID_W
    rows = L // Wc
    assert rows >= NA_WIN_ROWS and rows % NA_RB == 0
    hw = NA_HG * NA_HEAD_DIM
    nhg = NA_HEADS // NA_HG
    assert col0 % hw == 0
    c0 = col0 // hw
    bias = _na_bias_tables(rpb)
    nrb = rows // NA_RB
    return pl.pallas_call(
        functools.partial(_na_kernel, rows=rows),
        out_shape=jax.ShapeDtypeStruct((B * L, NA_WIDTH), BF16),
        grid=(B, nhg, nrb),
        in_specs=[pl.BlockSpec((NA_RB * Wc, hw), lambda b, g, r: (b * nrb + r, c0 + g)),
                  pl.BlockSpec((L, hw), lambda b, g, r: (b, c0 + nhg + g)),
                  pl.BlockSpec((L, hw), lambda b, g, r: (b, c0 + 2 * nhg + g)),
                  pl.BlockSpec((None, NA_WIN_ROWS, NA_HG * Wc, NA_WIN_ROWS * Wc), lambda b, g, r: (g, 0, 0, 0))],
        out_specs=pl.BlockSpec((NA_RB * Wc, hw), lambda b, g, r: (b * nrb + r, g)),
        compiler_params=_cp(("parallel", "parallel", "arbitrary")),
        name="neighborhood_attention",
    )(qkv, qkv, qkv, bias)


def _trunk(x, mem, B, L, prm, wb):
    n_mem = mem.shape[0] // B
    depth = prm["norm_g"].shape[0]
    hw3 = 3 * HY_WIDTH
    for layer in range(depth):
        i = layer // 2
        g = prm["norm_g"][layer]
        wo = wb["mix_wo"]
        if layer % 2 == 0:
            ident = jnp.zeros((3, S5_WIDTH), F32).at[1].set(1.0)
            cw = jnp.concatenate([prm["hy_short_w"][i].astype(F32), ident], axis=1)
            cb = jnp.concatenate([prm["hy_short_b"][i].astype(F32), jnp.zeros((S5_WIDTH,), F32)])
            uc = norm_matmul_conv(x, g[0], wb["ev_w_in"], cw, cb, L, layer=i)
            z = hyena_mixer(uc, B, L, prm["hy_w1"][i], prm["hy_b1"][i], prm["hy_freq"][i], prm["hy_w2"][i],
                            prm["hy_b2"][i], prm["hy_w3"][i], prm["hy_skip"][i])
            ss = s5_mixer(uc, hw3, B, L, prm["s5_a_re"][i], prm["s5_a_im"][i], prm["s5_log_dt"][i],
                          prm["s5_b_re"][i], prm["s5_b_im"][i], prm["s5_c_re"][i], prm["s5_c_im"][i],
                          prm["s5_d"][i], prm["s5_w_glu"][i])
            ops = [(z, wo, prm["hy_out_g"][i], 0), (ss, wo, None, HY_WIDTH)]
        else:
            p = norm_matmul(x, g[0], wb["od_w_in"], BF16, layer=i, name="odd_in_proj")
            ret = retention_mixer(p, B, L, prm["ret_decay"][i])
            na = neighborhood_mixer(p, 4 * RET_WIDTH, B, L, prm["na_rpb"][i])
            ops = [(ret, wo, None, 0), (na, wo, None, RET_WIDTH)]
        x, xn = matmul_norm_residual(ops, g[1], x, g[2], layer=layer, name="mix_out_proj")
        kv = norm_matmul(mem, prm["mem_norm_g"][layer], wb["xa_wkv"], BF16, layer=layer, name="xattn_kv_proj")
        x = xattn_block(xn, x, wb["xa_wq"], kv, wb["xa_wo"], g[3], B, L, n_mem, layer=layer)
        x = ffn_block(x, g[4], wb["ffn_wg"], wb["ffn_wu"], wb["ffn_wd"], g[5], layer=layer)
    return x


def kernel(x_prompt, x_sample, mem_prompt, mem_sample, norm_g, mix_wo, ev_w_in, hy_short_w, hy_short_b, hy_w1, hy_b1, hy_freq, hy_w2, hy_b2, hy_w3, hy_skip, hy_out_g, s5_a_re, s5_a_im, s5_log_dt, s5_b_re, s5_b_im, s5_c_re, s5_c_im, s5_d, s5_w_glu, od_w_in, ret_decay, na_rpb, mem_norm_g, xa_wq, xa_wkv, xa_wo, ffn_wg, ffn_wu, ffn_wd):
    prm = dict(norm_g=norm_g, hy_short_w=hy_short_w, hy_short_b=hy_short_b, hy_w1=hy_w1, hy_b1=hy_b1,
               hy_freq=hy_freq, hy_w2=hy_w2, hy_b2=hy_b2, hy_w3=hy_w3, hy_skip=hy_skip, hy_out_g=hy_out_g,
               s5_a_re=s5_a_re, s5_a_im=s5_a_im, s5_log_dt=s5_log_dt, s5_b_re=s5_b_re, s5_b_im=s5_b_im,
               s5_c_re=s5_c_re, s5_c_im=s5_c_im, s5_d=s5_d, s5_w_glu=s5_w_glu, ret_decay=ret_decay,
               na_rpb=na_rpb, mem_norm_g=mem_norm_g)
    wb = {k: v.astype(BF16) for k, v in dict(mix_wo=mix_wo, ev_w_in=ev_w_in, od_w_in=od_w_in, xa_wq=xa_wq,
                                             xa_wkv=xa_wkv, xa_wo=xa_wo, ffn_wg=ffn_wg, ffn_wu=ffn_wu,
                                             ffn_wd=ffn_wd).items()}
    outs = []
    for x, mem in ((x_prompt, mem_prompt), (x_sample, mem_sample)):
        B, L, D = x.shape
        y = _trunk(x.reshape(B * L, D), mem.reshape(-1, D), B, L, prm, wb)
        outs.append(y.reshape(B, L, D))
    return tuple(outs)
```

```python
import functools
import math

import jax
import jax.numpy as jnp
from jax import lax
from jax.experimental import pallas as pl
from jax.experimental.pallas import tpu as pltpu

F32 = jnp.float32
BF16 = jnp.bfloat16

V7X_VMEM_BYTES = 64 * 1024 * 1024
VMEM_LIMIT = V7X_VMEM_BYTES - 8 * 1024 * 1024
LANES = 128
SUBLANES = 8

D_MODEL = 2048
GRID_W = 64
HY_WIDTH = 3 * D_MODEL // 4
S5_WIDTH = D_MODEL - HY_WIDTH
S5_GROUP = 16
S5_GROUPS = S5_WIDTH // S5_GROUP
S5_STATE = 64
HY_ORDER = 2
HY_BANDS = 16
HY_FILTER_HIDDEN = 64
HY_DECAY_TARGET = 1e-2
HY_SHORT_DECAY_PCT = 0.3
HY_LONG_DECAY_PCT = 1.5
RET_WIDTH = D_MODEL // 2
RET_HEADS = 4
RET_HEAD_DIM = RET_WIDTH // RET_HEADS
ROPE_BASE = 10000.0
NA_WIDTH = D_MODEL - RET_WIDTH
NA_HEADS = 16
NA_HEAD_DIM = NA_WIDTH // NA_HEADS
NA_WIN_ROWS = 8
NA_WIN_COLS = 16
XA_HEADS = 4
XA_HEAD_DIM = D_MODEL // XA_HEADS
RMS_EPS = 1e-6
GN_EPS = 1e-6

FFT_N2 = 128
FFT_RB = SUBLANES
S5_T = 32
RET_CC = 256
RET_SUB = 2
FFN_SPLIT = 2
NA_HG = 4
NA_RB = 8
NEG_INF = -1e30


def _cp(sem, vmem=VMEM_LIMIT):
    return pltpu.CompilerParams(dimension_semantics=sem, vmem_limit_bytes=vmem)


def _rms(x, g, eps=RMS_EPS):
    return x * lax.rsqrt(jnp.mean(x * x, axis=-1, keepdims=True) + eps) * g


def _norm_mm_kernel(x_ref, g_ref, w_ref, o_ref, xn_ref):
    @pl.when(pl.program_id(1) == 0)
    def _():
        tm = x_ref.shape[0]
        rc = min(tm, 256)
        for r in range(0, tm, rc):
            xn_ref[r:r + rc, :] = _rms(x_ref[r:r + rc, :], g_ref[...]).astype(BF16)

    o_ref[...] = jnp.dot(xn_ref[...], w_ref[...], preferred_element_type=F32).astype(o_ref.dtype)


def _stacked(w):
    return w if w.ndim == 3 else w[None]


def norm_matmul(x, g, w, out_dtype, col0=0, ncols=None, layer=0, tm=1024, tn=1024, name="norm_matmul"):
    M, K = x.shape
    w = _stacked(w)
    N = w.shape[2] - col0 if ncols is None else ncols
    tm, tn = min(tm, M), min(tn, N)
    assert M % tm == 0 and N % tn == 0 and col0 % tn == 0
    cb0 = col0 // tn
    return pl.pallas_call(
        _norm_mm_kernel,
        out_shape=jax.ShapeDtypeStruct((M, N), out_dtype),
        grid=(M // tm, N // tn),
        in_specs=[pl.BlockSpec((tm, K), lambda i, j: (i, 0)),
                  pl.BlockSpec((1, K), lambda i, j: (0, 0)),
                  pl.BlockSpec((None, K, tn), lambda i, j: (layer, 0, cb0 + j))],
        out_specs=pl.BlockSpec((tm, tn), lambda i, j: (i, j)),
        scratch_shapes=[pltpu.VMEM((tm, K), BF16)],
        compiler_params=_cp(("parallel", "arbitrary")),
        name=name,
    )(x, g.reshape(1, K), w)


def _mm_norm_res_kernel(*refs, n_ops, prenorm):
    g_ref, x_ref, gn_ref, o_ref, on_ref = refs[-5:]
    tm = x_ref.shape[0]
    rc = tm // 2 if tm % 32 == 0 else tm
    for r in range(0, tm, rc):
        pos = 0
        y = None
        for t in range(n_ops):
            a = refs[pos][r:r + rc, :]
            w_ref = refs[pos + 1]
            pos += 2
            if prenorm[t]:
                a = _rms(a.astype(F32), refs[pos][...])
                pos += 1
            d = jnp.dot(a.astype(BF16), w_ref[...], preferred_element_type=F32)
            y = d if y is None else y + d
        x1 = x_ref[r:r + rc, :] + _rms(y, g_ref[...])
        o_ref[r:r + rc, :] = x1
        on_ref[r:r + rc, :] = _rms(x1, gn_ref[...]).astype(on_ref.dtype)


def matmul_norm_residual(ops, g, x, g_next, layer=0, tm=512, name="matmul_norm_residual"):
    M, N = x.shape
    tm = min(tm, M)
    assert M % tm == 0
    args, specs, prenorm = [], [], []
    for a, w, pg, k0 in ops:
        kt = a.shape[1]
        assert k0 % kt == 0
        args += [a, _stacked(w)]
        specs += [pl.BlockSpec((tm, kt), lambda i: (i, 0)),
                  pl.BlockSpec((None, kt, N), lambda i, kb=k0 // kt: (layer, kb, 0))]
        prenorm.append(pg is not None)
        if pg is not None:
            args.append(pg.reshape(1, kt))
            specs.append(pl.BlockSpec((1, kt), lambda i: (0, 0)))
    row = pl.BlockSpec((1, N), lambda i: (0, 0))
    tile = pl.BlockSpec((tm, N), lambda i: (i, 0))
    args += [g.reshape(1, N), x, g_next.reshape(1, N)]
    specs += [row, tile, row]
    return pl.pallas_call(
        functools.partial(_mm_norm_res_kernel, n_ops=len(ops), prenorm=tuple(prenorm)),
        out_shape=(jax.ShapeDtypeStruct((M, N), F32), jax.ShapeDtypeStruct((M, N), BF16)),
        grid=(M // tm,),
        in_specs=specs,
        out_specs=(tile, tile),
        compiler_params=_cp(("parallel",)),
        name=name,
    )(*args)


def _xattn_block_kernel(xn_ref, x_ref, wq_ref, k_ref, v_ref, wo_ref, g_ref, o_ref, *, heads):
    dh = xn_ref.shape[1] // heads
    scale = dh ** -0.5
    q = jnp.dot(xn_ref[...], wq_ref[...], preferred_element_type=F32).astype(BF16)
    outs = []
    for h in range(heads):
        sl = slice(h * dh, (h + 1) * dh)
        s = lax.dot_general(q[:, sl], k_ref[:, sl], (((1,), (1,)), ((), ())), preferred_element_type=F32) * scale
        m = jnp.max(s, axis=-1, keepdims=True)
        p = jnp.exp(s - m)
        l = jnp.sum(p, axis=-1, keepdims=True)
        outs.append((jnp.dot(p.astype(BF16), v_ref[:, sl], preferred_element_type=F32) / l).astype(BF16))
    o = jnp.concatenate(outs, axis=-1)
    y = jnp.dot(o, wo_ref[...], preferred_element_type=F32)
    o_ref[...] = x_ref[...] + _rms(y, g_ref[...])


def xattn_block(xn, x, wq, kv, wo, g, B, L, n_mem, layer=0, tm=512):
    M, D = x.shape
    tm = min(tm, L)
    assert L % tm == 0
    bpl = L // tm
    wq, wo = _stacked(wq), _stacked(wo)
    wspec = pl.BlockSpec((None, D, D), lambda i: (layer, 0, 0), pipeline_mode=pl.Buffered(1))
    return pl.pallas_call(
        functools.partial(_xattn_block_kernel, heads=XA_HEADS),
        out_shape=jax.ShapeDtypeStruct((M, D), F32),
        grid=(M // tm,),
        in_specs=[pl.BlockSpec((tm, D), lambda i: (i, 0)),
                  pl.BlockSpec((tm, D), lambda i: (i, 0)),
                  wspec,
                  pl.BlockSpec((n_mem, D), lambda i: (i // bpl, 0)),
                  pl.BlockSpec((n_mem, D), lambda i: (i // bpl, 1)),
                  wspec,
                  pl.BlockSpec((1, D), lambda i: (0, 0))],
        out_specs=pl.BlockSpec((tm, D), lambda i: (i, 0)),
        compiler_params=_cp(("parallel",)),
        name="xattn_block",
    )(xn, x, wq, kv, kv, wo, g.reshape(1, D))


def _ffn_kernel(x_ref, gi_ref, wg_ref, wu_ref, wd_ref, go_ref, o_ref, xn_ref):
    j = pl.program_id(1)

    tm = x_ref.shape[0]
    rc = min(tm, 256)

    @pl.when(j == 0)
    def _():
        for r in range(0, tm, rc):
            xn_ref[r:r + rc, :] = _rms(x_ref[r:r + rc, :], gi_ref[...]).astype(BF16)
        o_ref[...] = jnp.zeros_like(o_ref)

    th = wg_ref.shape[1]
    hh = th // FFN_SPLIT
    for c in range(FFN_SPLIT):
        xn = xn_ref[...]
        a = jnp.dot(xn, wg_ref[:, c * hh:(c + 1) * hh], preferred_element_type=F32)
        u = jnp.dot(xn, wu_ref[:, c * hh:(c + 1) * hh], preferred_element_type=F32)
        h = (a * jax.nn.sigmoid(a) * u).astype(BF16)
        o_ref[...] += jnp.dot(h, wd_ref[c * hh:(c + 1) * hh, :], preferred_element_type=F32)

    @pl.when(j == pl.num_programs(1) - 1)
    def _():
        for r in range(0, tm, rc):
            o_ref[r:r + rc, :] = x_ref[r:r + rc, :] + _rms(o_ref[r:r + rc, :], go_ref[...])


def ffn_block(x, g_in, wg, wu, wd, g_out, layer=0, tm=1024, th=512):
    M, D = x.shape
    wg, wu, wd = _stacked(wg), _stacked(wu), _stacked(wd)
    Hd = wg.shape[2]
    tm, th = min(tm, M), min(th, Hd)
    assert M % tm == 0 and Hd % th == 0
    return pl.pallas_call(
        _ffn_kernel,
        out_shape=jax.ShapeDtypeStruct((M, D), F32),
        grid=(M // tm, Hd // th),
        in_specs=[pl.BlockSpec((tm, D), lambda i, j: (i, 0)),
                  pl.BlockSpec((1, D), lambda i, j: (0, 0)),
                  pl.BlockSpec((None, D, th), lambda i, j: (layer, 0, j)),
                  pl.BlockSpec((None, D, th), lambda i, j: (layer, 0, j)),
                  pl.BlockSpec((None, th, D), lambda i, j: (layer, j, 0)),
                  pl.BlockSpec((1, D), lambda i, j: (0, 0))],
        out_specs=pl.BlockSpec((tm, D), lambda i, j: (i, 0)),
        scratch_shapes=[pltpu.VMEM((tm, D), BF16)],
        compiler_params=_cp(("parallel", "arbitrary")),
        name="ffn_block",
    )(x, g_in.reshape(1, D), wg, wu, wd, g_out.reshape(1, D))


HALO = 16


def _norm_mm_conv_kernel(xp_ref, xc_ref, xn_ref, g_ref, w_ref, cw_ref, cb_ref, o_ref, xs_ref, *, blocks_per_seq):
    tm = xc_ref.shape[0]

    @pl.when(pl.program_id(1) == 0)
    def _():
        li = pl.program_id(0) % blocks_per_seq
        g = g_ref[...]
        keep_prev = jnp.where(li == 0, 0.0, 1.0)
        keep_next = jnp.where(li == blocks_per_seq - 1, 0.0, 1.0)
        xs_ref[0:HALO] = (_rms(xp_ref[...], g) * keep_prev).astype(BF16)
        rc = min(tm, 256)
        for r in range(0, tm, rc):
            xs_ref[HALO + r:HALO + r + rc] = _rms(xc_ref[r:r + rc, :], g).astype(BF16)
        xs_ref[HALO + tm:] = (_rms(xn_ref[...], g) * keep_next).astype(BF16)

    n = tm + 2 * HALO
    tn = o_ref.shape[1]
    hw = tn // 2 if tn % (2 * LANES) == 0 else tn
    for c in range(0, tn, hw):
        cs = slice(c, c + hw)
        y = jnp.dot(xs_ref[...], w_ref[:, cs], preferred_element_type=F32)
        up = pltpu.roll(y, 1, axis=0)[HALO:HALO + tm]
        dn = pltpu.roll(y, n - 1, axis=0)[HALO:HALO + tm]
        o_ref[:, cs] = (up * cw_ref[0:1, cs] + y[HALO:HALO + tm] * cw_ref[1:2, cs] + dn * cw_ref[2:3, cs]
                        + cb_ref[:, cs])


def norm_matmul_conv(x, g, w, cw, cb, L, layer=0, tm=1024, tn=1024):
    M, K = x.shape
    w = _stacked(w)
    N = w.shape[2]
    tm, tn = min(tm, L), min(tn, N)
    assert L % tm == 0 and N % tn == 0 and tm % HALO == 0
    hb = tm // HALO
    nhb = M // HALO
    return pl.pallas_call(
        functools.partial(_norm_mm_conv_kernel, blocks_per_seq=L // tm),
        out_shape=jax.ShapeDtypeStruct((M, N), F32),
        grid=(M // tm, N // tn),
        in_specs=[pl.BlockSpec((HALO, K), lambda i, j: (jnp.maximum(i * hb - 1, 0), 0)),
                  pl.BlockSpec((tm, K), lambda i, j: (i, 0)),
                  pl.BlockSpec((HALO, K), lambda i, j: (jnp.minimum((i + 1) * hb, nhb - 1), 0)),
                  pl.BlockSpec((1, K), lambda i, j: (0, 0)),
                  pl.BlockSpec((None, K, tn), lambda i, j: (layer, 0, j)),
                  pl.BlockSpec((3, tn), lambda i, j: (0, j)),
                  pl.BlockSpec((1, tn), lambda i, j: (0, j))],
        out_specs=pl.BlockSpec((tm, tn), lambda i, j: (i, j)),
        scratch_shapes=[pltpu.VMEM((tm + 2 * HALO, K), BF16)],
        compiler_params=_cp(("parallel", "arbitrary")),
        name="even_in_proj_conv",
    )(x, x, x, g.reshape(1, K), w, cw, cb.reshape(1, N))


def _filter_kernel(bands_ref, w1t_ref, w1c_ref, w1s_ref, b1_ref, f_ref, w2_ref, b2_ref, w3h_ref, w3l_ref, dl_ref,
                   sk_ref, o_ref, hid_ref, *, L, ncb):
    hi = lax.Precision.HIGHEST
    s = pl.program_id(1)
    tl = o_ref.shape[0]
    i0 = pl.program_id(0) * tl

    @pl.when(s == 0)
    def _():
        m_row = i0 + lax.broadcasted_iota(jnp.int32, (1, tl), 1)
        for d, pos in enumerate((m_row, L - m_row)):
            t = pos.astype(F32) * (1.0 / L)
            ang = (2.0 * math.pi) * bands_ref[...] * t
            pre = (w1t_ref[...] * t
                   + jnp.dot(w1c_ref[...], jnp.cos(ang), preferred_element_type=F32, precision=hi)
                   + jnp.dot(w1s_ref[...], jnp.sin(ang), preferred_element_type=F32, precision=hi)
                   + b1_ref[...])
            h = jnp.sin(f_ref[0] * pre)
            h = jnp.sin(f_ref[1] * (jnp.dot(w2_ref[...], h, preferred_element_type=F32, precision=hi) + b2_ref[...]))
            h = h.T
            h_hi = h.astype(BF16)
            hid_ref[d, 0] = h_hi
            hid_ref[d, 1] = (h - h_hi.astype(F32)).astype(BF16)

    d = (s // ncb) % 2
    m = i0 + lax.broadcasted_iota(jnp.int32, (tl, 1), 0)
    t = jnp.where(d == 0, m, L - m).astype(F32) * (1.0 / L)
    out = (jnp.dot(hid_ref[d, 0], w3h_ref[...], preferred_element_type=F32)
           + jnp.dot(hid_ref[d, 0], w3l_ref[...], preferred_element_type=F32)
           + jnp.dot(hid_ref[d, 1], w3h_ref[...], preferred_element_type=F32))
    out = out * jnp.exp(-t * dl_ref[...])
    o_ref[...] = out

    @pl.when(pl.program_id(0) == 0)
    def _():
        o_ref[0:1, :] = jnp.where(d == 0, out[0:1, :] + sk_ref[...], 0.0)


def hyena_filters(w1, b1, freq, w2, b2, w3, skip, L, tl=512, W=512):
    C = HY_WIDTH
    Hh = HY_FILTER_HIDDEN
    nb = HY_BANDS
    tl = min(tl, L)
    col = lambda v: v.astype(F32).reshape(-1, 1)
    bands = col(jnp.arange(1, nb + 1, dtype=F32))
    w1 = w1.astype(F32)
    w3_hi = w3.astype(BF16)
    w3_lo = (w3.astype(F32) - w3_hi.astype(F32)).astype(BF16)
    deltas = jnp.abs(jnp.linspace(math.log(HY_DECAY_TARGET) / HY_LONG_DECAY_PCT,
                                  math.log(HY_DECAY_TARGET) / HY_SHORT_DECAY_PCT, C, dtype=F32)).reshape(1, C)
    ncb = C // W
    small = lambda *shape: pl.BlockSpec(shape, lambda i, s: (0,) * len(shape))
    return pl.pallas_call(
        functools.partial(_filter_kernel, L=L, ncb=ncb),
        out_shape=jax.ShapeDtypeStruct((2 * HY_ORDER, L, C), F32),
        grid=(L // tl, 2 * HY_ORDER * ncb),
        in_specs=[small(nb, 1), small(Hh, 1), small(Hh, nb), small(Hh, nb), small(Hh, 1),
                  small(2, Hh, 1), small(Hh, Hh), small(Hh, 1),
                  pl.BlockSpec((Hh, W), lambda i, s: (0, s)),
                  pl.BlockSpec((Hh, W), lambda i, s: (0, s)),
                  pl.BlockSpec((1, W), lambda i, s: (0, s % ncb)),
                  pl.BlockSpec((None, 1, W), lambda i, s: (s // (2 * ncb), 0, s % ncb))],
        out_specs=pl.BlockSpec((None, tl, W), lambda i, s: (s // ncb, i, s % ncb)),
        scratch_shapes=[pltpu.VMEM((2, 2, tl, Hh), BF16)],
        compiler_params=_cp(("parallel", "arbitrary")),
        name="hyena_filters",
    )(bands, col(w1[0]), w1[1:1 + nb].T, w1[1 + nb:].T, col(b1), freq.astype(F32)[:, :, None], w2.astype(F32).T,
      col(b2), w3_hi, w3_lo, deltas, skip.astype(F32).reshape(HY_ORDER, 1, C))


def _fft_tables(L):
    N2 = FFT_N2
    N = 2 * L
    N1 = N // N2
    N1h = N1 // 2
    n2 = jnp.arange(N2, dtype=jnp.int32)[:, None, None]
    k1 = jnp.arange(N1, dtype=jnp.int32)[None, :, None]
    n1 = jnp.arange(N1h, dtype=jnp.int32)[None, None, :]
    ph = ((n1 * N2 + n2) * k1) % N
    ang = ph.astype(F32) * (2.0 * math.pi / N)
    gr, gi = jnp.cos(ang), -jnp.sin(ang)
    g_fwd = jnp.concatenate([jnp.concatenate([gr, -gi], 2), jnp.concatenate([gi, gr], 2)], 1).astype(BF16)
    sgn = jnp.where(k1 % 2 == 0, 1.0, -1.0).astype(F32)
    g_flt = jnp.concatenate([jnp.concatenate([gr, sgn * gr], 2), jnp.concatenate([gi, sgn * gi], 2)], 1).astype(BF16)
    er = jnp.swapaxes(gr, 1, 2) / N
    ei = -jnp.swapaxes(gi, 1, 2) / N
    g_inv = jnp.concatenate([jnp.concatenate([er, -ei], 2), jnp.concatenate([ei, er], 2)], 1).astype(BF16)
    a2 = (jnp.arange(N2, dtype=jnp.int32)[:, None] * jnp.arange(N2, dtype=jnp.int32)[None, :]) % N2
    ang2 = a2.astype(F32) * (2.0 * math.pi / N2)
    fr, fi = jnp.cos(ang2), -jnp.sin(ang2)
    f_mid = jnp.concatenate([jnp.concatenate([fr, -fi], 1), jnp.concatenate([fi, fr], 1)], 0).astype(BF16)
    f_mid_inv = jnp.concatenate([jnp.concatenate([fr, fi], 1), jnp.concatenate([-fi, fr], 1)], 0).astype(BF16)
    return dict(g_fwd=g_fwd, g_flt=g_flt, g_inv=g_inv, f_mid=f_mid, f_mid_inv=f_mid_inv, N1=N1)


def _pack_complex(re, im):
    r = lax.bitcast_convert_type(re.astype(BF16).astype(F32), jnp.uint32)
    i = lax.bitcast_convert_type(im.astype(BF16).astype(F32), jnp.uint32)
    return r | (i >> 16)


def _unpack_complex(w):
    re = lax.bitcast_convert_type(w & jnp.uint32(0xFFFF0000), F32)
    im = lax.bitcast_convert_type(w << 16, F32)
    return re, im


def _to_slabs(src_ref, slab_ref, lead=()):
    ns, rows, _ = slab_ref.shape
    for s in range(ns):
        slab_ref[s] = src_ref[lead + (slice(None), slice(None), slice(s * LANES, (s + 1) * LANES))].reshape(
            rows, LANES)


def _slab_rows(slab_ref, r, n):
    return jnp.concatenate([slab_ref.at[s][pl.ds(r, n, stride=FFT_RB), :] for s in range(slab_ref.shape[0])],
                           axis=1)


def _set_slab_rows(slab_ref, r, val):
    n = val.shape[0]
    for s in range(slab_ref.shape[0]):
        slab_ref.at[s][pl.ds(r, n, stride=FFT_RB), :] = val[:, s * LANES:(s + 1) * LANES]


def _from_slabs(slab_ref, dst_ref, lead=()):
    ns, rows, _ = slab_ref.shape
    for s in range(ns):
        dst_ref[lead + (slice(None), slice(None), slice(s * LANES, (s + 1) * LANES))] = (
            slab_ref[s].reshape(rows // FFT_RB, FFT_RB, LANES))


def _fft_a_kernel(x_ref, g_ref, z_ref, xa_ref, xb_ref, zs_ref):
    n1 = z_ref.shape[0]
    _to_slabs(x_ref, xa_ref, (0,))
    _to_slabs(x_ref, xb_ref, (1,))
    for r in range(FFT_RB):
        xs = jnp.concatenate([_slab_rows(xa_ref, r, n1 // 2), _slab_rows(xb_ref, r, n1 // 2)], axis=0).astype(BF16)
        a = jnp.dot(g_ref[r], xs, preferred_element_type=F32)
        _set_slab_rows(zs_ref, r, _pack_complex(a[:n1], a[n1:]))
    _from_slabs(zs_ref, z_ref)


def fft_stage_a(x4, col0, C, g, W=512):
    B, N1h, N2, _ = x4.shape
    N1 = 2 * N1h
    W = min(W, C)
    assert B % 2 == 0 and C % W == 0 and col0 % W == 0 and N2 % FFT_RB == 0
    cb0 = col0 // W
    return pl.pallas_call(
        _fft_a_kernel,
        out_shape=jax.ShapeDtypeStruct((B // 2, N1, N2, C), jnp.uint32),
        grid=(B // 2, N2 // FFT_RB, C // W),
        in_specs=[pl.BlockSpec((2, N1h, FFT_RB, W), lambda p, j, c: (p, 0, j, cb0 + c)),
                  pl.BlockSpec((FFT_RB, 2 * N1, N1), lambda p, j, c: (j, 0, 0))],
        out_specs=pl.BlockSpec((None, N1, FFT_RB, W), lambda p, j, c: (p, 0, j, c)),
        scratch_shapes=[pltpu.VMEM((W // LANES, N1h * FFT_RB, LANES), F32),
                        pltpu.VMEM((W // LANES, N1h * FFT_RB, LANES), F32),
                        pltpu.VMEM((W // LANES, N1 * FFT_RB, LANES), jnp.uint32)],
        compiler_params=_cp(("parallel", "parallel", "parallel")),
        name="hyena_fft_a",
    )(x4, g)


def _fft_mid_kernel(*refs, kb, with_filter):
    if with_filter:
        z_ref, h_ref, f_ref, fi_ref, y_ref = refs
    else:
        z_ref, f_ref, y_ref = refs
    n2 = z_ref.shape[1]
    for k in range(kb):
        zr, zi = _unpack_complex(z_ref[k])
        zs = jnp.concatenate([zr, zi], axis=0).astype(BF16)
        x = jnp.dot(f_ref[...], zs, preferred_element_type=F32)
        if with_filter:
            xr, xi = x[:n2], x[n2:]
            hr, hi = _unpack_complex(h_ref[k])
            ys = jnp.concatenate([xr * hr - xi * hi, xr * hi + xi * hr], axis=0).astype(BF16)
            x = jnp.dot(fi_ref[...], ys, preferred_element_type=F32)
        y_ref[k] = _pack_complex(x[:n2], x[n2:])


def fft_stage_mid(z, tabs, h=None, order=0, kb=8, W=512):
    P, N1, N2, C = z.shape
    kb, W = min(kb, N1), min(W, C)
    assert N1 % kb == 0 and C % W == 0
    blk = pl.BlockSpec((None, kb, N2, W), lambda k, c, p: (p, k, 0, c))
    mat = pl.BlockSpec((2 * N2, 2 * N2), lambda k, c, p: (0, 0))
    if h is None:
        args, specs = (z, tabs["f_mid"]), [blk, mat]
    else:
        hblk = pl.BlockSpec((None, kb, N2, W), lambda k, c, p: (order, k, 0, c))
        args, specs = (z, h, tabs["f_mid"], tabs["f_mid_inv"]), [blk, hblk, mat, mat]
    return pl.pallas_call(
        functools.partial(_fft_mid_kernel, kb=kb, with_filter=h is not None),
        out_shape=jax.ShapeDtypeStruct(z.shape, jnp.uint32),
        grid=(N1 // kb, C // W, P),
        in_specs=specs,
        out_specs=blk,
        compiler_params=_cp(("parallel", "parallel", "parallel")),
        name="hyena_fft_mid",
    )(*args)


def _fft_c_kernel(y_ref, g_ref, x_ref, o_ref, ys_ref, xa_ref, xb_ref):
    n1h = x_ref.shape[1]
    _to_slabs(y_ref, ys_ref)
    _to_slabs(x_ref, xa_ref, (0,))
    _to_slabs(x_ref, xb_ref, (1,))
    for r in range(FFT_RB):
        yr, yi = _unpack_complex(_slab_rows(ys_ref, r, 2 * n1h))
        ys = jnp.concatenate([yr, yi], axis=0).astype(BF16)
        c = jnp.dot(g_ref[r], ys, preferred_element_type=F32)
        _set_slab_rows(xa_ref, r, _slab_rows(xa_ref, r, n1h) * c[:n1h])
        _set_slab_rows(xb_ref, r, _slab_rows(xb_ref, r, n1h) * c[n1h:])
    _from_slabs(xa_ref, o_ref, (0,))
    _from_slabs(xb_ref, o_ref, (1,))


def fft_stage_c(y, g, x4, xcol0, W=512):
    P, N1, N2, C = y.shape
    N1h = N1 // 2
    W = min(W, C)
    assert C % W == 0 and xcol0 % W == 0
    xb = xcol0 // W
    slab = lambda n, dt: pltpu.VMEM((W // LANES, n * FFT_RB, LANES), dt)
    return pl.pallas_call(
        _fft_c_kernel,
        out_shape=jax.ShapeDtypeStruct((2 * P, N1h, N2, C), F32),
        grid=(P, N2 // FFT_RB, C // W),
        in_specs=[pl.BlockSpec((None, N1, FFT_RB, W), lambda p, j, c: (p, 0, j, c)),
                  pl.BlockSpec((FFT_RB, N1, 2 * N1), lambda p, j, c: (j, 0, 0)),
                  pl.BlockSpec((2, N1h, FFT_RB, W), lambda p, j, c: (p, 0, j, xb + c))],
        out_specs=pl.BlockSpec((2, N1h, FFT_RB, W), lambda p, j, c: (p, 0, j, c)),
        scratch_shapes=[slab(N1, jnp.uint32), slab(N1h, F32), slab(N1h, F32)],
        compiler_params=_cp(("parallel", "parallel", "parallel")),
        name="hyena_fft_c",
    )(y, g, x4)


def _fft_ca_kernel(y_ref, gi_ref, x_ref, gf_ref, z_ref, ys_ref, xa_ref, xb_ref, zs_ref):
    n1h = x_ref.shape[1]
    n1 = 2 * n1h
    _to_slabs(y_ref, ys_ref)
    _to_slabs(x_ref, xa_ref, (0,))
    _to_slabs(x_ref, xb_ref, (1,))
    for r in range(FFT_RB):
        yr, yi = _unpack_complex(_slab_rows(ys_ref, r, n1))
        c = jnp.dot(gi_ref[r], jnp.concatenate([yr, yi], axis=0).astype(BF16), preferred_element_type=F32)
        gated = jnp.concatenate([_slab_rows(xa_ref, r, n1h) * c[:n1h], _slab_rows(xb_ref, r, n1h) * c[n1h:]], axis=0)
        a = jnp.dot(gf_ref[r], gated.astype(BF16), preferred_element_type=F32)
        _set_slab_rows(zs_ref, r, _pack_complex(a[:n1], a[n1:]))
    _from_slabs(zs_ref, z_ref)


def fft_stage_ca(y, g_inv, x4, xcol0, g_fwd, W=512):
    P, N1, N2, C = y.shape
    N1h = N1 // 2
    W = min(W, C)
    assert C % W == 0 and xcol0 % W == 0
    xb = xcol0 // W
    slab = lambda n, dt: pltpu.VMEM((W // LANES, n * FFT_RB, LANES), dt)
    zblk = pl.BlockSpec((None, N1, FFT_RB, W), lambda p, j, c: (p, 0, j, c))
    return pl.pallas_call(
        _fft_ca_kernel,
        out_shape=jax.ShapeDtypeStruct((P, N1, N2, C), jnp.uint32),
        grid=(P, N2 // FFT_RB, C // W),
        in_specs=[zblk,
                  pl.BlockSpec((FFT_RB, N1, 2 * N1), lambda p, j, c: (j, 0, 0)),
                  pl.BlockSpec((2, N1h, FFT_RB, W), lambda p, j, c: (p, 0, j, xb + c)),
                  pl.BlockSpec((FFT_RB, 2 * N1, N1), lambda p, j, c: (j, 0, 0))],
        out_specs=zblk,
        scratch_shapes=[slab(N1, jnp.uint32), slab(N1h, F32), slab(N1h, F32), slab(N1, jnp.uint32)],
        compiler_params=_cp(("parallel", "parallel", "parallel")),
        name="hyena_fft_ca",
    )(y, g_inv, x4, g_fwd)


def hyena_mixer(uc, B, L, w1, b1, freq, w2, b2, w3, skip):
    C = HY_WIDTH
    N2 = FFT_N2
    N1h = L // N2
    tabs = _fft_tables(L)
    uc4 = uc.reshape(B, N1h, N2, uc.shape[1])
    filt = hyena_filters(w1, b1, freq, w2, b2, w3, skip, L)
    spec = fft_stage_mid(fft_stage_a(filt.reshape(2 * HY_ORDER, N1h, N2, C), 0, C, tabs["g_flt"]), tabs)
    y = fft_stage_mid(fft_stage_a(uc4, 0, C, tabs["g_fwd"]), tabs, h=spec, order=0)
    zz = fft_stage_ca(y, tabs["g_inv"], uc4, C, tabs["g_fwd"])
    y = fft_stage_mid(zz, tabs, h=spec, order=1)
    z = fft_stage_c(y, tabs["g_inv"], uc4, 2 * C)
    return z.reshape(B * L, C)


def _s5_tables(a_re, a_im, log_dt, b_re, b_im, c_re, c_im, n_chunks):
    T = S5_T
    G, P, I = S5_GROUPS, S5_STATE, S5_GROUP
    f32 = lambda a: a.astype(F32)
    a_re, a_im, b_re, b_im, c_re, c_im = map(f32, (a_re, a_im, b_re, b_im, c_re, c_im))
    step = jnp.exp(f32(log_dt))[..., None]
    lr, li = a_re * step, a_im * step
    mag = jnp.exp(lr)
    br_, bi_ = mag * jnp.cos(li), mag * jnp.sin(li)
    den = a_re * a_re + a_im * a_im
    qr = ((br_ - 1.0) * a_re + bi_ * a_im) / den
    qi = (bi_ * a_re - (br_ - 1.0) * a_im) / den
    bbr = qr[..., None] * b_re - qi[..., None] * b_im
    bbi = qr[..., None] * b_im + qi[..., None] * b_re

    def lam_pow(k):
        k = k.astype(F32)
        m = jnp.exp(lr[..., None] * k)
        return m * jnp.cos(li[..., None] * k), m * jnp.sin(li[..., None] * k)

    lags = jnp.arange(T + 1)
    pr, pi_ = lam_pow(lags)
    cpr = c_re[..., None] * pr[:, :, None] - c_im[..., None] * pi_[:, :, None]
    cpi = c_re[..., None] * pi_[:, :, None] + c_im[..., None] * pr[:, :, None]
    kern = jnp.einsum("dgjpk,dgpi->dgkji", cpr, bbr) - jnp.einsum("dgjpk,dgpi->dgkji", cpi, bbi)
    tt = jnp.arange(T)
    lag = tt[:, None] - tt[None, :]
    sel_f = (lag[:, :, None] == lags).astype(F32)
    sel_b = (-lag[:, :, None] == lags).astype(F32)
    toep = (jnp.einsum("tuk,gkji->gtjui", sel_f, kern[0], precision=lax.Precision.HIGHEST)
            + jnp.einsum("tuk,gkji->gtjui", sel_b, kern[1], precision=lax.Precision.HIGHEST))
    toep = toep.reshape(G, T * I, T * I)
    ef = T - 1 - tt
    eb = tt

    def bst(d, e):
        wr, wi = pr[d][..., e], pi_[d][..., e]
        re = wr[..., None] * bbr[d][:, :, None] - wi[..., None] * bbi[d][:, :, None]
        im = wr[..., None] * bbi[d][:, :, None] + wi[..., None] * bbr[d][:, :, None]
        return jnp.concatenate([re, im], axis=1).reshape(G, 2 * P, T * I)

    bst_all = jnp.stack([bst(0, ef), bst(1, eb)], axis=1)

    def cst(d, e):
        xr, xi = cpr[d][..., e], cpi[d][..., e]
        m = jnp.concatenate([xr, -xi], axis=2)
        return m.transpose(0, 3, 1, 2).reshape(G, T * I, 2 * P)

    cst_all = jnp.stack([cst(0, tt + 1), cst(1, T - tt)], axis=1)
    nsteps = max(1, int(math.log2(n_chunks)))
    e2 = T * (2 ** jnp.arange(nsteps))
    ar, ai = lam_pow(e2)
    ap = jnp.stack([ar, ai], axis=-1).transpose(1, 0, 3, 4, 2)
    ap = jnp.broadcast_to(ap[..., None], ap.shape + (LANES,))
    return toep.astype(BF16), bst_all.astype(BF16), cst_all.astype(BF16), ap, nsteps


def _s5_kernel(u_ref, toep_ref, bst_ref, cst_ref, ap_ref, y_ref, *, n_chunks, nsteps):
    T, I, cols = u_ref.shape
    P = S5_STATE
    u = u_ref[...].reshape(T * I, cols).astype(BF16)
    y = jnp.dot(toep_ref[...], u, preferred_element_type=F32)
    cidx = lax.broadcasted_iota(jnp.int32, (P, cols), 1) % n_chunks

    def shifted(x, sh, d):
        if d == 0:
            return jnp.where(cidx >= sh, pltpu.roll(x, sh, axis=1), 0.0)
        return jnp.where(cidx < n_chunks - sh, pltpu.roll(x, cols - sh, axis=1), 0.0)

    for d in range(2):
        v = jnp.dot(bst_ref[d], u, preferred_element_type=F32)
        sr, si = v[:P], v[P:]
        for j in range(nsteps):
            if (1 << j) >= n_chunks:
                break
            ar, ai = ap_ref[d, j, 0][:, :1], ap_ref[d, j, 1][:, :1]
            rr, ri = shifted(sr, 1 << j, d), shifted(si, 1 << j, d)
            sr, si = sr + ar * rr - ai * ri, si + ar * ri + ai * rr
        s_in = jnp.concatenate([shifted(sr, 1, d), shifted(si, 1, d)], axis=0).astype(BF16)
        y = y + jnp.dot(cst_ref[d], s_in, preferred_element_type=F32)
    y_ref[...] = y.reshape(T, I, cols)


def _s5_out_kernel(y_ref, u_ref, d_ref, w_ref, o_ref):
    y = y_ref[...] + d_ref[...] * u_ref[...]
    y = jax.nn.gelu(y, approximate=True)
    z = jnp.dot(y.astype(BF16), w_ref[...], preferred_element_type=F32)
    o_ref[...] = (y * jax.nn.sigmoid(z)).astype(o_ref.dtype)


def s5_mixer(p, col0, B, L, a_re, a_im, log_dt, b_re, b_im, c_re, c_im, d, w_glu, tm=1024):
    T, Wd, I, G = S5_T, S5_WIDTH, S5_GROUP, S5_GROUPS
    nC = L // T
    cols = B * nC
    assert L % T == 0 and nC & (nC - 1) == 0 and cols % LANES == 0
    toep, bst, cst, ap, nsteps = _s5_tables(a_re, a_im, log_dt, b_re, b_im, c_re, c_im, nC)
    u = p[:, col0:col0 + Wd]
    ut = u.reshape(B, nC, T, Wd).transpose(2, 3, 0, 1).reshape(T, Wd, cols)
    blk = pl.BlockSpec((T, I, cols), lambda g: (0, g, 0))
    yt = pl.pallas_call(
        functools.partial(_s5_kernel, n_chunks=nC, nsteps=nsteps),
        out_shape=jax.ShapeDtypeStruct((T, Wd, cols), F32),
        grid=(G,),
        in_specs=[blk,
                  pl.BlockSpec((None, T * I, T * I), lambda g: (g, 0, 0)),
                  pl.BlockSpec((None, 2, 2 * S5_STATE, T * I), lambda g: (g, 0, 0, 0)),
                  pl.BlockSpec((None, 2, T * I, 2 * S5_STATE), lambda g: (g, 0, 0, 0)),
                  pl.BlockSpec((None, 2, nsteps, 2, S5_STATE, LANES), lambda g: (g, 0, 0, 0, 0, 0))],
        out_specs=blk,
        compiler_params=_cp(("parallel",)),
        name="s5_scan",
    )(ut, toep, bst, cst, ap)
    y = yt.reshape(T, Wd, B, nC).transpose(2, 3, 0, 1).reshape(B * L, Wd)
    M = B * L
    tm = min(tm, M)
    cb = col0 // Wd
    assert col0 % Wd == 0 and M % tm == 0
    return pl.pallas_call(
        _s5_out_kernel,
        out_shape=jax.ShapeDtypeStruct((M, Wd), BF16),
        grid=(M // tm,),
        in_specs=[pl.BlockSpec((tm, Wd), lambda i: (i, 0)),
                  pl.BlockSpec((tm, Wd), lambda i: (i, cb)),
                  pl.BlockSpec((1, Wd), lambda i: (0, 0)),
                  pl.BlockSpec((Wd, Wd), lambda i: (0, 0))],
        out_specs=pl.BlockSpec((tm, Wd), lambda i: (i, 0)),
        compiler_params=_cp(("parallel",)),
        name="s5_glu",
    )(y, p, d.astype(F32).reshape(1, Wd), w_glu.astype(BF16))


def _ret_kernel(q_ref, k_ref, v_ref, g_ref, cos_ref, sin_ref, dm_ref, qs_ref, ks_ref, cd_ref, o_ref, st_ref,
                carry_ref, *, n_chunks):
    c = pl.program_id(2)
    nC = n_chunks
    nS = nC // RET_SUB
    Cc = dm_ref.shape[0]
    dh = k_ref.shape[1]
    h2 = dh // 2

    def rows(ref, u):
        return ref[u * Cc:(u + 1) * Cc, :].astype(F32)

    blk = jnp.where(c < nS, c, c - nS)

    def rot(x, u):
        r0 = pl.multiple_of((blk * RET_SUB + u) * Cc, Cc)
        cos, sin = cos_ref[pl.ds(r0, Cc), :], sin_ref[pl.ds(r0, Cc), :]
        x1, x2 = x[:, :h2], x[:, h2:]
        return jnp.concatenate([x1 * cos - x2 * sin, x1 * sin + x2 * cos], axis=-1)

    @pl.when(c < nS)
    def _():
        for u in range(RET_SUB):
            kt = (rot(rows(k_ref, u), u) * (dh ** -0.5)).T
            v = v_ref[u * Cc:(u + 1) * Cc, :].astype(BF16)
            for d in range(2):
                st_ref[d, c * RET_SUB + u] = jnp.dot((kt * ks_ref[d]).astype(BF16), v, preferred_element_type=F32)

    @pl.when(c == nS)
    def _():
        for d in range(2):
            cd = cd_ref[d, 0:1, 0:1]
            carry_ref[...] = jnp.zeros_like(carry_ref)

            def body(i, carry, d=d, cd=cd):
                idx = i if d == 0 else nC - 1 - i
                t = st_ref[d, idx]
                st_ref[d, idx] = carry_ref[...]
                carry_ref[...] = carry_ref[...] * cd + t
                return carry

            lax.fori_loop(0, nC, body, 0)

    @pl.when(c >= nS)
    def _():
        for u in range(RET_SUB):
            ch = (c - nS) * RET_SUB + u
            qr = rot(rows(q_ref, u), u)
            kr = rot(rows(k_ref, u), u) * (dh ** -0.5)
            v = v_ref[u * Cc:(u + 1) * Cc, :].astype(BF16)
            s = lax.dot_general(qr.astype(BF16), kr.astype(BF16), (((1,), (1,)), ((), ())),
                                preferred_element_type=F32) * dm_ref[...]
            o = jnp.dot(s.astype(BF16), v, preferred_element_type=F32)
            for d in range(2):
                o = o + jnp.dot((qr * qs_ref[d]).astype(BF16), st_ref[d, ch].astype(BF16),
                                preferred_element_type=F32)
            mu = jnp.mean(o, axis=-1, keepdims=True)
            oc = o - mu
            var = jnp.mean(oc * oc, axis=-1, keepdims=True)
            g = rows(g_ref, u)
            o_ref[u * Cc:(u + 1) * Cc, :] = (oc * lax.rsqrt(var + GN_EPS)
                                             * (g * jax.nn.sigmoid(g))).astype(o_ref.dtype)


def retention_mixer(p, B, L, ret_decay):
    H, dh = RET_HEADS, RET_HEAD_DIM
    Cc = min(RET_CC, L)
    nC = L // Cc
    assert L % Cc == 0
    lg = -jnp.exp(ret_decay.astype(F32))
    pos = jnp.arange(Cc, dtype=F32)
    rel = pos[:, None] - pos[None, :]
    lf, lb = lg[0][:, None, None], lg[1][:, None, None]
    dm = jnp.where(rel >= 0, jnp.exp(jnp.maximum(rel, 0.0) * lf), jnp.exp(jnp.maximum(-rel, 0.0) * lb))
    qs = jnp.stack([jnp.exp((pos + 1.0)[None] * lg[0][:, None]),
                    jnp.exp((Cc - pos)[None] * lg[1][:, None])], axis=1)[..., None]
    ks = jnp.stack([jnp.exp((Cc - 1.0 - pos)[None] * lg[0][:, None]),
                    jnp.exp(pos[None] * lg[1][:, None])], axis=1)[:, :, None, :]
    cd = jnp.broadcast_to(jnp.exp(Cc * lg).T[:, :, None, None], (H, 2, SUBLANES, LANES))
    inv = ROPE_BASE ** (-jnp.arange(0, dh, 2, dtype=F32) / dh)
    ang = jnp.arange(L, dtype=F32)[:, None] * inv[None, :]
    cos, sin = jnp.cos(ang), jnp.sin(ang)

    assert nC % RET_SUB == 0
    nS = nC // RET_SUB
    rb = RET_SUB * Cc

    def kch(c):
        return jnp.where(c < nS, c, c - nS)

    def qch(c):
        return jnp.maximum(c - nS, 0)

    kv = lambda off: pl.BlockSpec((rb, dh), lambda b, h, c: (b * nS + kch(c), off * H + h))
    qo = lambda off: pl.BlockSpec((rb, dh), lambda b, h, c: (b * nS + qch(c), off * H + h))
    tab = pl.BlockSpec((L, dh // 2), lambda b, h, c: (0, 0), pipeline_mode=pl.Buffered(1))
    return pl.pallas_call(
        functools.partial(_ret_kernel, n_chunks=nC),
        out_shape=jax.ShapeDtypeStruct((B * L, H * dh), BF16),
        grid=(B, H, 2 * nS),
        in_specs=[qo(0), kv(1), kv(2), qo(3), tab, tab,
                  pl.BlockSpec((None, Cc, Cc), lambda b, h, c: (h, 0, 0)),
                  pl.BlockSpec((None, 2, Cc, 1), lambda b, h, c: (h, 0, 0, 0)),
                  pl.BlockSpec((None, 2, 1, Cc), lambda b, h, c: (h, 0, 0, 0)),
                  pl.BlockSpec((None, 2, SUBLANES, LANES), lambda b, h, c: (h, 0, 0, 0))],
        out_specs=qo(0),
        scratch_shapes=[pltpu.VMEM((2, nC, dh, dh), F32), pltpu.VMEM((dh, dh), F32)],
        compiler_params=_cp(("parallel", "parallel", "arbitrary")),
        name="retention",
    )(p, p, p, p, cos, sin, dm, qs, ks, cd)


def _na_bias_tables(rpb):
    Wc, WR, WC = GRID_W, NA_WIN_ROWS, NA_WIN_COLS
    hi = lax.Precision.HIGHEST
    rpb = rpb.astype(F32)
    c = jnp.arange(Wc)[:, None]
    kc = jnp.arange(Wc)[None, :]
    cs = jnp.clip(c - WC // 2, 0, Wc - WC)
    valid = (kc >= cs) & (kc < cs + WC)
    csel = (((kc - c + (WC - 1))[:, :, None] == jnp.arange(2 * WC - 1)) & valid[:, :, None]).astype(F32)
    dj = jnp.arange(WR)[None, :] - jnp.arange(WR)[:, None] + (WR - 1)
    rsel = (dj[:, :, None] == jnp.arange(2 * WR - 1)).astype(F32)
    t = jnp.einsum("hrs,cks->hrck", rpb, csel, precision=hi)
    t = jnp.einsum("djr,hrck->hdcjk", rsel, t, precision=hi)
    t = jnp.where(valid[None, None, :, None, :], t, NEG_INF)
    t = t.reshape(NA_HEADS // NA_HG, NA_HG, WR, Wc, WR * Wc).transpose(0, 2, 1, 3, 4)
    return t.reshape(NA_HEADS // NA_HG, WR, NA_HG * Wc, WR * Wc)


def _na_kernel(q_ref, k_ref, v_ref, b_ref, o_ref, *, rows):
    Wc, WR = GRID_W, NA_WIN_ROWS
    hw = NA_HG * NA_HEAD_DIM
    hq = NA_HG * Wc
    scale = NA_HEAD_DIM ** -0.5
    assert math.frexp(scale)[0] == 0.5
    rb = pl.program_id(2)
    own = (lax.broadcasted_iota(jnp.int32, (hq, hw), 0) // Wc
           == lax.broadcasted_iota(jnp.int32, (hq, hw), 1) // NA_HEAD_DIM)

    for i in range(NA_RB):
        r = rb * NA_RB + i
        rs = jnp.clip(r - WR // 2, 0, rows - WR)
        q = q_ref[i * Wc:(i + 1) * Wc, :] * jnp.asarray(scale, q_ref.dtype)
        qs = jnp.where(own, jnp.concatenate([q] * NA_HG, axis=0), jnp.zeros((), q.dtype))
        k0 = pl.multiple_of(rs * Wc, Wc)
        kw = k_ref[pl.ds(k0, WR * Wc), :]
        vw = v_ref[pl.ds(k0, WR * Wc), :]
        s = lax.dot_general(qs, kw, (((1,), (1,)), ((), ())), preferred_element_type=F32) + b_ref[r - rs]
        m = jnp.max(s, axis=-1, keepdims=True)
        e = jnp.exp(s - m)
        l = jnp.sum(e, axis=-1, keepdims=True)
        o = jnp.dot(e.astype(BF16), vw, preferred_element_type=F32) / l
        o = jnp.where(own, o, 0.0)
        out = o[0:Wc]
        for h in range(1, NA_HG):
            out = out + o[h * Wc:(h + 1) * Wc]
        o_ref[i * Wc:(i + 1) * Wc, :] = out.astype(o_ref.dtype)


def neighborhood_mixer(qkv, col0, B, L, rpb):
    Wc = GRID_W
    rows = L // Wc
    assert rows >= NA_WIN_ROWS and rows % NA_RB == 0
    hw = NA_HG * NA_HEAD_DIM
    nhg = NA_HEADS // NA_HG
    assert col0 % hw == 0
    c0 = col0 // hw
    bias = _na_bias_tables(rpb)
    nrb = rows // NA_RB
    return pl.pallas_call(
        functools.partial(_na_kernel, rows=rows),
        out_shape=jax.ShapeDtypeStruct((B * L, NA_WIDTH), BF16),
        grid=(B, nhg, nrb),
        in_specs=[pl.BlockSpec((NA_RB * Wc, hw), lambda b, g, r: (b * nrb + r, c0 + g)),
                  pl.BlockSpec((L, hw), lambda b, g, r: (b, c0 + nhg + g)),
                  pl.BlockSpec((L, hw), lambda b, g, r: (b, c0 + 2 * nhg + g)),
                  pl.BlockSpec((None, NA_WIN_ROWS, NA_HG * Wc, NA_WIN_ROWS * Wc), lambda b, g, r: (g, 0, 0, 0))],
        out_specs=pl.BlockSpec((NA_RB * Wc, hw), lambda b, g, r: (b * nrb + r, g)),
        compiler_params=_cp(("parallel", "parallel", "arbitrary")),
        name="neighborhood_attention",
    )(qkv, qkv, qkv, bias)


def _trunk(x, mem, B, L, prm, wb):
    n_mem = mem.shape[0] // B
    depth = prm["norm_g"].shape[0]
    hw3 = 3 * HY_WIDTH
    for layer in range(depth):
        i = layer // 2
        g = prm["norm_g"][layer]
        wo = wb["mix_wo"]
        if layer % 2 == 0:
            ident = jnp.zeros((3, S5_WIDTH), F32).at[1].set(1.0)
            cw = jnp.concatenate([prm["hy_short_w"][i].astype(F32), ident], axis=1)
            cb = jnp.concatenate([prm["hy_short_b"][i].astype(F32), jnp.zeros((S5_WIDTH,), F32)])
            uc = norm_matmul_conv(x, g[0], wb["ev_w_in"], cw, cb, L, layer=i)
            z = hyena_mixer(uc, B, L, prm["hy_w1"][i], prm["hy_b1"][i], prm["hy_freq"][i], prm["hy_w2"][i],
                            prm["hy_b2"][i], prm["hy_w3"][i], prm["hy_skip"][i])
            ss = s5_mixer(uc, hw3, B, L, prm["s5_a_re"][i], prm["s5_a_im"][i], prm["s5_log_dt"][i],
                          prm["s5_b_re"][i], prm["s5_b_im"][i], prm["s5_c_re"][i], prm["s5_c_im"][i],
                          prm["s5_d"][i], prm["s5_w_glu"][i])
            ops = [(z, wo, prm["hy_out_g"][i], 0), (ss, wo, None, HY_WIDTH)]
        else:
            p = norm_matmul(x, g[0], wb["od_w_in"], BF16, layer=i, name="odd_in_proj")
            ret = retention_mixer(p, B, L, prm["ret_decay"][i])
            na = neighborhood_mixer(p, 4 * RET_WIDTH, B, L, prm["na_rpb"][i])
            ops = [(ret, wo, None, 0), (na, wo, None, RET_WIDTH)]
        x, xn = matmul_norm_residual(ops, g[1], x, g[2], layer=layer, name="mix_out_proj")
        kv = norm_matmul(mem, prm["mem_norm_g"][layer], wb["xa_wkv"], BF16, layer=layer, name="xattn_kv_proj")
        x = xattn_block(xn, x, wb["xa_wq"], kv, wb["xa_wo"], g[3], B, L, n_mem, layer=layer)
        x = ffn_block(x, g[4], wb["ffn_wg"], wb["ffn_wu"], wb["ffn_wd"], g[5], layer=layer)
    return x


def kernel(x_prompt, x_sample, mem_prompt, mem_sample, norm_g, mix_wo, ev_w_in, hy_short_w, hy_short_b, hy_w1, hy_b1, hy_freq, hy_w2, hy_b2, hy_w3, hy_skip, hy_out_g, s5_a_re, s5_a_im, s5_log_dt, s5_b_re, s5_b_im, s5_c_re, s5_c_im, s5_d, s5_w_glu, od_w_in, ret_decay, na_rpb, mem_norm_g, xa_wq, xa_wkv, xa_wo, ffn_wg, ffn_wu, ffn_wd):
    prm = dict(norm_g=norm_g, hy_short_w=hy_short_w, hy_short_b=hy_short_b, hy_w1=hy_w1, hy_b1=hy_b1,
               hy_freq=hy_freq, hy_w2=hy_w2, hy_b2=hy_b2, hy_w3=hy_w3, hy_skip=hy_skip, hy_out_g=hy_out_g,
               s5_a_re=s5_a_re, s5_a_im=s5_a_im, s5_log_dt=s5_log_dt, s5_b_re=s5_b_re, s5_b_im=s5_b_im,
               s5_c_re=s5_c_re, s5_c_im=s5_c_im, s5_d=s5_d, s5_w_glu=s5_w_glu, ret_decay=ret_decay,
               na_rpb=na_rpb, mem_norm_g=mem_norm_g)
    wb = {k: v.astype(BF16) for k, v in dict(mix_wo=mix_wo, ev_w_in=ev_w_in, od_w_in=od_w_in, xa_wq=xa_wq,
                                             xa_wkv=xa_wkv, xa_wo=xa_wo, ffn_wg=ffn_wg, ffn_wu=ffn_wu,
                                             ffn_wd=ffn_wd).items()}
    outs = []
    for x, mem in ((x_prompt, mem_prompt), (x_sample, mem_sample)):
        B, L, D = x.shape
        y = _trunk(x.reshape(B * L, D), mem.reshape(-1, D), B, L, prm, wb)
        outs.append(y.reshape(B, L, D))
    return tuple(outs)
```

```python
import functools
import math

import jax
import jax.numpy as jnp
from jax import lax
from jax.experimental import pallas as pl
from jax.experimental.pallas import tpu as pltpu

F32 = jnp.float32
BF16 = jnp.bfloat16

V7X_VMEM_BYTES = 64 * 1024 * 1024
VMEM_LIMIT = V7X_VMEM_BYTES - 8 * 1024 * 1024
LANES = 128
SUBLANES = 8

D_MODEL = 2048
GRID_W = 64
HY_WIDTH = 3 * D_MODEL // 4
S5_WIDTH = D_MODEL - HY_WIDTH
S5_GROUP = 16
S5_GROUPS = S5_WIDTH // S5_GROUP
S5_STATE = 64
HY_ORDER = 2
HY_BANDS = 16
HY_FILTER_HIDDEN = 64
HY_DECAY_TARGET = 1e-2
HY_SHORT_DECAY_PCT = 0.3
HY_LONG_DECAY_PCT = 1.5
RET_WIDTH = D_MODEL // 2
RET_HEADS = 4
RET_HEAD_DIM = RET_WIDTH // RET_HEADS
ROPE_BASE = 10000.0
NA_WIDTH = D_MODEL - RET_WIDTH
NA_HEADS = 16
NA_HEAD_DIM = NA_WIDTH // NA_HEADS
NA_WIN_ROWS = 8
NA_WIN_COLS = 16
XA_HEADS = 4
XA_HEAD_DIM = D_MODEL // XA_HEADS
RMS_EPS = 1e-6
GN_EPS = 1e-6

FFT_N2 = 128
FFT_RB = SUBLANES
S5_T = 32
RET_CC = 256
RET_SUB = 4
FFN_SPLIT = 2
NA_HG = 4
NA_RB = 16
NEG_INF = -1e30


def _cp(sem, vmem=VMEM_LIMIT):
    return pltpu.CompilerParams(dimension_semantics=sem, vmem_limit_bytes=vmem)


def _rms(x, g, eps=RMS_EPS):
    return x * lax.rsqrt(jnp.mean(x * x, axis=-1, keepdims=True) + eps) * g


def _norm_mm_kernel(x_ref, g_ref, w_ref, o_ref, xn_ref):
    @pl.when(pl.program_id(1) == 0)
    def _():
        tm = x_ref.shape[0]
        rc = min(tm, 256)
        for r in range(0, tm, rc):
            xn_ref[r:r + rc, :] = _rms(x_ref[r:r + rc, :], g_ref[...]).astype(BF16)

    o_ref[...] = jnp.dot(xn_ref[...], w_ref[...], preferred_element_type=F32).astype(o_ref.dtype)


def _stacked(w):
    return w if w.ndim == 3 else w[None]


def norm_matmul(x, g, w, out_dtype, col0=0, ncols=None, layer=0, tm=1024, tn=1024, name="norm_matmul"):
    M, K = x.shape
    w = _stacked(w)
    N = w.shape[2] - col0 if ncols is None else ncols
    tm, tn = min(tm, M), min(tn, N)
    assert M % tm == 0 and N % tn == 0 and col0 % tn == 0
    cb0 = col0 // tn
    return pl.pallas_call(
        _norm_mm_kernel,
        out_shape=jax.ShapeDtypeStruct((M, N), out_dtype),
        grid=(M // tm, N // tn),
        in_specs=[pl.BlockSpec((tm, K), lambda i, j: (i, 0)),
                  pl.BlockSpec((1, K), lambda i, j: (0, 0)),
                  pl.BlockSpec((None, K, tn), lambda i, j: (layer, 0, cb0 + j))],
        out_specs=pl.BlockSpec((tm, tn), lambda i, j: (i, j)),
        scratch_shapes=[pltpu.VMEM((tm, K), BF16)],
        compiler_params=_cp(("parallel", "arbitrary")),
        name=name,
    )(x, g.reshape(1, K), w)


def _mm_norm_res_kernel(*refs, n_ops, prenorm):
    g_ref, x_ref, gn_ref, o_ref, on_ref = refs[-5:]
    tm = x_ref.shape[0]
    rc = tm // 2 if tm % 32 == 0 else tm
    for r in range(0, tm, rc):
        pos = 0
        y = None
        for t in range(n_ops):
            a = refs[pos][r:r + rc, :]
            w_ref = refs[pos + 1]
            pos += 2
            if prenorm[t]:
                a = _rms(a.astype(F32), refs[pos][...])
                pos += 1
            d = jnp.dot(a.astype(BF16), w_ref[...], preferred_element_type=F32)
            y = d if y is None else y + d
        x1 = x_ref[r:r + rc, :] + _rms(y, g_ref[...])
        o_ref[r:r + rc, :] = x1
        on_ref[r:r + rc, :] = _rms(x1, gn_ref[...]).astype(on_ref.dtype)


def matmul_norm_residual(ops, g, x, g_next, layer=0, tm=512, name="matmul_norm_residual"):
    M, N = x.shape
    tm = min(tm, M)
    assert M % tm == 0
    args, specs, prenorm = [], [], []
    for a, w, pg, k0 in ops:
        kt = a.shape[1]
        assert k0 % kt == 0
        args += [a, _stacked(w)]
        specs += [pl.BlockSpec((tm, kt), lambda i: (i, 0)),
                  pl.BlockSpec((None, kt, N), lambda i, kb=k0 // kt: (layer, kb, 0))]
        prenorm.append(pg is not None)
        if pg is not None:
            args.append(pg.reshape(1, kt))
            specs.append(pl.BlockSpec((1, kt), lambda i: (0, 0)))
    row = pl.BlockSpec((1, N), lambda i: (0, 0))
    tile = pl.BlockSpec((tm, N), lambda i: (i, 0))
    args += [g.reshape(1, N), x, g_next.reshape(1, N)]
    specs += [row, tile, row]
    return pl.pallas_call(
        functools.partial(_mm_norm_res_kernel, n_ops=len(ops), prenorm=tuple(prenorm)),
        out_shape=(jax.ShapeDtypeStruct((M, N), F32), jax.ShapeDtypeStruct((M, N), BF16)),
        grid=(M // tm,),
        in_specs=specs,
        out_specs=(tile, tile),
        compiler_params=_cp(("parallel",)),
        name=name,
    )(*args)


def _xattn_block_kernel(xn_ref, x_ref, wq_ref, k_ref, v_ref, wo_ref, g_ref, o_ref, *, heads):
    dh = xn_ref.shape[1] // heads
    scale = dh ** -0.5
    q = jnp.dot(xn_ref[...], wq_ref[...], preferred_element_type=F32).astype(BF16)
    outs = []
    for h in range(heads):
        sl = slice(h * dh, (h + 1) * dh)
        s = lax.dot_general(q[:, sl], k_ref[:, sl], (((1,), (1,)), ((), ())), preferred_element_type=F32) * scale
        m = jnp.max(s, axis=-1, keepdims=True)
        p = jnp.exp(s - m)
        l = jnp.sum(p, axis=-1, keepdims=True)
        outs.append((jnp.dot(p.astype(BF16), v_ref[:, sl], preferred_element_type=F32) / l).astype(BF16))
    o = jnp.concatenate(outs, axis=-1)
    y = jnp.dot(o, wo_ref[...], preferred_element_type=F32)
    o_ref[...] = x_ref[...] + _rms(y, g_ref[...])


def xattn_block(xn, x, wq, kv, wo, g, B, L, n_mem, layer=0, tm=512):
    M, D = x.shape
    tm = min(tm, L)
    assert L % tm == 0
    bpl = L // tm
    wq, wo = _stacked(wq), _stacked(wo)
    wspec = pl.BlockSpec((None, D, D), lambda i: (layer, 0, 0), pipeline_mode=pl.Buffered(1))
    return pl.pallas_call(
        functools.partial(_xattn_block_kernel, heads=XA_HEADS),
        out_shape=jax.ShapeDtypeStruct((M, D), F32),
        grid=(M // tm,),
        in_specs=[pl.BlockSpec((tm, D), lambda i: (i, 0)),
                  pl.BlockSpec((tm, D), lambda i: (i, 0)),
                  wspec,
                  pl.BlockSpec((n_mem, D), lambda i: (i // bpl, 0)),
                  pl.BlockSpec((n_mem, D), lambda i: (i // bpl, 1)),
                  wspec,
                  pl.BlockSpec((1, D), lambda i: (0, 0))],
        out_specs=pl.BlockSpec((tm, D), lambda i: (i, 0)),
        compiler_params=_cp(("parallel",)),
        name="xattn_block",
    )(xn, x, wq, kv, kv, wo, g.reshape(1, D))


def _ffn_kernel(x_ref, gi_ref, wg_ref, wu_ref, wd_ref, go_ref, o_ref, xn_ref):
    j = pl.program_id(1)

    tm = x_ref.shape[0]
    rc = min(tm, 256)

    @pl.when(j == 0)
    def _():
        for r in range(0, tm, rc):
            xn_ref[r:r + rc, :] = _rms(x_ref[r:r + rc, :], gi_ref[...]).astype(BF16)
        o_ref[...] = jnp.zeros_like(o_ref)

    th = wg_ref.shape[1]
    hh = th // FFN_SPLIT
    for c in range(FFN_SPLIT):
        xn = xn_ref[...]
        a = jnp.dot(xn, wg_ref[:, c * hh:(c + 1) * hh], preferred_element_type=F32)
        u = jnp.dot(xn, wu_ref[:, c * hh:(c + 1) * hh], preferred_element_type=F32)
        h = (a * jax.nn.sigmoid(a) * u).astype(BF16)
        o_ref[...] += jnp.dot(h, wd_ref[c * hh:(c + 1) * hh, :], preferred_element_type=F32)

    @pl.when(j == pl.num_programs(1) - 1)
    def _():
        for r in range(0, tm, SUBLANES):
            o_ref[r:r + SUBLANES, :] = (x_ref[r:r + SUBLANES, :]
                                        + _rms(o_ref[r:r + SUBLANES, :], go_ref[...]))


def ffn_block(x, g_in, wg, wu, wd, g_out, layer=0, tm=1024, th=512):
    M, D = x.shape
    wg, wu, wd = _stacked(wg), _stacked(wu), _stacked(wd)
    Hd = wg.shape[2]
    tm, th = min(tm, M), min(th, Hd)
    assert M % tm == 0 and Hd % th == 0
    return pl.pallas_call(
        _ffn_kernel,
        out_shape=jax.ShapeDtypeStruct((M, D), F32),
        grid=(M // tm, Hd // th),
        in_specs=[pl.BlockSpec((tm, D), lambda i, j: (i, 0)),
                  pl.BlockSpec((1, D), lambda i, j: (0, 0)),
                  pl.BlockSpec((None, D, th), lambda i, j: (layer, 0, j)),
                  pl.BlockSpec((None, D, th), lambda i, j: (layer, 0, j)),
                  pl.BlockSpec((None, th, D), lambda i, j: (layer, j, 0)),
                  pl.BlockSpec((1, D), lambda i, j: (0, 0))],
        out_specs=pl.BlockSpec((tm, D), lambda i, j: (i, 0)),
        scratch_shapes=[pltpu.VMEM((tm, D), BF16)],
        compiler_params=_cp(("parallel", "arbitrary")),
        name="ffn_block",
    )(x, g_in.reshape(1, D), wg, wu, wd, g_out.reshape(1, D))


HALO = 16


def _norm_mm_conv_kernel(xp_ref, xc_ref, xn_ref, g_ref, w_ref, cw_ref, cb_ref, o_ref, xs_ref, *, blocks_per_seq):
    tm = xc_ref.shape[0]

    @pl.when(pl.program_id(1) == 0)
    def _():
        li = pl.program_id(0) % blocks_per_seq
        g = g_ref[...]
        keep_prev = jnp.where(li == 0, 0.0, 1.0)
        keep_next = jnp.where(li == blocks_per_seq - 1, 0.0, 1.0)
        xs_ref[0:HALO] = (_rms(xp_ref[...], g) * keep_prev).astype(BF16)
        rc = min(tm, 256)
        for r in range(0, tm, rc):
            xs_ref[HALO + r:HALO + r + rc] = _rms(xc_ref[r:r + rc, :], g).astype(BF16)
        xs_ref[HALO + tm:] = (_rms(xn_ref[...], g) * keep_next).astype(BF16)

    n = tm + 2 * HALO
    tn = o_ref.shape[1]
    hw = tn // 2 if tn % (2 * LANES) == 0 else tn
    for c in range(0, tn, hw):
        cs = slice(c, c + hw)
        y = jnp.dot(xs_ref[...], w_ref[:, cs], preferred_element_type=F32)
        up = pltpu.roll(y, 1, axis=0)[HALO:HALO + tm]
        dn = pltpu.roll(y, n - 1, axis=0)[HALO:HALO + tm]
        o_ref[:, cs] = (up * cw_ref[0:1, cs] + y[HALO:HALO + tm] * cw_ref[1:2, cs] + dn * cw_ref[2:3, cs]
                        + cb_ref[:, cs])


def norm_matmul_conv(x, g, w, cw, cb, L, layer=0, tm=1024, tn=1024):
    M, K = x.shape
    w = _stacked(w)
    N = w.shape[2]
    tm, tn = min(tm, L), min(tn, N)
    assert L % tm == 0 and N % tn == 0 and tm % HALO == 0
    hb = tm // HALO
    nhb = M // HALO
    return pl.pallas_call(
        functools.partial(_norm_mm_conv_kernel, blocks_per_seq=L // tm),
        out_shape=jax.ShapeDtypeStruct((M, N), F32),
        grid=(M // tm, N // tn),
        in_specs=[pl.BlockSpec((HALO, K), lambda i, j: (jnp.maximum(i * hb - 1, 0), 0)),
                  pl.BlockSpec((tm, K), lambda i, j: (i, 0)),
                  pl.BlockSpec((HALO, K), lambda i, j: (jnp.minimum((i + 1) * hb, nhb - 1), 0)),
                  pl.BlockSpec((1, K), lambda i, j: (0, 0)),
                  pl.BlockSpec((None, K, tn), lambda i, j: (layer, 0, j)),
                  pl.BlockSpec((3, tn), lambda i, j: (0, j)),
                  pl.BlockSpec((1, tn), lambda i, j: (0, j))],
        out_specs=pl.BlockSpec((tm, tn), lambda i, j: (i, j)),
        scratch_shapes=[pltpu.VMEM((tm + 2 * HALO, K), BF16)],
        compiler_params=_cp(("parallel", "arbitrary")),
        name="even_in_proj_conv",
    )(x, x, x, g.reshape(1, K), w, cw, cb.reshape(1, N))


def _filter_kernel(bands_ref, w1t_ref, w1c_ref, w1s_ref, b1_ref, f_ref, w2_ref, b2_ref, w3h_ref, w3l_ref, dl_ref,
                   sk_ref, o_ref, hid_ref, *, L, ncb):
    hi = lax.Precision.HIGHEST
    s = pl.program_id(1)
    tl = o_ref.shape[0]
    i0 = pl.program_id(0) * tl

    @pl.when(s == 0)
    def _():
        m_row = i0 + lax.broadcasted_iota(jnp.int32, (1, tl), 1)
        for d, pos in enumerate((m_row, L - m_row)):
            t = pos.astype(F32) * (1.0 / L)
            ang = (2.0 * math.pi) * bands_ref[...] * t
            pre = (w1t_ref[...] * t
                   + jnp.dot(w1c_ref[...], jnp.cos(ang), preferred_element_type=F32, precision=hi)
                   + jnp.dot(w1s_ref[...], jnp.sin(ang), preferred_element_type=F32, precision=hi)
                   + b1_ref[...])
            h = jnp.sin(f_ref[0] * pre)
            h = jnp.sin(f_ref[1] * (jnp.dot(w2_ref[...], h, preferred_element_type=F32, precision=hi) + b2_ref[...]))
            h = h.T
            h_hi = h.astype(BF16)
            hid_ref[d, 0] = h_hi
            hid_ref[d, 1] = (h - h_hi.astype(F32)).astype(BF16)

    d = (s // ncb) % 2
    m = i0 + lax.broadcasted_iota(jnp.int32, (tl, 1), 0)
    t = jnp.where(d == 0, m, L - m).astype(F32) * (1.0 / L)
    out = (jnp.dot(hid_ref[d, 0], w3h_ref[...], preferred_element_type=F32)
           + jnp.dot(hid_ref[d, 0], w3l_ref[...], preferred_element_type=F32)
           + jnp.dot(hid_ref[d, 1], w3h_ref[...], preferred_element_type=F32))
    out = out * jnp.exp(-t * dl_ref[...])
    o_ref[...] = out

    @pl.when(pl.program_id(0) == 0)
    def _():
        o_ref[0:1, :] = jnp.where(d == 0, out[0:1, :] + sk_ref[...], 0.0)


def hyena_filters(w1, b1, freq, w2, b2, w3, skip, L, tl=1024, W=512):
    C = HY_WIDTH
    Hh = HY_FILTER_HIDDEN
    nb = HY_BANDS
    tl = min(tl, L)
    col = lambda v: v.astype(F32).reshape(-1, 1)
    bands = col(jnp.arange(1, nb + 1, dtype=F32))
    w1 = w1.astype(F32)
    w3_hi = w3.astype(BF16)
    w3_lo = (w3.astype(F32) - w3_hi.astype(F32)).astype(BF16)
    deltas = jnp.abs(jnp.linspace(math.log(HY_DECAY_TARGET) / HY_LONG_DECAY_PCT,
                                  math.log(HY_DECAY_TARGET) / HY_SHORT_DECAY_PCT, C, dtype=F32)).reshape(1, C)
    ncb = C // W
    small = lambda *shape: pl.BlockSpec(shape, lambda i, s: (0,) * len(shape))
    return pl.pallas_call(
        functools.partial(_filter_kernel, L=L, ncb=ncb),
        out_shape=jax.ShapeDtypeStruct((2 * HY_ORDER, L, C), F32),
        grid=(L // tl, 2 * HY_ORDER * ncb),
        in_specs=[small(nb, 1), small(Hh, 1), small(Hh, nb), small(Hh, nb), small(Hh, 1),
                  small(2, Hh, 1), small(Hh, Hh), small(Hh, 1),
                  pl.BlockSpec((Hh, W), lambda i, s: (0, s)),
                  pl.BlockSpec((Hh, W), lambda i, s: (0, s)),
                  pl.BlockSpec((1, W), lambda i, s: (0, s % ncb)),
                  pl.BlockSpec((None, 1, W), lambda i, s: (s // (2 * ncb), 0, s % ncb))],
        out_specs=pl.BlockSpec((None, tl, W), lambda i, s: (s // ncb, i, s % ncb)),
        scratch_shapes=[pltpu.VMEM((2, 2, tl, Hh), BF16)],
        compiler_params=_cp(("parallel", "arbitrary")),
        name="hyena_filters",
    )(bands, col(w1[0]), w1[1:1 + nb].T, w1[1 + nb:].T, col(b1), freq.astype(F32)[:, :, None], w2.astype(F32).T,
      col(b2), w3_hi, w3_lo, deltas, skip.astype(F32).reshape(HY_ORDER, 1, C))


def _fft_tables(L):
    N2 = FFT_N2
    N = 2 * L
    N1 = N // N2
    N1h = N1 // 2
    n2 = jnp.arange(N2, dtype=jnp.int32)[:, None, None]
    k1 = jnp.arange(N1, dtype=jnp.int32)[None, :, None]
    n1 = jnp.arange(N1h, dtype=jnp.int32)[None, None, :]
    ph = ((n1 * N2 + n2) * k1) % N
    ang = ph.astype(F32) * (2.0 * math.pi / N)
    gr, gi = jnp.cos(ang), -jnp.sin(ang)
    g_fwd = jnp.concatenate([jnp.concatenate([gr, -gi], 2), jnp.concatenate([gi, gr], 2)], 1).astype(BF16)
    sgn = jnp.where(k1 % 2 == 0, 1.0, -1.0).astype(F32)
    g_flt = jnp.concatenate([jnp.concatenate([gr, sgn * gr], 2), jnp.concatenate([gi, sgn * gi], 2)], 1).astype(BF16)
    er = jnp.swapaxes(gr, 1, 2) / N
    ei = -jnp.swapaxes(gi, 1, 2) / N
    g_inv = jnp.concatenate([jnp.concatenate([er, -ei], 2), jnp.concatenate([ei, er], 2)], 1).astype(BF16)
    a2 = (jnp.arange(N2, dtype=jnp.int32)[:, None] * jnp.arange(N2, dtype=jnp.int32)[None, :]) % N2
    ang2 = a2.astype(F32) * (2.0 * math.pi / N2)
    fr, fi = jnp.cos(ang2), -jnp.sin(ang2)
    f_mid = jnp.concatenate([jnp.concatenate([fr, -fi], 1), jnp.concatenate([fi, fr], 1)], 0).astype(BF16)
    f_mid_inv = jnp.concatenate([jnp.concatenate([fr, fi], 1), jnp.concatenate([-fi, fr], 1)], 0).astype(BF16)
    return dict(g_fwd=g_fwd, g_flt=g_flt, g_inv=g_inv, f_mid=f_mid, f_mid_inv=f_mid_inv, N1=N1)


def _pack_complex(re, im):
    r = lax.bitcast_convert_type(re.astype(BF16).astype(F32), jnp.uint32)
    i = lax.bitcast_convert_type(im.astype(BF16).astype(F32), jnp.uint32)
    return r | (i >> 16)


def _unpack_complex(w):
    re = lax.bitcast_convert_type(w & jnp.uint32(0xFFFF0000), F32)
    im = lax.bitcast_convert_type(w << 16, F32)
    return re, im


def _to_slabs(src_ref, slab_ref, lead=()):
    ns, rows, _ = slab_ref.shape
    for s in range(ns):
        slab_ref[s] = src_ref[lead + (slice(None), slice(None), slice(s * LANES, (s + 1) * LANES))].reshape(
            rows, LANES)


def _slab_rows(slab_ref, r, n):
    return jnp.concatenate([slab_ref.at[s][pl.ds(r, n, stride=FFT_RB), :] for s in range(slab_ref.shape[0])],
                           axis=1)


def _set_slab_rows(slab_ref, r, val):
    n = val.shape[0]
    for s in range(slab_ref.shape[0]):
        slab_ref.at[s][pl.ds(r, n, stride=FFT_RB), :] = val[:, s * LANES:(s + 1) * LANES]


def _from_slabs(slab_ref, dst_ref, lead=()):
    ns, rows, _ = slab_ref.shape
    for s in range(ns):
        dst_ref[lead + (slice(None), slice(None), slice(s * LANES, (s + 1) * LANES))] = (
            slab_ref[s].reshape(rows // FFT_RB, FFT_RB, LANES))


def _fft_a_kernel(x_ref, g_ref, z_ref, xa_ref, xb_ref, zs_ref):
    n1 = z_ref.shape[0]
    _to_slabs(x_ref, xa_ref, (0,))
    _to_slabs(x_ref, xb_ref, (1,))
    for r in range(FFT_RB):
        xs = jnp.concatenate([_slab_rows(xa_ref, r, n1 // 2), _slab_rows(xb_ref, r, n1 // 2)], axis=0).astype(BF16)
        a = jnp.dot(g_ref[r], xs, preferred_element_type=F32)
        _set_slab_rows(zs_ref, r, _pack_complex(a[:n1], a[n1:]))
    _from_slabs(zs_ref, z_ref)


def fft_stage_a(x4, col0, C, g, W=512):
    B, N1h, N2, _ = x4.shape
    N1 = 2 * N1h
    W = min(W, C)
    assert B % 2 == 0 and C % W == 0 and col0 % W == 0 and N2 % FFT_RB == 0
    cb0 = col0 // W
    return pl.pallas_call(
        _fft_a_kernel,
        out_shape=jax.ShapeDtypeStruct((B // 2, N1, N2, C), jnp.uint32),
        grid=(B // 2, N2 // FFT_RB, C // W),
        in_specs=[pl.BlockSpec((2, N1h, FFT_RB, W), lambda p, j, c: (p, 0, j, cb0 + c)),
                  pl.BlockSpec((FFT_RB, 2 * N1, N1), lambda p, j, c: (j, 0, 0))],
        out_specs=pl.BlockSpec((None, N1, FFT_RB, W), lambda p, j, c: (p, 0, j, c)),
        scratch_shapes=[pltpu.VMEM((W // LANES, N1h * FFT_RB, LANES), F32),
                        pltpu.VMEM((W // LANES, N1h * FFT_RB, LANES), F32),
                        pltpu.VMEM((W // LANES, N1 * FFT_RB, LANES), jnp.uint32)],
        compiler_params=_cp(("parallel", "parallel", "parallel")),
        name="hyena_fft_a",
    )(x4, g)


def _fft_mid_kernel(*refs, kb, with_filter):
    if with_filter:
        z_ref, h_ref, f_ref, fi_ref, y_ref = refs
    else:
        z_ref, f_ref, y_ref = refs
    n2 = z_ref.shape[1]
    for k in range(kb):
        zr, zi = _unpack_complex(z_ref[k])
        zs = jnp.concatenate([zr, zi], axis=0).astype(BF16)
        x = jnp.dot(f_ref[...], zs, preferred_element_type=F32)
        if with_filter:
            xr, xi = x[:n2], x[n2:]
            hr, hi = _unpack_complex(h_ref[k])
            ys = jnp.concatenate([xr * hr - xi * hi, xr * hi + xi * hr], axis=0).astype(BF16)
            x = jnp.dot(fi_ref[...], ys, preferred_element_type=F32)
        y_ref[k] = _pack_complex(x[:n2], x[n2:])


def fft_stage_mid(z, tabs, h=None, order=0, kb=16, W=512):
    P, N1, N2, C = z.shape
    kb, W = min(kb, N1), min(W, C)
    assert N1 % kb == 0 and C % W == 0
    blk = pl.BlockSpec((None, kb, N2, W), lambda k, c, p: (p, k, 0, c))
    mat = pl.BlockSpec((2 * N2, 2 * N2), lambda k, c, p: (0, 0))
    if h is None:
        args, specs = (z, tabs["f_mid"]), [blk, mat]
    else:
        hblk = pl.BlockSpec((None, kb, N2, W), lambda k, c, p: (order, k, 0, c))
        args, specs = (z, h, tabs["f_mid"], tabs["f_mid_inv"]), [blk, hblk, mat, mat]
    return pl.pallas_call(
        functools.partial(_fft_mid_kernel, kb=kb, with_filter=h is not None),
        out_shape=jax.ShapeDtypeStruct(z.shape, jnp.uint32),
        grid=(N1 // kb, C // W, P),
        in_specs=specs,
        out_specs=blk,
        compiler_params=_cp(("parallel", "parallel", "parallel")),
        name="hyena_fft_mid",
    )(*args)


def _fft_c_kernel(y_ref, g_ref, x_ref, o_ref, ys_ref, xa_ref, xb_ref):
    n1h = x_ref.shape[1]
    _to_slabs(y_ref, ys_ref)
    _to_slabs(x_ref, xa_ref, (0,))
    _to_slabs(x_ref, xb_ref, (1,))
    for r in range(FFT_RB):
        yr, yi = _unpack_complex(_slab_rows(ys_ref, r, 2 * n1h))
        ys = jnp.concatenate([yr, yi], axis=0).astype(BF16)
        c = jnp.dot(g_ref[r], ys, preferred_element_type=F32)
        _set_slab_rows(xa_ref, r, _slab_rows(xa_ref, r, n1h) * c[:n1h])
        _set_slab_rows(xb_ref, r, _slab_rows(xb_ref, r, n1h) * c[n1h:])
    _from_slabs(xa_ref, o_ref, (0,))
    _from_slabs(xb_ref, o_ref, (1,))


def fft_stage_c(y, g, x4, xcol0, W=512):
    P, N1, N2, C = y.shape
    N1h = N1 // 2
    W = min(W, C)
    assert C % W == 0 and xcol0 % W == 0
    xb = xcol0 // W
    slab = lambda n, dt: pltpu.VMEM((W // LANES, n * FFT_RB, LANES), dt)
    return pl.pallas_call(
        _fft_c_kernel,
        out_shape=jax.ShapeDtypeStruct((2 * P, N1h, N2, C), F32),
        grid=(P, N2 // FFT_RB, C // W),
        in_specs=[pl.BlockSpec((None, N1, FFT_RB, W), lambda p, j, c: (p, 0, j, c)),
                  pl.BlockSpec((FFT_RB, N1, 2 * N1), lambda p, j, c: (j, 0, 0)),
                  pl.BlockSpec((2, N1h, FFT_RB, W), lambda p, j, c: (p, 0, j, xb + c))],
        out_specs=pl.BlockSpec((2, N1h, FFT_RB, W), lambda p, j, c: (p, 0, j, c)),
        scratch_shapes=[slab(N1, jnp.uint32), slab(N1h, F32), slab(N1h, F32)],
        compiler_params=_cp(("parallel", "parallel", "parallel")),
        name="hyena_fft_c",
    )(y, g, x4)


def _fft_ca_kernel(y_ref, gi_ref, x_ref, gf_ref, z_ref, ys_ref, xa_ref, xb_ref, zs_ref):
    n1h = x_ref.shape[1]
    n1 = 2 * n1h
    _to_slabs(y_ref, ys_ref)
    _to_slabs(x_ref, xa_ref, (0,))
    _to_slabs(x_ref, xb_ref, (1,))
    for r in range(FFT_RB):
        yr, yi = _unpack_complex(_slab_rows(ys_ref, r, n1))
        c = jnp.dot(gi_ref[r], jnp.concatenate([yr, yi], axis=0).astype(BF16), preferred_element_type=F32)
        gated = jnp.concatenate([_slab_rows(xa_ref, r, n1h) * c[:n1h], _slab_rows(xb_ref, r, n1h) * c[n1h:]], axis=0)
        a = jnp.dot(gf_ref[r], gated.astype(BF16), preferred_element_type=F32)
        _set_slab_rows(zs_ref, r, _pack_complex(a[:n1], a[n1:]))
    _from_slabs(zs_ref, z_ref)


def fft_stage_ca(y, g_inv, x4, xcol0, g_fwd, W=512):
    P, N1, N2, C = y.shape
    N1h = N1 // 2
    W = min(W, C)
    assert C % W == 0 and xcol0 % W == 0
    xb = xcol0 // W
    slab = lambda n, dt: pltpu.VMEM((W // LANES, n * FFT_RB, LANES), dt)
    zblk = pl.BlockSpec((None, N1, FFT_RB, W), lambda p, j, c: (p, 0, j, c))
    return pl.pallas_call(
        _fft_ca_kernel,
        out_shape=jax.ShapeDtypeStruct((P, N1, N2, C), jnp.uint32),
        grid=(P, N2 // FFT_RB, C // W),
        in_specs=[zblk,
                  pl.BlockSpec((FFT_RB, N1, 2 * N1), lambda p, j, c: (j, 0, 0)),
                  pl.BlockSpec((2, N1h, FFT_RB, W), lambda p, j, c: (p, 0, j, xb + c)),
                  pl.BlockSpec((FFT_RB, 2 * N1, N1), lambda p, j, c: (j, 0, 0))],
        out_specs=zblk,
        scratch_shapes=[slab(N1, jnp.uint32), slab(N1h, F32), slab(N1h, F32), slab(N1, jnp.uint32)],
        compiler_params=_cp(("parallel", "parallel", "parallel")),
        name="hyena_fft_ca",
    )(y, g_inv, x4, g_fwd)


def hyena_mixer(uc, B, L, w1, b1, freq, w2, b2, w3, skip):
    C = HY_WIDTH
    N2 = FFT_N2
    N1h = L // N2
    tabs = _fft_tables(L)
    uc4 = uc.reshape(B, N1h, N2, uc.shape[1])
    filt = hyena_filters(w1, b1, freq, w2, b2, w3, skip, L)
    spec = fft_stage_mid(fft_stage_a(filt.reshape(2 * HY_ORDER, N1h, N2, C), 0, C, tabs["g_flt"]), tabs)
    y = fft_stage_mid(fft_stage_a(uc4, 0, C, tabs["g_fwd"]), tabs, h=spec, order=0)
    zz = fft_stage_ca(y, tabs["g_inv"], uc4, C, tabs["g_fwd"])
    y = fft_stage_mid(zz, tabs, h=spec, order=1)
    z = fft_stage_c(y, tabs["g_inv"], uc4, 2 * C)
    return z.reshape(B * L, C)


def _s5_tables(a_re, a_im, log_dt, b_re, b_im, c_re, c_im, n_chunks):
    T = S5_T
    G, P, I = S5_GROUPS, S5_STATE, S5_GROUP
    f32 = lambda a: a.astype(F32)
    a_re, a_im, b_re, b_im, c_re, c_im = map(f32, (a_re, a_im, b_re, b_im, c_re, c_im))
    step = jnp.exp(f32(log_dt))[..., None]
    lr, li = a_re * step, a_im * step
    mag = jnp.exp(lr)
    br_, bi_ = mag * jnp.cos(li), mag * jnp.sin(li)
    den = a_re * a_re + a_im * a_im
    qr = ((br_ - 1.0) * a_re + bi_ * a_im) / den
    qi = (bi_ * a_re - (br_ - 1.0) * a_im) / den
    bbr = qr[..., None] * b_re - qi[..., None] * b_im
    bbi = qr[..., None] * b_im + qi[..., None] * b_re

    def lam_pow(k):
        k = k.astype(F32)
        m = jnp.exp(lr[..., None] * k)
        return m * jnp.cos(li[..., None] * k), m * jnp.sin(li[..., None] * k)

    lags = jnp.arange(T + 1)
    pr, pi_ = lam_pow(lags)
    cpr = c_re[..., None] * pr[:, :, None] - c_im[..., None] * pi_[:, :, None]
    cpi = c_re[..., None] * pi_[:, :, None] + c_im[..., None] * pr[:, :, None]
    kern = jnp.einsum("dgjpk,dgpi->dgkji", cpr, bbr) - jnp.einsum("dgjpk,dgpi->dgkji", cpi, bbi)
    tt = jnp.arange(T)
    lag = tt[:, None] - tt[None, :]
    sel_f = (lag[:, :, None] == lags).astype(F32)
    sel_b = (-lag[:, :, None] == lags).astype(F32)
    toep = (jnp.einsum("tuk,gkji->gtjui", sel_f, kern[0], precision=lax.Precision.HIGHEST)
            + jnp.einsum("tuk,gkji->gtjui", sel_b, kern[1], precision=lax.Precision.HIGHEST))
    toep = toep.reshape(G, T * I, T * I)
    ef = T - 1 - tt
    eb = tt

    def bst(d, e):
        wr, wi = pr[d][..., e], pi_[d][..., e]
        re = wr[..., None] * bbr[d][:, :, None] - wi[..., None] * bbi[d][:, :, None]
        im = wr[..., None] * bbi[d][:, :, None] + wi[..., None] * bbr[d][:, :, None]
        return jnp.concatenate([re, im], axis=1).reshape(G, 2 * P, T * I)

    bst_all = jnp.stack([bst(0, ef), bst(1, eb)], axis=1)

    def cst(d, e):
        xr, xi = cpr[d][..., e], cpi[d][..., e]
        m = jnp.concatenate([xr, -xi], axis=2)
        return m.transpose(0, 3, 1, 2).reshape(G, T * I, 2 * P)

    cst_all = jnp.stack([cst(0, tt + 1), cst(1, T - tt)], axis=1)
    nsteps = max(1, int(math.log2(n_chunks)))
    e2 = T * (2 ** jnp.arange(nsteps))
    ar, ai = lam_pow(e2)
    ap = jnp.stack([ar, ai], axis=-1).transpose(1, 0, 3, 4, 2)
    ap = jnp.broadcast_to(ap[..., None], ap.shape + (LANES,))
    return toep.astype(BF16), bst_all.astype(BF16), cst_all.astype(BF16), ap, nsteps


def _s5_kernel(u_ref, toep_ref, bst_ref, cst_ref, ap_ref, y_ref, *, n_chunks, nsteps):
    T, I, cols = u_ref.shape
    P = S5_STATE
    u = u_ref[...].reshape(T * I, cols).astype(BF16)
    y = jnp.dot(toep_ref[...], u, preferred_element_type=F32)
    cidx = lax.broadcasted_iota(jnp.int32, (P, cols), 1) % n_chunks

    def shifted(x, sh, d):
        if d == 0:
            return jnp.where(cidx >= sh, pltpu.roll(x, sh, axis=1), 0.0)
        return jnp.where(cidx < n_chunks - sh, pltpu.roll(x, cols - sh, axis=1), 0.0)

    for d in range(2):
        v = jnp.dot(bst_ref[d], u, preferred_element_type=F32)
        sr, si = v[:P], v[P:]
        for j in range(nsteps):
            if (1 << j) >= n_chunks:
                break
            ar, ai = ap_ref[d, j, 0][:, :1], ap_ref[d, j, 1][:, :1]
            rr, ri = shifted(sr, 1 << j, d), shifted(si, 1 << j, d)
            sr, si = sr + ar * rr - ai * ri, si + ar * ri + ai * rr
        s_in = jnp.concatenate([shifted(sr, 1, d), shifted(si, 1, d)], axis=0).astype(BF16)
        y = y + jnp.dot(cst_ref[d], s_in, preferred_element_type=F32)
    y_ref[...] = y.reshape(T, I, cols)


def _s5_out_kernel(y_ref, u_ref, d_ref, w_ref, o_ref):
    y = y_ref[...] + d_ref[...] * u_ref[...]
    y = jax.nn.gelu(y, approximate=True)
    z = jnp.dot(y.astype(BF16), w_ref[...], preferred_element_type=F32)
    o_ref[...] = (y * jax.nn.sigmoid(z)).astype(o_ref.dtype)


def s5_mixer(p, col0, B, L, a_re, a_im, log_dt, b_re, b_im, c_re, c_im, d, w_glu, tm=1024):
    T, Wd, I, G = S5_T, S5_WIDTH, S5_GROUP, S5_GROUPS
    nC = L // T
    cols = B * nC
    assert L % T == 0 and nC & (nC - 1) == 0 and cols % LANES == 0
    toep, bst, cst, ap, nsteps = _s5_tables(a_re, a_im, log_dt, b_re, b_im, c_re, c_im, nC)
    u = p[:, col0:col0 + Wd]
    ut = u.reshape(B, nC, T, Wd).transpose(2, 3, 0, 1).reshape(T, Wd, cols)
    blk = pl.BlockSpec((T, I, cols), lambda g: (0, g, 0))
    yt = pl.pallas_call(
        functools.partial(_s5_kernel, n_chunks=nC, nsteps=nsteps),
        out_shape=jax.ShapeDtypeStruct((T, Wd, cols), F32),
        grid=(G,),
        in_specs=[blk,
                  pl.BlockSpec((None, T * I, T * I), lambda g: (g, 0, 0)),
                  pl.BlockSpec((None, 2, 2 * S5_STATE, T * I), lambda g: (g, 0, 0, 0)),
                  pl.BlockSpec((None, 2, T * I, 2 * S5_STATE), lambda g: (g, 0, 0, 0)),
                  pl.BlockSpec((None, 2, nsteps, 2, S5_STATE, LANES), lambda g: (g, 0, 0, 0, 0, 0))],
        out_specs=blk,
        compiler_params=_cp(("parallel",)),
        name="s5_scan",
    )(ut, toep, bst, cst, ap)
    y = yt.reshape(T, Wd, B, nC).transpose(2, 3, 0, 1).reshape(B * L, Wd)
    M = B * L
    tm = min(tm, M)
    cb = col0 // Wd
    assert col0 % Wd == 0 and M % tm == 0
    return pl.pallas_call(
        _s5_out_kernel,
        out_shape=jax.ShapeDtypeStruct((M, Wd), BF16),
        grid=(M // tm,),
        in_specs=[pl.BlockSpec((tm, Wd), lambda i: (i, 0)),
                  pl.BlockSpec((tm, Wd), lambda i: (i, cb)),
                  pl.BlockSpec((1, Wd), lambda i: (0, 0)),
                  pl.BlockSpec((Wd, Wd), lambda i: (0, 0))],
        out_specs=pl.BlockSpec((tm, Wd), lambda i: (i, 0)),
        compiler_params=_cp(("parallel",)),
        name="s5_glu",
    )(y, p, d.astype(F32).reshape(1, Wd), w_glu.astype(BF16))


def _ret_kernel(q_ref, k_ref, v_ref, g_ref, cos_ref, sin_ref, dm_ref, qs_ref, ks_ref, cd_ref, o_ref, st_ref,
                carry_ref, *, n_chunks):
    c = pl.program_id(2)
    nC = n_chunks
    nS = nC // RET_SUB
    Cc = dm_ref.shape[0]
    dh = k_ref.shape[1]
    h2 = dh // 2

    def rows(ref, u):
        return ref[u * Cc:(u + 1) * Cc, :].astype(F32)

    blk = jnp.where(c < nS, c, c - nS)

    def rot(x, u):
        r0 = pl.multiple_of((blk * RET_SUB + u) * Cc, Cc)
        cos, sin = cos_ref[pl.ds(r0, Cc), :], sin_ref[pl.ds(r0, Cc), :]
        x1, x2 = x[:, :h2], x[:, h2:]
        return jnp.concatenate([x1 * cos - x2 * sin, x1 * sin + x2 * cos], axis=-1)

    @pl.when(c < nS)
    def _():
        for u in range(RET_SUB):
            kt = (rot(rows(k_ref, u), u) * (dh ** -0.5)).T
            v = v_ref[u * Cc:(u + 1) * Cc, :].astype(BF16)
            for d in range(2):
                st_ref[d, c * RET_SUB + u] = jnp.dot((kt * ks_ref[d]).astype(BF16), v, preferred_element_type=F32)

    @pl.when(c == nS)
    def _():
        for d in range(2):
            cd = cd_ref[d, 0:1, 0:1]
            carry_ref[...] = jnp.zeros_like(carry_ref)

            def body(i, carry, d=d, cd=cd):
                idx = i if d == 0 else nC - 1 - i
                t = st_ref[d, idx]
                st_ref[d, idx] = carry_ref[...]
                carry_ref[...] = carry_ref[...] * cd + t
                return carry

            lax.fori_loop(0, nC, body, 0)

    @pl.when(c >= nS)
    def _():
        for u in range(RET_SUB):
            ch = (c - nS) * RET_SUB + u
            qr = rot(rows(q_ref, u), u)
            kr = rot(rows(k_ref, u), u) * (dh ** -0.5)
            v = v_ref[u * Cc:(u + 1) * Cc, :].astype(BF16)
            s = lax.dot_general(qr.astype(BF16), kr.astype(BF16), (((1,), (1,)), ((), ())),
                                preferred_element_type=F32) * dm_ref[...]
            o = jnp.dot(s.astype(BF16), v, preferred_element_type=F32)
            for d in range(2):
                o = o + jnp.dot((qr * qs_ref[d]).astype(BF16), st_ref[d, ch].astype(BF16),
                                preferred_element_type=F32)
            mu = jnp.mean(o, axis=-1, keepdims=True)
            oc = o - mu
            var = jnp.mean(oc * oc, axis=-1, keepdims=True)
            g = rows(g_ref, u)
            o_ref[u * Cc:(u + 1) * Cc, :] = (oc * lax.rsqrt(var + GN_EPS)
                                             * (g * jax.nn.sigmoid(g))).astype(o_ref.dtype)


def retention_mixer(p, B, L, ret_decay):
    H, dh = RET_HEADS, RET_HEAD_DIM
    Cc = min(RET_CC, L)
    nC = L // Cc
    assert L % Cc == 0
    lg = -jnp.exp(ret_decay.astype(F32))
    pos = jnp.arange(Cc, dtype=F32)
    rel = pos[:, None] - pos[None, :]
    lf, lb = lg[0][:, None, None], lg[1][:, None, None]
    dm = jnp.where(rel >= 0, jnp.exp(jnp.maximum(rel, 0.0) * lf), jnp.exp(jnp.maximum(-rel, 0.0) * lb))
    qs = jnp.stack([jnp.exp((pos + 1.0)[None] * lg[0][:, None]),
                    jnp.exp((Cc - pos)[None] * lg[1][:, None])], axis=1)[..., None]
    ks = jnp.stack([jnp.exp((Cc - 1.0 - pos)[None] * lg[0][:, None]),
                    jnp.exp(pos[None] * lg[1][:, None])], axis=1)[:, :, None, :]
    cd = jnp.broadcast_to(jnp.exp(Cc * lg).T[:, :, None, None], (H, 2, SUBLANES, LANES))
    inv = ROPE_BASE ** (-jnp.arange(0, dh, 2, dtype=F32) / dh)
    ang = jnp.arange(L, dtype=F32)[:, None] * inv[None, :]
    cos, sin = jnp.cos(ang), jnp.sin(ang)

    assert nC % RET_SUB == 0
    nS = nC // RET_SUB
    rb = RET_SUB * Cc

    def kch(c):
        return jnp.where(c < nS, c, c - nS)

    def qch(c):
        return jnp.maximum(c - nS, 0)

    kv = lambda off: pl.BlockSpec((rb, dh), lambda b, h, c: (b * nS + kch(c), off * H + h))
    qo = lambda off: pl.BlockSpec((rb, dh), lambda b, h, c: (b * nS + qch(c), off * H + h))
    tab = pl.BlockSpec((L, dh // 2), lambda b, h, c: (0, 0), pipeline_mode=pl.Buffered(1))
    return pl.pallas_call(
        functools.partial(_ret_kernel, n_chunks=nC),
        out_shape=jax.ShapeDtypeStruct((B * L, H * dh), BF16),
        grid=(B, H, 2 * nS),
        in_specs=[qo(0), kv(1), kv(2), qo(3), tab, tab,
                  pl.BlockSpec((None, Cc, Cc), lambda b, h, c: (h, 0, 0)),
                  pl.BlockSpec((None, 2, Cc, 1), lambda b, h, c: (h, 0, 0, 0)),
                  pl.BlockSpec((None, 2, 1, Cc), lambda b, h, c: (h, 0, 0, 0)),
                  pl.BlockSpec((None, 2, SUBLANES, LANES), lambda b, h, c: (h, 0, 0, 0))],
        out_specs=qo(0),
        scratch_shapes=[pltpu.VMEM((2, nC, dh, dh), F32), pltpu.VMEM((dh, dh), F32)],
        compiler_params=_cp(("parallel", "parallel", "arbitrary")),
        name="retention",
    )(p, p, p, p, cos, sin, dm, qs, ks, cd)


def _na_bias_tables(rpb):
    Wc, WR, WC = GRID_W, NA_WIN_ROWS, NA_WIN_COLS
    hi = lax.Precision.HIGHEST
    rpb = rpb.astype(F32)
    c = jnp.arange(Wc)[:, None]
    kc = jnp.arange(Wc)[None, :]
    cs = jnp.clip(c - WC // 2, 0, Wc - WC)
    valid = (kc >= cs) & (kc < cs + WC)
    csel = (((kc - c + (WC - 1))[:, :, None] == jnp.arange(2 * WC - 1)) & valid[:, :, None]).astype(F32)
    dj = jnp.arange(WR)[None, :] - jnp.arange(WR)[:, None] + (WR - 1)
    rsel = (dj[:, :, None] == jnp.arange(2 * WR - 1)).astype(F32)
    t = jnp.einsum("hrs,cks->hrck", rpb, csel, precision=hi)
    t = jnp.einsum("djr,hrck->hdcjk", rsel, t, precision=hi)
    t = jnp.where(valid[None, None, :, None, :], t, NEG_INF)
    t = t.reshape(NA_HEADS // NA_HG, NA_HG, WR, Wc, WR * Wc).transpose(0, 2, 1, 3, 4)
    return t.reshape(NA_HEADS // NA_HG, WR, NA_HG * Wc, WR * Wc)


def _na_kernel(q_ref, k_ref, v_ref, b_ref, o_ref, *, rows):
    Wc, WR = GRID_W, NA_WIN_ROWS
    hw = NA_HG * NA_HEAD_DIM
    hq = NA_HG * Wc
    scale = NA_HEAD_DIM ** -0.5
    assert math.frexp(scale)[0] == 0.5
    rb = pl.program_id(2)
    own = (lax.broadcasted_iota(jnp.int32, (hq, hw), 0) // Wc
           == lax.broadcasted_iota(jnp.int32, (hq, hw), 1) // NA_HEAD_DIM)

    for i in range(NA_RB):
        r = rb * NA_RB + i
        rs = jnp.clip(r - WR // 2, 0, rows - WR)
        q = q_ref[i * Wc:(i + 1) * Wc, :] * jnp.asarray(scale, q_ref.dtype)
        qs = jnp.where(own, jnp.concatenate([q] * NA_HG, axis=0), jnp.zeros((), q.dtype))
        k0 = pl.multiple_of(rs * Wc, Wc)
        kw = k_ref[pl.ds(k0, WR * Wc), :]
        vw = v_ref[pl.ds(k0, WR * Wc), :]
        s = lax.dot_general(qs, kw, (((1,), (1,)), ((), ())), preferred_element_type=F32) + b_ref[r - rs]
        m = jnp.max(s, axis=-1, keepdims=True)
        e = jnp.exp(s - m)
        l = jnp.sum(e, axis=-1, keepdims=True)
        o = jnp.dot(e.astype(BF16), vw, preferred_element_type=F32) / l
        o = jnp.where(own, o, 0.0)
        out = o[0:Wc]
        for h in range(1, NA_HG):
            out = out + o[h * Wc:(h + 1) * Wc]
        o_ref[i * Wc:(i + 1) * Wc, :] = out.astype(o_ref.dtype)


def neighborhood_mixer(qkv, col0, B, L, rpb):
    Wc = GRID_W
    rows = L // Wc
    assert rows >= NA_WIN_ROWS and rows % NA_RB == 0
    hw = NA_HG * NA_HEAD_DIM
    nhg = NA_HEADS // NA_HG
    assert col0 % hw == 0
    c0 = col0 // hw
    bias = _na_bias_tables(rpb)
    nrb = rows // NA_RB
    return pl.pallas_call(
        functools.partial(_na_kernel, rows=rows),
        out_shape=jax.ShapeDtypeStruct((B * L, NA_WIDTH), BF16),
        grid=(B, nhg, nrb),
        in_specs=[pl.BlockSpec((NA_RB * Wc, hw), lambda b, g, r: (b * nrb + r, c0 + g)),
                  pl.BlockSpec((L, hw), lambda b, g, r: (b, c0 + nhg + g)),
                  pl.BlockSpec((L, hw), lambda b, g, r: (b, c0 + 2 * nhg + g)),
                  pl.BlockSpec((None, NA_WIN_ROWS, NA_HG * Wc, NA_WIN_ROWS * Wc), lambda b, g, r: (g, 0, 0, 0))],
        out_specs=pl.BlockSpec((NA_RB * Wc, hw), lambda b, g, r: (b * nrb + r, g)),
        compiler_params=_cp(("parallel", "parallel", "arbitrary")),
        name="neighborhood_attention",
    )(qkv, qkv, qkv, bias)


def _trunk(x, mem, B, L, prm, wb):
    n_mem = mem.shape[0] // B
    depth = prm["norm_g"].shape[0]
    hw3 = 3 * HY_WIDTH
    for layer in range(depth):
        i = layer // 2
        g = prm["norm_g"][layer]
        wo = wb["mix_wo"]
        if layer % 2 == 0:
            ident = jnp.zeros((3, S5_WIDTH), F32).at[1].set(1.0)
            cw = jnp.concatenate([prm["hy_short_w"][i].astype(F32), ident], axis=1)
            cb = jnp.concatenate([prm["hy_short_b"][i].astype(F32), jnp.zeros((S5_WIDTH,), F32)])
            uc = norm_matmul_conv(x, g[0], wb["ev_w_in"], cw, cb, L, layer=i)
            z = hyena_mixer(uc, B, L, prm["hy_w1"][i], prm["hy_b1"][i], prm["hy_freq"][i], prm["hy_w2"][i],
                            prm["hy_b2"][i], prm["hy_w3"][i], prm["hy_skip"][i])
            ss = s5_mixer(uc, hw3, B, L, prm["s5_a_re"][i], prm["s5_a_im"][i], prm["s5_log_dt"][i],
                          prm["s5_b_re"][i], prm["s5_b_im"][i], prm["s5_c_re"][i], prm["s5_c_im"][i],
                          prm["s5_d"][i], prm["s5_w_glu"][i])
            ops = [(z, wo, prm["hy_out_g"][i], 0), (ss, wo, None, HY_WIDTH)]
        else:
            p = norm_matmul(x, g[0], wb["od_w_in"], BF16, layer=i, name="odd_in_proj")
            ret = retention_mixer(p, B, L, prm["ret_decay"][i])
            na = neighborhood_mixer(p, 4 * RET_WIDTH, B, L, prm["na_rpb"][i])
            ops = [(ret, wo, None, 0), (na, wo, None, RET_WIDTH)]
        x, xn = matmul_norm_residual(ops, g[1], x, g[2], layer=layer, name="mix_out_proj")
        kv = norm_matmul(mem, prm["mem_norm_g"][layer], wb["xa_wkv"], BF16, layer=layer, name="xattn_kv_proj")
        x = xattn_block(xn, x, wb["xa_wq"], kv, wb["xa_wo"], g[3], B, L, n_mem, layer=layer)
        x = ffn_block(x, g[4], wb["ffn_wg"], wb["ffn_wu"], wb["ffn_wd"], g[5], layer=layer)
    return x


def kernel(x_prompt, x_sample, mem_prompt, mem_sample, norm_g, mix_wo, ev_w_in, hy_short_w, hy_short_b, hy_w1, hy_b1, hy_freq, hy_w2, hy_b2, hy_w3, hy_skip, hy_out_g, s5_a_re, s5_a_im, s5_log_dt, s5_b_re, s5_b_im, s5_c_re, s5_c_im, s5_d, s5_w_glu, od_w_in, ret_decay, na_rpb, mem_norm_g, xa_wq, xa_wkv, xa_wo, ffn_wg, ffn_wu, ffn_wd):
    prm = dict(norm_g=norm_g, hy_short_w=hy_short_w, hy_short_b=hy_short_b, hy_w1=hy_w1, hy_b1=hy_b1,
               hy_freq=hy_freq, hy_w2=hy_w2, hy_b2=hy_b2, hy_w3=hy_w3, hy_skip=hy_skip, hy_out_g=hy_out_g,
               s5_a_re=s5_a_re, s5_a_im=s5_a_im, s5_log_dt=s5_log_dt, s5_b_re=s5_b_re, s5_b_im=s5_b_im,
               s5_c_re=s5_c_re, s5_c_im=s5_c_im, s5_d=s5_d, s5_w_glu=s5_w_glu, ret_decay=ret_decay,
               na_rpb=na_rpb, mem_norm_g=mem_norm_g)
    wb = {k: v.astype(BF16) for k, v in dict(mix_wo=mix_wo, ev_w_in=ev_w_in, od_w_in=od_w_in, xa_wq=xa_wq,
                                             xa_wkv=xa_wkv, xa_wo=xa_wo, ffn_wg=ffn_wg, ffn_wu=ffn_wu,
                                             ffn_wd=ffn_wd).items()}
    outs = []
    for x, mem in ((x_prompt, mem_prompt), (x_sample, mem_sample)):
        B, L, D = x.shape
        y = _trunk(x.reshape(B * L, D), mem.reshape(-1, D), B, L, prm, wb)
        outs.append(y.reshape(B, L, D))
    return tuple(outs)
```

```python
import functools
import math

import jax
import jax.numpy as jnp
from jax import lax
from jax.experimental import pallas as pl
from jax.experimental.pallas import tpu as pltpu

F32 = jnp.float32
BF16 = jnp.bfloat16

V7X_VMEM_BYTES = 64 * 1024 * 1024
VMEM_LIMIT = V7X_VMEM_BYTES - 8 * 1024 * 1024
LANES = 128
SUBLANES = 8

D_MODEL = 2048
GRID_W = 64
HY_WIDTH = 3 * D_MODEL // 4
S5_WIDTH = D_MODEL - HY_WIDTH
S5_GROUP = 16
S5_GROUPS = S5_WIDTH // S5_GROUP
S5_STATE = 64
HY_ORDER = 2
HY_BANDS = 16
HY_FILTER_HIDDEN = 64
HY_DECAY_TARGET = 1e-2
HY_SHORT_DECAY_PCT = 0.3
HY_LONG_DECAY_PCT = 1.5
RET_WIDTH = D_MODEL // 2
RET_HEADS = 4
RET_HEAD_DIM = RET_WIDTH // RET_HEADS
ROPE_BASE = 10000.0
NA_WIDTH = D_MODEL - RET_WIDTH
NA_HEADS = 16
NA_HEAD_DIM = NA_WIDTH // NA_HEADS
NA_WIN_ROWS = 8
NA_WIN_COLS = 16
XA_HEADS = 4
XA_HEAD_DIM = D_MODEL // XA_HEADS
RMS_EPS = 1e-6
GN_EPS = 1e-6

FFT_N2 = 128
FFT_RB = SUBLANES
S5_T = 32
RET_CC = 256
RET_SUB = 4
FFN_SPLIT = 2
NA_HG = 4
NA_RB = 16
NEG_INF = -1e30


def _cp(sem, vmem=VMEM_LIMIT):
    return pltpu.CompilerParams(dimension_semantics=sem, vmem_limit_bytes=vmem)


def _rms(x, g, eps=RMS_EPS):
    return x * lax.rsqrt(jnp.mean(x * x, axis=-1, keepdims=True) + eps) * g


def _norm_mm_kernel(x_ref, g_ref, w_ref, o_ref, xn_ref):
    @pl.when(pl.program_id(1) == 0)
    def _():
        tm = x_ref.shape[0]
        rc = min(tm, 256)
        for r in range(0, tm, rc):
            xn_ref[r:r + rc, :] = _rms(x_ref[r:r + rc, :], g_ref[...]).astype(BF16)

    o_ref[...] = jnp.dot(xn_ref[...], w_ref[...], preferred_element_type=F32).astype(o_ref.dtype)


def _stacked(w):
    return w if w.ndim == 3 else w[None]


def norm_matmul(x, g, w, out_dtype, col0=0, ncols=None, layer=0, tm=1024, tn=1024, name="norm_matmul"):
    M, K = x.shape
    w = _stacked(w)
    N = w.shape[2] - col0 if ncols is None else ncols
    tm, tn = min(tm, M), min(tn, N)
    assert M % tm == 0 and N % tn == 0 and col0 % tn == 0
    cb0 = col0 // tn
    return pl.pallas_call(
        _norm_mm_kernel,
        out_shape=jax.ShapeDtypeStruct((M, N), out_dtype),
        grid=(M // tm, N // tn),
        in_specs=[pl.BlockSpec((tm, K), lambda i, j: (i, 0)),
                  pl.BlockSpec((1, K), lambda i, j: (0, 0)),
                  pl.BlockSpec((None, K, tn), lambda i, j: (layer, 0, cb0 + j))],
        out_specs=pl.BlockSpec((tm, tn), lambda i, j: (i, j)),
        scratch_shapes=[pltpu.VMEM((tm, K), BF16)],
        compiler_params=_cp(("parallel", "arbitrary")),
        name=name,
    )(x, g.reshape(1, K), w)


def _mm_norm_res_kernel(*refs, n_ops, prenorm):
    g_ref, x_ref, gn_ref, o_ref, on_ref = refs[-5:]
    tm = x_ref.shape[0]
    rc = tm // 2 if tm % 32 == 0 else tm
    for r in range(0, tm, rc):
        pos = 0
        y = None
        for t in range(n_ops):
            a = refs[pos][r:r + rc, :]
            w_ref = refs[pos + 1]
            pos += 2
            if prenorm[t]:
                a = _rms(a.astype(F32), refs[pos][...])
                pos += 1
            d = jnp.dot(a.astype(BF16), w_ref[...], preferred_element_type=F32)
            y = d if y is None else y + d
        x1 = x_ref[r:r + rc, :] + _rms(y, g_ref[...])
        o_ref[r:r + rc, :] = x1
        on_ref[r:r + rc, :] = _rms(x1, gn_ref[...]).astype(on_ref.dtype)


def matmul_norm_residual(ops, g, x, g_next, layer=0, tm=512, name="matmul_norm_residual"):
    M, N = x.shape
    tm = min(tm, M)
    assert M % tm == 0
    args, specs, prenorm = [], [], []
    for a, w, pg, k0 in ops:
        kt = a.shape[1]
        assert k0 % kt == 0
        args += [a, _stacked(w)]
        specs += [pl.BlockSpec((tm, kt), lambda i: (i, 0)),
                  pl.BlockSpec((None, kt, N), lambda i, kb=k0 // kt: (layer, kb, 0))]
        prenorm.append(pg is not None)
        if pg is not None:
            args.append(pg.reshape(1, kt))
            specs.append(pl.BlockSpec((1, kt), lambda i: (0, 0)))
    row = pl.BlockSpec((1, N), lambda i: (0, 0))
    tile = pl.BlockSpec((tm, N), lambda i: (i, 0))
    args += [g.reshape(1, N), x, g_next.reshape(1, N)]
    specs += [row, tile, row]
    return pl.pallas_call(
        functools.partial(_mm_norm_res_kernel, n_ops=len(ops), prenorm=tuple(prenorm)),
        out_shape=(jax.ShapeDtypeStruct((M, N), F32), jax.ShapeDtypeStruct((M, N), BF16)),
        grid=(M // tm,),
        in_specs=specs,
        out_specs=(tile, tile),
        compiler_params=_cp(("parallel",)),
        name=name,
    )(*args)


def _xattn_block_kernel(xn_ref, x_ref, wq_ref, k_ref, v_ref, wo_ref, g_ref, o_ref, *, heads):
    dh = xn_ref.shape[1] // heads
    scale = dh ** -0.5
    q = jnp.dot(xn_ref[...], wq_ref[...], preferred_element_type=F32).astype(BF16)
    outs = []
    for h in range(heads):
        sl = slice(h * dh, (h + 1) * dh)
        s = lax.dot_general(q[:, sl], k_ref[:, sl], (((1,), (1,)), ((), ())), preferred_element_type=F32) * scale
        m = jnp.max(s, axis=-1, keepdims=True)
        p = jnp.exp(s - m)
        l = jnp.sum(p, axis=-1, keepdims=True)
        outs.append((jnp.dot(p.astype(BF16), v_ref[:, sl], preferred_element_type=F32) / l).astype(BF16))
    o = jnp.concatenate(outs, axis=-1)
    y = jnp.dot(o, wo_ref[...], preferred_element_type=F32)
    o_ref[...] = x_ref[...] + _rms(y, g_ref[...])


def xattn_block(xn, x, wq, kv, wo, g, B, L, n_mem, layer=0, tm=512):
    M, D = x.shape
    tm = min(tm, L)
    assert L % tm == 0
    bpl = L // tm
    wq, wo = _stacked(wq), _stacked(wo)
    wspec = pl.BlockSpec((None, D, D), lambda i: (layer, 0, 0), pipeline_mode=pl.Buffered(1))
    return pl.pallas_call(
        functools.partial(_xattn_block_kernel, heads=XA_HEADS),
        out_shape=jax.ShapeDtypeStruct((M, D), F32),
        grid=(M // tm,),
        in_specs=[pl.BlockSpec((tm, D), lambda i: (i, 0)),
                  pl.BlockSpec((tm, D), lambda i: (i, 0)),
                  wspec,
                  pl.BlockSpec((n_mem, D), lambda i: (i // bpl, 0)),
                  pl.BlockSpec((n_mem, D), lambda i: (i // bpl, 1)),
                  wspec,
                  pl.BlockSpec((1, D), lambda i: (0, 0))],
        out_specs=pl.BlockSpec((tm, D), lambda i: (i, 0)),
        compiler_params=_cp(("parallel",)),
        name="xattn_block",
    )(xn, x, wq, kv, kv, wo, g.reshape(1, D))


def _ffn_kernel(x_ref, gi_ref, wg_ref, wu_ref, wd_ref, go_ref, o_ref, xn_ref):
    j = pl.program_id(1)

    tm = x_ref.shape[0]
    rc = min(tm, 256)

    @pl.when(j == 0)
    def _():
        for r in range(0, tm, rc):
            xn_ref[r:r + rc, :] = _rms(x_ref[r:r + rc, :], gi_ref[...]).astype(BF16)
        o_ref[...] = jnp.zeros_like(o_ref)

    th = wg_ref.shape[1]
    hh = th // FFN_SPLIT
    for c in range(FFN_SPLIT):
        xn = xn_ref[...]
        a = jnp.dot(xn, wg_ref[:, c * hh:(c + 1) * hh], preferred_element_type=F32)
        u = jnp.dot(xn, wu_ref[:, c * hh:(c + 1) * hh], preferred_element_type=F32)
        h = (a * jax.nn.sigmoid(a) * u).astype(BF16)
        o_ref[...] += jnp.dot(h, wd_ref[c * hh:(c + 1) * hh, :], preferred_element_type=F32)

    @pl.when(j == pl.num_programs(1) - 1)
    def _():
        for r in range(0, tm, SUBLANES):
            o_ref[r:r + SUBLANES, :] = (x_ref[r:r + SUBLANES, :]
                                        + _rms(o_ref[r:r + SUBLANES, :], go_ref[...]))


def ffn_block(x, g_in, wg, wu, wd, g_out, layer=0, tm=1024, th=512):
    M, D = x.shape
    wg, wu, wd = _stacked(wg), _stacked(wu), _stacked(wd)
    Hd = wg.shape[2]
    tm, th = min(tm, M), min(th, Hd)
    assert M % tm == 0 and Hd % th == 0
    return pl.pallas_call(
        _ffn_kernel,
        out_shape=jax.ShapeDtypeStruct((M, D), F32),
        grid=(M // tm, Hd // th),
        in_specs=[pl.BlockSpec((tm, D), lambda i, j: (i, 0)),
                  pl.BlockSpec((1, D), lambda i, j: (0, 0)),
                  pl.BlockSpec((None, D, th), lambda i, j: (layer, 0, j)),
                  pl.BlockSpec((None, D, th), lambda i, j: (layer, 0, j)),
                  pl.BlockSpec((None, th, D), lambda i, j: (layer, j, 0)),
                  pl.BlockSpec((1, D), lambda i, j: (0, 0))],
        out_specs=pl.BlockSpec((tm, D), lambda i, j: (i, 0)),
        scratch_shapes=[pltpu.VMEM((tm, D), BF16)],
        compiler_params=_cp(("parallel", "arbitrary")),
        name="ffn_block",
    )(x, g_in.reshape(1, D), wg, wu, wd, g_out.reshape(1, D))


HALO = 16


def _norm_mm_conv_kernel(xp_ref, xc_ref, xn_ref, g_ref, w_ref, cw_ref, cb_ref, o_ref, xs_ref, *, blocks_per_seq):
    tm = xc_ref.shape[0]

    @pl.when(pl.program_id(1) == 0)
    def _():
        li = pl.program_id(0) % blocks_per_seq
        g = g_ref[...]
        keep_prev = jnp.where(li == 0, 0.0, 1.0)
        keep_next = jnp.where(li == blocks_per_seq - 1, 0.0, 1.0)
        xs_ref[0:HALO] = (_rms(xp_ref[...], g) * keep_prev).astype(BF16)
        rc = min(tm, 256)
        for r in range(0, tm, rc):
            xs_ref[HALO + r:HALO + r + rc] = _rms(xc_ref[r:r + rc, :], g).astype(BF16)
        xs_ref[HALO + tm:] = (_rms(xn_ref[...], g) * keep_next).astype(BF16)

    n = tm + 2 * HALO
    tn = o_ref.shape[1]
    hw = tn // 2 if tn % (2 * LANES) == 0 else tn
    for c in range(0, tn, hw):
        cs = slice(c, c + hw)
        y = jnp.dot(xs_ref[...], w_ref[:, cs], preferred_element_type=F32)
        up = pltpu.roll(y, 1, axis=0)[HALO:HALO + tm]
        dn = pltpu.roll(y, n - 1, axis=0)[HALO:HALO + tm]
        o_ref[:, cs] = (up * cw_ref[0:1, cs] + y[HALO:HALO + tm] * cw_ref[1:2, cs] + dn * cw_ref[2:3, cs]
                        + cb_ref[:, cs])


def norm_matmul_conv(x, g, w, cw, cb, L, layer=0, tm=1024, tn=1024):
    M, K = x.shape
    w = _stacked(w)
    N = w.shape[2]
    tm, tn = min(tm, L), min(tn, N)
    assert L % tm == 0 and N % tn == 0 and tm % HALO == 0
    hb = tm // HALO
    nhb = M // HALO
    return pl.pallas_call(
        functools.partial(_norm_mm_conv_kernel, blocks_per_seq=L // tm),
        out_shape=jax.ShapeDtypeStruct((M, N), F32),
        grid=(M // tm, N // tn),
        in_specs=[pl.BlockSpec((HALO, K), lambda i, j: (jnp.maximum(i * hb - 1, 0), 0)),
                  pl.BlockSpec((tm, K), lambda i, j: (i, 0)),
                  pl.BlockSpec((HALO, K), lambda i, j: (jnp.minimum((i + 1) * hb, nhb - 1), 0)),
                  pl.BlockSpec((1, K), lambda i, j: (0, 0)),
                  pl.BlockSpec((None, K, tn), lambda i, j: (layer, 0, j)),
                  pl.BlockSpec((3, tn), lambda i, j: (0, j)),
                  pl.BlockSpec((1, tn), lambda i, j: (0, j))],
        out_specs=pl.BlockSpec((tm, tn), lambda i, j: (i, j)),
        scratch_shapes=[pltpu.VMEM((tm + 2 * HALO, K), BF16)],
        compiler_params=_cp(("parallel", "arbitrary")),
        name="even_in_proj_conv",
    )(x, x, x, g.reshape(1, K), w, cw, cb.reshape(1, N))


def _fft_tables(L):
    N2 = FFT_N2
    N = 2 * L
    N1 = N // N2
    N1h = N1 // 2
    n2 = jnp.arange(N2, dtype=jnp.int32)[:, None, None]
    k1 = jnp.arange(N1, dtype=jnp.int32)[None, :, None]
    n1 = jnp.arange(N1h, dtype=jnp.int32)[None, None, :]
    ph = ((n1 * N2 + n2) * k1) % N
    ang = ph.astype(F32) * (2.0 * math.pi / N)
    gr, gi = jnp.cos(ang), -jnp.sin(ang)
    g_fwd = jnp.concatenate([jnp.concatenate([gr, -gi], 2), jnp.concatenate([gi, gr], 2)], 1).astype(BF16)
    sgn = jnp.where(k1 % 2 == 0, 1.0, -1.0).astype(F32)
    g_flt = jnp.concatenate([jnp.concatenate([gr, sgn * gr], 2), jnp.concatenate([gi, sgn * gi], 2)], 1).astype(BF16)
    er = jnp.swapaxes(gr, 1, 2) / N
    ei = -jnp.swapaxes(gi, 1, 2) / N
    g_inv = jnp.concatenate([jnp.concatenate([er, -ei], 2), jnp.concatenate([ei, er], 2)], 1).astype(BF16)
    a2 = (jnp.arange(N2, dtype=jnp.int32)[:, None] * jnp.arange(N2, dtype=jnp.int32)[None, :]) % N2
    ang2 = a2.astype(F32) * (2.0 * math.pi / N2)
    fr, fi = jnp.cos(ang2), -jnp.sin(ang2)
    f_mid = jnp.concatenate([jnp.concatenate([fr, -fi], 1), jnp.concatenate([fi, fr], 1)], 0).astype(BF16)
    f_mid_inv = jnp.concatenate([jnp.concatenate([fr, fi], 1), jnp.concatenate([-fi, fr], 1)], 0).astype(BF16)
    return dict(g_fwd=g_fwd, g_flt=g_flt, g_inv=g_inv, f_mid=f_mid, f_mid_inv=f_mid_inv, N1=N1)


def _pack_complex(re, im):
    r = lax.bitcast_convert_type(re.astype(BF16).astype(F32), jnp.uint32)
    i = lax.bitcast_convert_type(im.astype(BF16).astype(F32), jnp.uint32)
    return r | (i >> 16)


def _unpack_complex(w):
    re = lax.bitcast_convert_type(w & jnp.uint32(0xFFFF0000), F32)
    im = lax.bitcast_convert_type(w << 16, F32)
    return re, im


def _to_slabs(src_ref, slab_ref, lead=()):
    ns, rows, _ = slab_ref.shape
    for s in range(ns):
        slab_ref[s] = src_ref[lead + (slice(None), slice(None), slice(s * LANES, (s + 1) * LANES))].reshape(
            rows, LANES)


def _slab_rows(slab_ref, r, n):
    return jnp.concatenate([slab_ref.at[s][pl.ds(r, n, stride=FFT_RB), :] for s in range(slab_ref.shape[0])],
                           axis=1)


def _set_slab_rows(slab_ref, r, val):
    n = val.shape[0]
    for s in range(slab_ref.shape[0]):
        slab_ref.at[s][pl.ds(r, n, stride=FFT_RB), :] = val[:, s * LANES:(s + 1) * LANES]


def _from_slabs(slab_ref, dst_ref, lead=()):
    ns, rows, _ = slab_ref.shape
    for s in range(ns):
        dst_ref[lead + (slice(None), slice(None), slice(s * LANES, (s + 1) * LANES))] = (
            slab_ref[s].reshape(rows // FFT_RB, FFT_RB, LANES))


def _fft_a_kernel(x_ref, g_ref, z_ref, xa_ref, xb_ref, zs_ref):
    n1 = z_ref.shape[0]
    _to_slabs(x_ref, xa_ref, (0,))
    _to_slabs(x_ref, xb_ref, (1,))
    for r in range(FFT_RB):
        xs = jnp.concatenate([_slab_rows(xa_ref, r, n1 // 2), _slab_rows(xb_ref, r, n1 // 2)], axis=0).astype(BF16)
        a = jnp.dot(g_ref[r], xs, preferred_element_type=F32)
        _set_slab_rows(zs_ref, r, _pack_complex(a[:n1], a[n1:]))
    _from_slabs(zs_ref, z_ref)


def fft_stage_a(x4, col0, C, g, W=512):
    B, N1h, N2, _ = x4.shape
    N1 = 2 * N1h
    W = min(W, C)
    assert B % 2 == 0 and C % W == 0 and col0 % W == 0 and N2 % FFT_RB == 0
    cb0 = col0 // W
    return pl.pallas_call(
        _fft_a_kernel,
        out_shape=jax.ShapeDtypeStruct((B // 2, N1, N2, C), jnp.uint32),
        grid=(B // 2, N2 // FFT_RB, C // W),
        in_specs=[pl.BlockSpec((2, N1h, FFT_RB, W), lambda p, j, c: (p, 0, j, cb0 + c)),
                  pl.BlockSpec((FFT_RB, 2 * N1, N1), lambda p, j, c: (j, 0, 0))],
        out_specs=pl.BlockSpec((None, N1, FFT_RB, W), lambda p, j, c: (p, 0, j, c)),
        scratch_shapes=[pltpu.VMEM((W // LANES, N1h * FFT_RB, LANES), F32),
                        pltpu.VMEM((W // LANES, N1h * FFT_RB, LANES), F32),
                        pltpu.VMEM((W // LANES, N1 * FFT_RB, LANES), jnp.uint32)],
        compiler_params=_cp(("parallel", "parallel", "parallel")),
        name="hyena_fft_a",
    )(x4, g)


def _fft_mid_kernel(*refs, kb, with_filter):
    if with_filter:
        z_ref, h_ref, f_ref, fi_ref, y_ref = refs
    else:
        z_ref, f_ref, y_ref = refs
    n2 = z_ref.shape[1]
    for k in range(kb):
        zr, zi = _unpack_complex(z_ref[k])
        zs = jnp.concatenate([zr, zi], axis=0).astype(BF16)
        x = jnp.dot(f_ref[...], zs, preferred_element_type=F32)
        if with_filter:
            xr, xi = x[:n2], x[n2:]
            hr, hi = _unpack_complex(h_ref[k])
            ys = jnp.concatenate([xr * hr - xi * hi, xr * hi + xi * hr], axis=0).astype(BF16)
            x = jnp.dot(fi_ref[...], ys, preferred_element_type=F32)
        y_ref[k] = _pack_complex(x[:n2], x[n2:])


def fft_stage_mid(z, tabs, h=None, order=0, kb=16, W=512):
    P, N1, N2, C = z.shape
    kb, W = min(kb, N1), min(W, C)
    assert N1 % kb == 0 and C % W == 0
    blk = pl.BlockSpec((None, kb, N2, W), lambda k, c, p: (p, k, 0, c))
    mat = pl.BlockSpec((2 * N2, 2 * N2), lambda k, c, p: (0, 0))
    if h is None:
        args, specs = (z, tabs["f_mid"]), [blk, mat]
    else:
        hblk = pl.BlockSpec((None, kb, N2, W), lambda k, c, p: (order, k, 0, c))
        args, specs = (z, h, tabs["f_mid"], tabs["f_mid_inv"]), [blk, hblk, mat, mat]
    return pl.pallas_call(
        functools.partial(_fft_mid_kernel, kb=kb, with_filter=h is not None),
        out_shape=jax.ShapeDtypeStruct(z.shape, jnp.uint32),
        grid=(N1 // kb, C // W, P),
        in_specs=specs,
        out_specs=blk,
        compiler_params=_cp(("parallel", "parallel", "parallel")),
        name="hyena_fft_mid",
    )(*args)


def _fft_c_kernel(y_ref, g_ref, x_ref, o_ref, ys_ref, xa_ref, xb_ref):
    n1h = x_ref.shape[1]
    _to_slabs(y_ref, ys_ref)
    _to_slabs(x_ref, xa_ref, (0,))
    _to_slabs(x_ref, xb_ref, (1,))
    for r in range(FFT_RB):
        yr, yi = _unpack_complex(_slab_rows(ys_ref, r, 2 * n1h))
        ys = jnp.concatenate([yr, yi], axis=0).astype(BF16)
        c = jnp.dot(g_ref[r], ys, preferred_element_type=F32)
        _set_slab_rows(xa_ref, r, _slab_rows(xa_ref, r, n1h) * c[:n1h])
        _set_slab_rows(xb_ref, r, _slab_rows(xb_ref, r, n1h) * c[n1h:])
    _from_slabs(xa_ref, o_ref, (0,))
    _from_slabs(xb_ref, o_ref, (1,))


def fft_stage_c(y, g, x4, xcol0, W=512):
    P, N1, N2, C = y.shape
    N1h = N1 // 2
    W = min(W, C)
    assert C % W == 0 and xcol0 % W == 0
    xb = xcol0 // W
    slab = lambda n, dt: pltpu.VMEM((W // LANES, n * FFT_RB, LANES), dt)
    return pl.pallas_call(
        _fft_c_kernel,
        out_shape=jax.ShapeDtypeStruct((2 * P, N1h, N2, C), F32),
        grid=(P, N2 // FFT_RB, C // W),
        in_specs=[pl.BlockSpec((None, N1, FFT_RB, W), lambda p, j, c: (p, 0, j, c)),
                  pl.BlockSpec((FFT_RB, N1, 2 * N1), lambda p, j, c: (j, 0, 0)),
                  pl.BlockSpec((2, N1h, FFT_RB, W), lambda p, j, c: (p, 0, j, xb + c))],
        out_specs=pl.BlockSpec((2, N1h, FFT_RB, W), lambda p, j, c: (p, 0, j, c)),
        scratch_shapes=[slab(N1, jnp.uint32), slab(N1h, F32), slab(N1h, F32)],
        compiler_params=_cp(("parallel", "parallel", "parallel")),
        name="hyena_fft_c",
    )(y, g, x4)


def _fft_ca_kernel(y_ref, gi_ref, x_ref, gf_ref, z_ref, ys_ref, xa_ref, xb_ref, zs_ref):
    n1h = x_ref.shape[1]
    n1 = 2 * n1h
    _to_slabs(y_ref, ys_ref)
    _to_slabs(x_ref, xa_ref, (0,))
    _to_slabs(x_ref, xb_ref, (1,))
    for r in range(FFT_RB):
        yr, yi = _unpack_complex(_slab_rows(ys_ref, r, n1))
        c = jnp.dot(gi_ref[r], jnp.concatenate([yr, yi], axis=0).astype(BF16), preferred_element_type=F32)
        gated = jnp.concatenate([_slab_rows(xa_ref, r, n1h) * c[:n1h], _slab_rows(xb_ref, r, n1h) * c[n1h:]], axis=0)
        a = jnp.dot(gf_ref[r], gated.astype(BF16), preferred_element_type=F32)
        _set_slab_rows(zs_ref, r, _pack_complex(a[:n1], a[n1:]))
    _from_slabs(zs_ref, z_ref)


def fft_stage_ca(y, g_inv, x4, xcol0, g_fwd, W=512):
    P, N1, N2, C = y.shape
    N1h = N1 // 2
    W = min(W, C)
    assert C % W == 0 and xcol0 % W == 0
    xb = xcol0 // W
    slab = lambda n, dt: pltpu.VMEM((W // LANES, n * FFT_RB, LANES), dt)
    zblk = pl.BlockSpec((None, N1, FFT_RB, W), lambda p, j, c: (p, 0, j, c))
    return pl.pallas_call(
        _fft_ca_kernel,
        out_shape=jax.ShapeDtypeStruct((P, N1, N2, C), jnp.uint32),
        grid=(P, N2 // FFT_RB, C // W),
        in_specs=[zblk,
                  pl.BlockSpec((FFT_RB, N1, 2 * N1), lambda p, j, c: (j, 0, 0)),
                  pl.BlockSpec((2, N1h, FFT_RB, W), lambda p, j, c: (p, 0, j, xb + c)),
                  pl.BlockSpec((FFT_RB, 2 * N1, N1), lambda p, j, c: (j, 0, 0))],
        out_specs=zblk,
        scratch_shapes=[slab(N1, jnp.uint32), slab(N1h, F32), slab(N1h, F32), slab(N1, jnp.uint32)],
        compiler_params=_cp(("parallel", "parallel", "parallel")),
        name="hyena_fft_ca",
    )(y, g_inv, x4, g_fwd)


def _filter_a_kernel(bands_ref, w1t_ref, w1c_ref, w1s_ref, b1_ref, f_ref, w2_ref, b2_ref, wfh_ref, wfl_ref, wbh_ref,
                     wbl_ref, dl_ref, sk_ref, g_ref, z_ref, hid_ref, flt_ref, zs_ref, *, L):
    hi = lax.Precision.HIGHEST
    j = pl.program_id(1)
    n1 = z_ref.shape[0]
    n1h = n1 // 2
    rows = FFT_RB * n1h

    def positions(shape, axis):
        q = lax.broadcasted_iota(jnp.int32, shape, axis)
        return (q % n1h) * FFT_N2 + j * FFT_RB + q // n1h

    @pl.when(pl.program_id(2) == 0)
    def _():
        m_row = positions((1, rows), 1)
        for d, pos in enumerate((m_row, L - m_row)):
            t = pos.astype(F32) * (1.0 / L)
            ang = (2.0 * math.pi) * bands_ref[...] * t
            pre = (w1t_ref[...] * t
                   + jnp.dot(w1c_ref[...], jnp.cos(ang), preferred_element_type=F32, precision=hi)
                   + jnp.dot(w1s_ref[...], jnp.sin(ang), preferred_element_type=F32, precision=hi)
                   + b1_ref[...])
            h = jnp.sin(f_ref[0] * pre)
            h = jnp.sin(f_ref[1] * (jnp.dot(w2_ref[...], h, preferred_element_type=F32, precision=hi) + b2_ref[...]))
            h = h.T
            h_hi = h.astype(BF16)
            hid_ref[d, 0] = h_hi
            hid_ref[d, 1] = (h - h_hi.astype(F32)).astype(BF16)

    m = positions((rows, 1), 0)
    for d, (wh_ref, wl_ref) in enumerate(((wfh_ref, wfl_ref), (wbh_ref, wbl_ref))):
        t = (m if d == 0 else L - m).astype(F32) * (1.0 / L)
        out = (jnp.dot(hid_ref[d, 0], wh_ref[...], preferred_element_type=F32)
               + jnp.dot(hid_ref[d, 0], wl_ref[...], preferred_element_type=F32)
               + jnp.dot(hid_ref[d, 1], wh_ref[...], preferred_element_type=F32))
        flt_ref[d] = out * jnp.exp(-t * dl_ref[...])

    @pl.when(j == 0)
    def _():
        flt_ref[0, 0:1, :] = flt_ref[0, 0:1, :] + sk_ref[...]
        flt_ref[1, 0:1, :] = jnp.zeros_like(sk_ref)

    for r in range(FFT_RB):
        xs = jnp.concatenate([flt_ref[0, r * n1h:(r + 1) * n1h, :], flt_ref[1, r * n1h:(r + 1) * n1h, :]],
                             axis=0).astype(BF16)
        a = jnp.dot(g_ref[r], xs, preferred_element_type=F32)
        _set_slab_rows(zs_ref, r, _pack_complex(a[:n1], a[n1:]))
    _from_slabs(zs_ref, z_ref)


def filter_stage_a(w1, b1, freq, w2, b2, w3, skip, L, g_flt, W=512):
    C, Hh, nb, N2 = HY_WIDTH, HY_FILTER_HIDDEN, HY_BANDS, FFT_N2
    N1h = L // N2
    N1 = 2 * N1h
    col = lambda v: v.astype(F32).reshape(-1, 1)
    w1 = w1.astype(F32)
    w3_hi = w3.astype(BF16)
    w3_lo = (w3.astype(F32) - w3_hi.astype(F32)).astype(BF16)
    deltas = jnp.abs(jnp.linspace(math.log(HY_DECAY_TARGET) / HY_LONG_DECAY_PCT,
                                  math.log(HY_DECAY_TARGET) / HY_SHORT_DECAY_PCT, C, dtype=F32)).reshape(1, C)
    ncb = C // W
    small = lambda *shape: pl.BlockSpec(shape, lambda o, j, c: (0,) * len(shape))
    w3spec = lambda d: pl.BlockSpec((Hh, W), lambda o, j, c: (0, (2 * o + d) * ncb + c))
    rows = FFT_RB * N1h
    return pl.pallas_call(
        functools.partial(_filter_a_kernel, L=L),
        out_shape=jax.ShapeDtypeStruct((HY_ORDER, N1, N2, C), jnp.uint32),
        grid=(HY_ORDER, N2 // FFT_RB, ncb),
        in_specs=[small(nb, 1), small(Hh, 1), small(Hh, nb), small(Hh, nb), small(Hh, 1),
                  small(2, Hh, 1), small(Hh, Hh), small(Hh, 1),
                  w3spec(0), w3spec(0), w3spec(1), w3spec(1),
                  pl.BlockSpec((1, W), lambda o, j, c: (0, c)),
                  pl.BlockSpec((None, 1, W), lambda o, j, c: (o, 0, c)),
                  pl.BlockSpec((FFT_RB, 2 * N1, N1), lambda o, j, c: (j, 0, 0))],
        out_specs=pl.BlockSpec((None, N1, FFT_RB, W), lambda o, j, c: (o, 0, j, c)),
        scratch_shapes=[pltpu.VMEM((2, 2, rows, Hh), BF16), pltpu.VMEM((2, rows, W), F32),
                        pltpu.VMEM((W // LANES, N1 * FFT_RB, LANES), jnp.uint32)],
        compiler_params=_cp(("parallel", "parallel", "arbitrary")),
        name="hyena_filter_fft_a",
    )(col(jnp.arange(1, nb + 1, dtype=F32)), col(w1[0]), w1[1:1 + nb].T, w1[1 + nb:].T, col(b1),
      freq.astype(F32)[:, :, None], w2.astype(F32).T, col(b2), w3_hi, w3_lo, w3_hi, w3_lo, deltas,
      skip.astype(F32).reshape(HY_ORDER, 1, C), g_flt)


def hyena_mixer(uc, B, L, w1, b1, freq, w2, b2, w3, skip):
    C = HY_WIDTH
    N2 = FFT_N2
    N1h = L // N2
    tabs = _fft_tables(L)
    uc4 = uc.reshape(B, N1h, N2, uc.shape[1])
    spec = fft_stage_mid(filter_stage_a(w1, b1, freq, w2, b2, w3, skip, L, tabs["g_flt"]), tabs)
    y = fft_stage_mid(fft_stage_a(uc4, 0, C, tabs["g_fwd"]), tabs, h=spec, order=0)
    zz = fft_stage_ca(y, tabs["g_inv"], uc4, C, tabs["g_fwd"])
    y = fft_stage_mid(zz, tabs, h=spec, order=1)
    z = fft_stage_c(y, tabs["g_inv"], uc4, 2 * C)
    return z.reshape(B * L, C)


def _s5_tables(a_re, a_im, log_dt, b_re, b_im, c_re, c_im, n_chunks):
    T = S5_T
    G, P, I = S5_GROUPS, S5_STATE, S5_GROUP
    f32 = lambda a: a.astype(F32)
    a_re, a_im, b_re, b_im, c_re, c_im = map(f32, (a_re, a_im, b_re, b_im, c_re, c_im))
    step = jnp.exp(f32(log_dt))[..., None]
    lr, li = a_re * step, a_im * step
    mag = jnp.exp(lr)
    br_, bi_ = mag * jnp.cos(li), mag * jnp.sin(li)
    den = a_re * a_re + a_im * a_im
    qr = ((br_ - 1.0) * a_re + bi_ * a_im) / den
    qi = (bi_ * a_re - (br_ - 1.0) * a_im) / den
    bbr = qr[..., None] * b_re - qi[..., None] * b_im
    bbi = qr[..., None] * b_im + qi[..., None] * b_re

    def lam_pow(k):
        k = k.astype(F32)
        m = jnp.exp(lr[..., None] * k)
        return m * jnp.cos(li[..., None] * k), m * jnp.sin(li[..., None] * k)

    lags = jnp.arange(T + 1)
    pr, pi_ = lam_pow(lags)
    cpr = c_re[..., None] * pr[:, :, None] - c_im[..., None] * pi_[:, :, None]
    cpi = c_re[..., None] * pi_[:, :, None] + c_im[..., None] * pr[:, :, None]
    kern = jnp.einsum("dgjpk,dgpi->dgkji", cpr, bbr) - jnp.einsum("dgjpk,dgpi->dgkji", cpi, bbi)
    tt = jnp.arange(T)
    lag = tt[:, None] - tt[None, :]
    sel_f = (lag[:, :, None] == lags).astype(F32)
    sel_b = (-lag[:, :, None] == lags).astype(F32)
    toep = (jnp.einsum("tuk,gkji->gtjui", sel_f, kern[0], precision=lax.Precision.HIGHEST)
            + jnp.einsum("tuk,gkji->gtjui", sel_b, kern[1], precision=lax.Precision.HIGHEST))
    toep = toep.reshape(G, T * I, T * I)
    ef = T - 1 - tt
    eb = tt

    def bst(d, e):
        wr, wi = pr[d][..., e], pi_[d][..., e]
        re = wr[..., None] * bbr[d][:, :, None] - wi[..., None] * bbi[d][:, :, None]
        im = wr[..., None] * bbi[d][:, :, None] + wi[..., None] * bbr[d][:, :, None]
        return jnp.concatenate([re, im], axis=1).reshape(G, 2 * P, T * I)

    bst_all = jnp.stack([bst(0, ef), bst(1, eb)], axis=1)

    def cst(d, e):
        xr, xi = cpr[d][..., e], cpi[d][..., e]
        m = jnp.concatenate([xr, -xi], axis=2)
        return m.transpose(0, 3, 1, 2).reshape(G, T * I, 2 * P)

    cst_all = jnp.stack([cst(0, tt + 1), cst(1, T - tt)], axis=1)
    nsteps = max(1, int(math.log2(n_chunks)))
    e2 = T * (2 ** jnp.arange(nsteps))
    ar, ai = lam_pow(e2)
    ap = jnp.stack([ar, ai], axis=-1).transpose(1, 0, 3, 4, 2)
    ap = jnp.broadcast_to(ap[..., None], ap.shape + (LANES,))
    return toep.astype(BF16), bst_all.astype(BF16), cst_all.astype(BF16), ap, nsteps


def _s5_kernel(u_ref, toep_ref, bst_ref, cst_ref, ap_ref, y_ref, *, n_chunks, nsteps):
    T, I, cols = u_ref.shape
    P = S5_STATE
    u = u_ref[...].reshape(T * I, cols).astype(BF16)
    y = jnp.dot(toep_ref[...], u, preferred_element_type=F32)
    cidx = lax.broadcasted_iota(jnp.int32, (P, cols), 1) % n_chunks

    def shifted(x, sh, d):
        if d == 0:
            return jnp.where(cidx >= sh, pltpu.roll(x, sh, axis=1), 0.0)
        return jnp.where(cidx < n_chunks - sh, pltpu.roll(x, cols - sh, axis=1), 0.0)

    for d in range(2):
        v = jnp.dot(bst_ref[d], u, preferred_element_type=F32)
        sr, si = v[:P], v[P:]
        for j in range(nsteps):
            if (1 << j) >= n_chunks:
                break
            ar, ai = ap_ref[d, j, 0][:, :1], ap_ref[d, j, 1][:, :1]
            rr, ri = shifted(sr, 1 << j, d), shifted(si, 1 << j, d)
            sr, si = sr + ar * rr - ai * ri, si + ar * ri + ai * rr
        s_in = jnp.concatenate([shifted(sr, 1, d), shifted(si, 1, d)], axis=0).astype(BF16)
        y = y + jnp.dot(cst_ref[d], s_in, preferred_element_type=F32)
    y_ref[...] = y.reshape(T, I, cols)


def _s5_out_kernel(y_ref, u_ref, d_ref, w_ref, o_ref):
    y = y_ref[...] + d_ref[...] * u_ref[...]
    y = jax.nn.gelu(y, approximate=True)
    z = jnp.dot(y.astype(BF16), w_ref[...], preferred_element_type=F32)
    o_ref[...] = (y * jax.nn.sigmoid(z)).astype(o_ref.dtype)


def s5_mixer(p, col0, B, L, a_re, a_im, log_dt, b_re, b_im, c_re, c_im, d, w_glu, tm=1024):
    T, Wd, I, G = S5_T, S5_WIDTH, S5_GROUP, S5_GROUPS
    nC = L // T
    cols = B * nC
    assert L % T == 0 and nC & (nC - 1) == 0 and cols % LANES == 0
    toep, bst, cst, ap, nsteps = _s5_tables(a_re, a_im, log_dt, b_re, b_im, c_re, c_im, nC)
    u = p[:, col0:col0 + Wd]
    ut = u.reshape(B, nC, T, Wd).transpose(2, 3, 0, 1).reshape(T, Wd, cols)
    blk = pl.BlockSpec((T, I, cols), lambda g: (0, g, 0))
    yt = pl.pallas_call(
        functools.partial(_s5_kernel, n_chunks=nC, nsteps=nsteps),
        out_shape=jax.ShapeDtypeStruct((T, Wd, cols), F32),
        grid=(G,),
        in_specs=[blk,
                  pl.BlockSpec((None, T * I, T * I), lambda g: (g, 0, 0)),
                  pl.BlockSpec((None, 2, 2 * S5_STATE, T * I), lambda g: (g, 0, 0, 0)),
                  pl.BlockSpec((None, 2, T * I, 2 * S5_STATE), lambda g: (g, 0, 0, 0)),
                  pl.BlockSpec((None, 2, nsteps, 2, S5_STATE, LANES), lambda g: (g, 0, 0, 0, 0, 0))],
        out_specs=blk,
        compiler_params=_cp(("parallel",)),
        name="s5_scan",
    )(ut, toep, bst, cst, ap)
    y = yt.reshape(T, Wd, B, nC).transpose(2, 3, 0, 1).reshape(B * L, Wd)
    M = B * L
    tm = min(tm, M)
    cb = col0 // Wd
    assert col0 % Wd == 0 and M % tm == 0
    return pl.pallas_call(
        _s5_out_kernel,
        out_shape=jax.ShapeDtypeStruct((M, Wd), BF16),
        grid=(M // tm,),
        in_specs=[pl.BlockSpec((tm, Wd), lambda i: (i, 0)),
                  pl.BlockSpec((tm, Wd), lambda i: (i, cb)),
                  pl.BlockSpec((1, Wd), lambda i: (0, 0)),
                  pl.BlockSpec((Wd, Wd), lambda i: (0, 0))],
        out_specs=pl.BlockSpec((tm, Wd), lambda i: (i, 0)),
        compiler_params=_cp(("parallel",)),
        name="s5_glu",
    )(y, p, d.astype(F32).reshape(1, Wd), w_glu.astype(BF16))


def _ret_kernel(q_ref, k_ref, v_ref, g_ref, cos_ref, sin_ref, dm_ref, qs_ref, ks_ref, cd_ref, o_ref, st_ref,
                carry_ref, *, n_chunks):
    c = pl.program_id(2)
    nC = n_chunks
    nS = nC // RET_SUB
    Cc = dm_ref.shape[0]
    dh = k_ref.shape[1]
    h2 = dh // 2

    def rows(ref, u):
        return ref[u * Cc:(u + 1) * Cc, :].astype(F32)

    blk = jnp.where(c < nS, c, c - nS)

    def rot(x, u):
        r0 = pl.multiple_of((blk * RET_SUB + u) * Cc, Cc)
        cos, sin = cos_ref[pl.ds(r0, Cc), :], sin_ref[pl.ds(r0, Cc), :]
        x1, x2 = x[:, :h2], x[:, h2:]
        return jnp.concatenate([x1 * cos - x2 * sin, x1 * sin + x2 * cos], axis=-1)

    @pl.when(c < nS)
    def _():
        for u in range(RET_SUB):
            kt = (rot(rows(k_ref, u), u) * (dh ** -0.5)).T
            v = v_ref[u * Cc:(u + 1) * Cc, :].astype(BF16)
            for d in range(2):
                st_ref[d, c * RET_SUB + u] = jnp.dot((kt * ks_ref[d]).astype(BF16), v, preferred_element_type=F32)

    @pl.when(c == nS)
    def _():
        for d in range(2):
            cd = cd_ref[d, 0:1, 0:1]
            carry_ref[...] = jnp.zeros_like(carry_ref)

            def body(i, carry, d=d, cd=cd):
                idx = i if d == 0 else nC - 1 - i
                t = st_ref[d, idx]
                st_ref[d, idx] = carry_ref[...]
                carry_ref[...] = carry_ref[...] * cd + t
                return carry

            lax.fori_loop(0, nC, body, 0)

    @pl.when(c >= nS)
    def _():
        for u in range(RET_SUB):
            ch = (c - nS) * RET_SUB + u
            qr = rot(rows(q_ref, u), u)
            kr = rot(rows(k_ref, u), u) * (dh ** -0.5)
            v = v_ref[u * Cc:(u + 1) * Cc, :].astype(BF16)
            s = lax.dot_general(qr.astype(BF16), kr.astype(BF16), (((1,), (1,)), ((), ())),
                                preferred_element_type=F32) * dm_ref[...]
            o = jnp.dot(s.astype(BF16), v, preferred_element_type=F32)
            for d in range(2):
                o = o + jnp.dot((qr * qs_ref[d]).astype(BF16), st_ref[d, ch].astype(BF16),
                                preferred_element_type=F32)
            mu = jnp.mean(o, axis=-1, keepdims=True)
            oc = o - mu
            var = jnp.mean(oc * oc, axis=-1, keepdims=True)
            g = rows(g_ref, u)
            o_ref[u * Cc:(u + 1) * Cc, :] = (oc * lax.rsqrt(var + GN_EPS)
                                             * (g * jax.nn.sigmoid(g))).astype(o_ref.dtype)


def retention_mixer(p, B, L, ret_decay):
    H, dh = RET_HEADS, RET_HEAD_DIM
    Cc = min(RET_CC, L)
    nC = L // Cc
    assert L % Cc == 0
    lg = -jnp.exp(ret_decay.astype(F32))
    pos = jnp.arange(Cc, dtype=F32)
    rel = pos[:, None] - pos[None, :]
    lf, lb = lg[0][:, None, None], lg[1][:, None, None]
    dm = jnp.where(rel >= 0, jnp.exp(jnp.maximum(rel, 0.0) * lf), jnp.exp(jnp.maximum(-rel, 0.0) * lb))
    qs = jnp.stack([jnp.exp((pos + 1.0)[None] * lg[0][:, None]),
                    jnp.exp((Cc - pos)[None] * lg[1][:, None])], axis=1)[..., None]
    ks = jnp.stack([jnp.exp((Cc - 1.0 - pos)[None] * lg[0][:, None]),
                    jnp.exp(pos[None] * lg[1][:, None])], axis=1)[:, :, None, :]
    cd = jnp.broadcast_to(jnp.exp(Cc * lg).T[:, :, None, None], (H, 2, SUBLANES, LANES))
    inv = ROPE_BASE ** (-jnp.arange(0, dh, 2, dtype=F32) / dh)
    ang = jnp.arange(L, dtype=F32)[:, None] * inv[None, :]
    cos, sin = jnp.cos(ang), jnp.sin(ang)

    assert nC % RET_SUB == 0
    nS = nC // RET_SUB
    rb = RET_SUB * Cc

    def kch(c):
        return jnp.where(c < nS, c, c - nS)

    def qch(c):
        return jnp.maximum(c - nS, 0)

    kv = lambda off: pl.BlockSpec((rb, dh), lambda b, h, c: (b * nS + kch(c), off * H + h))
    qo = lambda off: pl.BlockSpec((rb, dh), lambda b, h, c: (b * nS + qch(c), off * H + h))
    tab = pl.BlockSpec((L, dh // 2), lambda b, h, c: (0, 0), pipeline_mode=pl.Buffered(1))
    return pl.pallas_call(
        functools.partial(_ret_kernel, n_chunks=nC),
        out_shape=jax.ShapeDtypeStruct((B * L, H * dh), BF16),
        grid=(B, H, 2 * nS),
        in_specs=[qo(0), kv(1), kv(2), qo(3), tab, tab,
                  pl.BlockSpec((None, Cc, Cc), lambda b, h, c: (h, 0, 0)),
                  pl.BlockSpec((None, 2, Cc, 1), lambda b, h, c: (h, 0, 0, 0)),
                  pl.BlockSpec((None, 2, 1, Cc), lambda b, h, c: (h, 0, 0, 0)),
                  pl.BlockSpec((None, 2, SUBLANES, LANES), lambda b, h, c: (h, 0, 0, 0))],
        out_specs=qo(0),
        scratch_shapes=[pltpu.VMEM((2, nC, dh, dh), F32), pltpu.VMEM((dh, dh), F32)],
        compiler_params=_cp(("parallel", "parallel", "arbitrary")),
        name="retention",
    )(p, p, p, p, cos, sin, dm, qs, ks, cd)


def _na_bias_tables(rpb):
    Wc, WR, WC = GRID_W, NA_WIN_ROWS, NA_WIN_COLS
    hi = lax.Precision.HIGHEST
    rpb = rpb.astype(F32)
    c = jnp.arange(Wc)[:, None]
    kc = jnp.arange(Wc)[None, :]
    cs = jnp.clip(c - WC // 2, 0, Wc - WC)
    valid = (kc >= cs) & (kc < cs + WC)
    csel = (((kc - c + (WC - 1))[:, :, None] == jnp.arange(2 * WC - 1)) & valid[:, :, None]).astype(F32)
    dj = jnp.arange(WR)[None, :] - jnp.arange(WR)[:, None] + (WR - 1)
    rsel = (dj[:, :, None] == jnp.arange(2 * WR - 1)).astype(F32)
    t = jnp.einsum("hrs,cks->hrck", rpb, csel, precision=hi)
    t = jnp.einsum("djr,hrck->hdcjk", rsel, t, precision=hi)
    t = jnp.where(valid[None, None, :, None, :], t, NEG_INF)
    t = t.reshape(NA_HEADS // NA_HG, NA_HG, WR, Wc, WR * Wc).transpose(0, 2, 1, 3, 4)
    return t.reshape(NA_HEADS // NA_HG, WR, NA_HG * Wc, WR * Wc)


def _na_kernel(q_ref, k_ref, v_ref, b_ref, o_ref, *, rows):
    Wc, WR = GRID_W, NA_WIN_ROWS
    hw = NA_HG * NA_HEAD_DIM
    hq = NA_HG * Wc
    scale = NA_HEAD_DIM ** -0.5
    assert math.frexp(scale)[0] == 0.5
    rb = pl.program_id(2)
    own = (lax.broadcasted_iota(jnp.int32, (hq, hw), 0) // Wc
           == lax.broadcasted_iota(jnp.int32, (hq, hw), 1) // NA_HEAD_DIM)

    for i in range(NA_RB):
        r = rb * NA_RB + i
        rs = jnp.clip(r - WR // 2, 0, rows - WR)
        q = q_ref[i * Wc:(i + 1) * Wc, :] * jnp.asarray(scale, q_ref.dtype)
        qs = jnp.where(own, jnp.concatenate([q] * NA_HG, axis=0), jnp.zeros((), q.dtype))
        k0 = pl.multiple_of(rs * Wc, Wc)
        kw = k_ref[pl.ds(k0, WR * Wc), :]
        vw = v_ref[pl.ds(k0, WR * Wc), :]
        s = lax.dot_general(qs, kw, (((1,), (1,)), ((), ())), preferred_element_type=F32) + b_ref[r - rs]
        m = jnp.max(s, axis=-1, keepdims=True)
        e = jnp.exp(s - m)
        l = jnp.sum(e, axis=-1, keepdims=True)
        o = jnp.dot(e.astype(BF16), vw, preferred_element_type=F32) / l
        o = jnp.where(own, o, 0.0)
        out = o[0:Wc]
        for h in range(1, NA_HG):
            out = out + o[h * Wc:(h + 1) * Wc]
        o_ref[i * Wc:(i + 1) * Wc, :] = out.astype(o_ref.dtype)


def neighborhood_mixer(qkv, col0, B, L, rpb):
    Wc = GRID_W
    rows = L // Wc
    assert rows >= NA_WIN_ROWS and rows % NA_RB == 0
    hw = NA_HG * NA_HEAD_DIM
    nhg = NA_HEADS // NA_HG
    assert col0 % hw == 0
    c0 = col0 // hw
    bias = _na_bias_tables(rpb)
    nrb = rows // NA_RB
    return pl.pallas_call(
        functools.partial(_na_kernel, rows=rows),
        out_shape=jax.ShapeDtypeStruct((B * L, NA_WIDTH), BF16),
        grid=(B, nhg, nrb),
        in_specs=[pl.BlockSpec((NA_RB * Wc, hw), lambda b, g, r: (b * nrb + r, c0 + g)),
                  pl.BlockSpec((L, hw), lambda b, g, r: (b, c0 + nhg + g)),
                  pl.BlockSpec((L, hw), lambda b, g, r: (b, c0 + 2 * nhg + g)),
                  pl.BlockSpec((None, NA_WIN_ROWS, NA_HG * Wc, NA_WIN_ROWS * Wc), lambda b, g, r: (g, 0, 0, 0))],
        out_specs=pl.BlockSpec((NA_RB * Wc, hw), lambda b, g, r: (b * nrb + r, g)),
        compiler_params=_cp(("parallel", "parallel", "arbitrary")),
        name="neighborhood_attention",
    )(qkv, qkv, qkv, bias)


def _trunk(x, mem, B, L, prm, wb):
    n_mem = mem.shape[0] // B
    depth = prm["norm_g"].shape[0]
    hw3 = 3 * HY_WIDTH
    for layer in range(depth):
        i = layer // 2
        g = prm["norm_g"][layer]
        wo = wb["mix_wo"]
        if layer % 2 == 0:
            ident = jnp.zeros((3, S5_WIDTH), F32).at[1].set(1.0)
            cw = jnp.concatenate([prm["hy_short_w"][i].astype(F32), ident], axis=1)
            cb = jnp.concatenate([prm["hy_short_b"][i].astype(F32), jnp.zeros((S5_WIDTH,), F32)])
            uc = norm_matmul_conv(x, g[0], wb["ev_w_in"], cw, cb, L, layer=i)
            z = hyena_mixer(uc, B, L, prm["hy_w1"][i], prm["hy_b1"][i], prm["hy_freq"][i], prm["hy_w2"][i],
                            prm["hy_b2"][i], prm["hy_w3"][i], prm["hy_skip"][i])
            ss = s5_mixer(uc, hw3, B, L, prm["s5_a_re"][i], prm["s5_a_im"][i], prm["s5_log_dt"][i],
                          prm["s5_b_re"][i], prm["s5_b_im"][i], prm["s5_c_re"][i], prm["s5_c_im"][i],
                          prm["s5_d"][i], prm["s5_w_glu"][i])
            ops = [(z, wo, prm["hy_out_g"][i], 0), (ss, wo, None, HY_WIDTH)]
        else:
            p = norm_matmul(x, g[0], wb["od_w_in"], BF16, layer=i, name="odd_in_proj")
            ret = retention_mixer(p, B, L, prm["ret_decay"][i])
            na = neighborhood_mixer(p, 4 * RET_WIDTH, B, L, prm["na_rpb"][i])
            ops = [(ret, wo, None, 0), (na, wo, None, RET_WIDTH)]
        x, xn = matmul_norm_residual(ops, g[1], x, g[2], layer=layer, name="mix_out_proj")
        kv = norm_matmul(mem, prm["mem_norm_g"][layer], wb["xa_wkv"], BF16, layer=layer, name="xattn_kv_proj")
        x = xattn_block(xn, x, wb["xa_wq"], kv, wb["xa_wo"], g[3], B, L, n_mem, layer=layer)
        x = ffn_block(x, g[4], wb["ffn_wg"], wb["ffn_wu"], wb["ffn_wd"], g[5], layer=layer)
    return x


def kernel(x_prompt, x_sample, mem_prompt, mem_sample, norm_g, mix_wo, ev_w_in, hy_short_w, hy_short_b, hy_w1, hy_b1, hy_freq, hy_w2, hy_b2, hy_w3, hy_skip, hy_out_g, s5_a_re, s5_a_im, s5_log_dt, s5_b_re, s5_b_im, s5_c_re, s5_c_im, s5_d, s5_w_glu, od_w_in, ret_decay, na_rpb, mem_norm_g, xa_wq, xa_wkv, xa_wo, ffn_wg, ffn_wu, ffn_wd):
    prm = dict(norm_g=norm_g, hy_short_w=hy_short_w, hy_short_b=hy_short_b, hy_w1=hy_w1, hy_b1=hy_b1,
               hy_freq=hy_freq, hy_w2=hy_w2, hy_b2=hy_b2, hy_w3=hy_w3, hy_skip=hy_skip, hy_out_g=hy_out_g,
               s5_a_re=s5_a_re, s5_a_im=s5_a_im, s5_log_dt=s5_log_dt, s5_b_re=s5_b_re, s5_b_im=s5_b_im,
               s5_c_re=s5_c_re, s5_c_im=s5_c_im, s5_d=s5_d, s5_w_glu=s5_w_glu, ret_decay=ret_decay,
               na_rpb=na_rpb, mem_norm_g=mem_norm_g)
    wb = {k: v.astype(BF16) for k, v in dict(mix_wo=mix_wo, ev_w_in=ev_w_in, od_w_in=od_w_in, xa_wq=xa_wq,
                                             xa_wkv=xa_wkv, xa_wo=xa_wo, ffn_wg=ffn_wg, ffn_wu=ffn_wu,
                                             ffn_wd=ffn_wd).items()}
    outs = []
    for x, mem in ((x_prompt, mem_prompt), (x_sample, mem_sample)):
        B, L, D = x.shape
        y = _trunk(x.reshape(B * L, D), mem.reshape(-1, D), B, L, prm, wb)
        outs.append(y.reshape(B, L, D))
    return tuple(outs)
```

```python
import functools
import math

import jax
import jax.numpy as jnp
from jax import lax
from jax.experimental import pallas as pl
from jax.experimental.pallas import tpu as pltpu

F32 = jnp.float32
BF16 = jnp.bfloat16

V7X_VMEM_BYTES = 64 * 1024 * 1024
VMEM_LIMIT = V7X_VMEM_BYTES - 8 * 1024 * 1024
LANES = 128
SUBLANES = 8

D_MODEL = 2048
GRID_W = 64
HY_WIDTH = 3 * D_MODEL // 4
S5_WIDTH = D_MODEL - HY_WIDTH
S5_GROUP = 16
S5_GROUPS = S5_WIDTH // S5_GROUP
S5_STATE = 64
HY_ORDER = 2
HY_BANDS = 16
HY_FILTER_HIDDEN = 64
HY_DECAY_TARGET = 1e-2
HY_SHORT_DECAY_PCT = 0.3
HY_LONG_DECAY_PCT = 1.5
RET_WIDTH = D_MODEL // 2
RET_HEADS = 4
RET_HEAD_DIM = RET_WIDTH // RET_HEADS
ROPE_BASE = 10000.0
NA_WIDTH = D_MODEL - RET_WIDTH
NA_HEADS = 16
NA_HEAD_DIM = NA_WIDTH // NA_HEADS
NA_WIN_ROWS = 8
NA_WIN_COLS = 16
XA_HEADS = 4
XA_HEAD_DIM = D_MODEL // XA_HEADS
RMS_EPS = 1e-6
GN_EPS = 1e-6

FFT_N2 = 128
FFT_RB = SUBLANES
S5_T = 32
RET_CC = 256
RET_SUB = 4
FFN_SPLIT = 2
NA_HG = 4
NA_RB = 16
NEG_INF = -1e30


def _cp(sem, vmem=VMEM_LIMIT):
    return pltpu.CompilerParams(dimension_semantics=sem, vmem_limit_bytes=vmem)


def _rms(x, g, eps=RMS_EPS):
    return x * lax.rsqrt(jnp.mean(x * x, axis=-1, keepdims=True) + eps) * g


def _norm_mm_kernel(x_ref, g_ref, w_ref, o_ref, xn_ref):
    @pl.when(pl.program_id(1) == 0)
    def _():
        tm = x_ref.shape[0]
        rc = min(tm, 256)
        for r in range(0, tm, rc):
            xn_ref[r:r + rc, :] = _rms(x_ref[r:r + rc, :], g_ref[...]).astype(BF16)

    o_ref[...] = jnp.dot(xn_ref[...], w_ref[...], preferred_element_type=F32).astype(o_ref.dtype)


def _stacked(w):
    return w if w.ndim == 3 else w[None]


def norm_matmul(x, g, w, out_dtype, col0=0, ncols=None, layer=0, tm=1024, tn=1024, name="norm_matmul"):
    M, K = x.shape
    w = _stacked(w)
    N = w.shape[2] - col0 if ncols is None else ncols
    tm, tn = min(tm, M), min(tn, N)
    assert M % tm == 0 and N % tn == 0 and col0 % tn == 0
    cb0 = col0 // tn
    return pl.pallas_call(
        _norm_mm_kernel,
        out_shape=jax.ShapeDtypeStruct((M, N), out_dtype),
        grid=(M // tm, N // tn),
        in_specs=[pl.BlockSpec((tm, K), lambda i, j: (i, 0)),
                  pl.BlockSpec((1, K), lambda i, j: (0, 0)),
                  pl.BlockSpec((None, K, tn), lambda i, j: (layer, 0, cb0 + j))],
        out_specs=pl.BlockSpec((tm, tn), lambda i, j: (i, j)),
        scratch_shapes=[pltpu.VMEM((tm, K), BF16)],
        compiler_params=_cp(("parallel", "arbitrary")),
        name=name,
    )(x, g.reshape(1, K), w)


def _mm_norm_res_kernel(*refs, n_ops, prenorm):
    g_ref, x_ref, gn_ref, o_ref, on_ref = refs[-5:]
    tm = x_ref.shape[0]
    rc = tm // 2 if tm % 32 == 0 else tm
    for r in range(0, tm, rc):
        pos = 0
        y = None
        for t in range(n_ops):
            a = refs[pos][r:r + rc, :]
            w_ref = refs[pos + 1]
            pos += 2
            if prenorm[t]:
                a = _rms(a.astype(F32), refs[pos][...])
                pos += 1
            d = jnp.dot(a.astype(BF16), w_ref[...], preferred_element_type=F32)
            y = d if y is None else y + d
        x1 = x_ref[r:r + rc, :] + _rms(y, g_ref[...])
        o_ref[r:r + rc, :] = x1
        on_ref[r:r + rc, :] = _rms(x1, gn_ref[...]).astype(on_ref.dtype)


def matmul_norm_residual(ops, g, x, g_next, layer=0, tm=512, name="matmul_norm_residual"):
    M, N = x.shape
    tm = min(tm, M)
    assert M % tm == 0
    args, specs, prenorm = [], [], []
    for a, w, pg, k0 in ops:
        kt = a.shape[1]
        assert k0 % kt == 0
        args += [a, _stacked(w)]
        specs += [pl.BlockSpec((tm, kt), lambda i: (i, 0)),
                  pl.BlockSpec((None, kt, N), lambda i, kb=k0 // kt: (layer, kb, 0))]
        prenorm.append(pg is not None)
        if pg is not None:
            args.append(pg.reshape(1, kt))
            specs.append(pl.BlockSpec((1, kt), lambda i: (0, 0)))
    row = pl.BlockSpec((1, N), lambda i: (0, 0))
    tile = pl.BlockSpec((tm, N), lambda i: (i, 0))
    args += [g.reshape(1, N), x, g_next.reshape(1, N)]
    specs += [row, tile, row]
    return pl.pallas_call(
        functools.partial(_mm_norm_res_kernel, n_ops=len(ops), prenorm=tuple(prenorm)),
        out_shape=(jax.ShapeDtypeStruct((M, N), F32), jax.ShapeDtypeStruct((M, N), BF16)),
        grid=(M // tm,),
        in_specs=specs,
        out_specs=(tile, tile),
        compiler_params=_cp(("parallel",)),
        name=name,
    )(*args)


def _xattn_block_kernel(xn_ref, x_ref, wq_ref, k_ref, v_ref, wo_ref, g_ref, o_ref, *, heads):
    dh = xn_ref.shape[1] // heads
    scale = dh ** -0.5
    q = jnp.dot(xn_ref[...], wq_ref[...], preferred_element_type=F32).astype(BF16)
    outs = []
    for h in range(heads):
        sl = slice(h * dh, (h + 1) * dh)
        s = lax.dot_general(q[:, sl], k_ref[:, sl], (((1,), (1,)), ((), ())), preferred_element_type=F32) * scale
        m = jnp.max(s, axis=-1, keepdims=True)
        p = jnp.exp(s - m)
        l = jnp.sum(p, axis=-1, keepdims=True)
        outs.append((jnp.dot(p.astype(BF16), v_ref[:, sl], preferred_element_type=F32) / l).astype(BF16))
    o = jnp.concatenate(outs, axis=-1)
    y = jnp.dot(o, wo_ref[...], preferred_element_type=F32)
    o_ref[...] = x_ref[...] + _rms(y, g_ref[...])


def xattn_block(xn, x, wq, kv, wo, g, B, L, n_mem, layer=0, tm=512):
    M, D = x.shape
    tm = min(tm, L)
    assert L % tm == 0
    bpl = L // tm
    wq, wo = _stacked(wq), _stacked(wo)
    wspec = pl.BlockSpec((None, D, D), lambda i: (layer, 0, 0), pipeline_mode=pl.Buffered(1))
    return pl.pallas_call(
        functools.partial(_xattn_block_kernel, heads=XA_HEADS),
        out_shape=jax.ShapeDtypeStruct((M, D), F32),
        grid=(M // tm,),
        in_specs=[pl.BlockSpec((tm, D), lambda i: (i, 0)),
                  pl.BlockSpec((tm, D), lambda i: (i, 0)),
                  wspec,
                  pl.BlockSpec((n_mem, D), lambda i: (i // bpl, 0)),
                  pl.BlockSpec((n_mem, D), lambda i: (i // bpl, 1)),
                  wspec,
                  pl.BlockSpec((1, D), lambda i: (0, 0))],
        out_specs=pl.BlockSpec((tm, D), lambda i: (i, 0)),
        compiler_params=_cp(("parallel",)),
        name="xattn_block",
    )(xn, x, wq, kv, kv, wo, g.reshape(1, D))


def _ffn_kernel(x_ref, gi_ref, wg_ref, wu_ref, wd_ref, go_ref, o_ref, xn_ref):
    j = pl.program_id(1)

    tm = x_ref.shape[0]
    rc = min(tm, 256)

    @pl.when(j == 0)
    def _():
        for r in range(0, tm, rc):
            xn_ref[r:r + rc, :] = _rms(x_ref[r:r + rc, :], gi_ref[...]).astype(BF16)
        o_ref[...] = jnp.zeros_like(o_ref)

    th = wg_ref.shape[1]
    hh = th // FFN_SPLIT
    for c in range(FFN_SPLIT):
        xn = xn_ref[...]
        a = jnp.dot(xn, wg_ref[:, c * hh:(c + 1) * hh], preferred_element_type=F32)
        u = jnp.dot(xn, wu_ref[:, c * hh:(c + 1) * hh], preferred_element_type=F32)
        h = (a * jax.nn.sigmoid(a) * u).astype(BF16)
        o_ref[...] += jnp.dot(h, wd_ref[c * hh:(c + 1) * hh, :], preferred_element_type=F32)

    @pl.when(j == pl.num_programs(1) - 1)
    def _():
        for r in range(0, tm, SUBLANES):
            o_ref[r:r + SUBLANES, :] = (x_ref[r:r + SUBLANES, :]
                                        + _rms(o_ref[r:r + SUBLANES, :], go_ref[...]))


def ffn_block(x, g_in, wg, wu, wd, g_out, layer=0, tm=1024, th=512):
    M, D = x.shape
    wg, wu, wd = _stacked(wg), _stacked(wu), _stacked(wd)
    Hd = wg.shape[2]
    tm, th = min(tm, M), min(th, Hd)
    assert M % tm == 0 and Hd % th == 0
    return pl.pallas_call(
        _ffn_kernel,
        out_shape=jax.ShapeDtypeStruct((M, D), F32),
        grid=(M // tm, Hd // th),
        in_specs=[pl.BlockSpec((tm, D), lambda i, j: (i, 0)),
                  pl.BlockSpec((1, D), lambda i, j: (0, 0)),
                  pl.BlockSpec((None, D, th), lambda i, j: (layer, 0, j)),
                  pl.BlockSpec((None, D, th), lambda i, j: (layer, 0, j)),
                  pl.BlockSpec((None, th, D), lambda i, j: (layer, j, 0)),
                  pl.BlockSpec((1, D), lambda i, j: (0, 0))],
        out_specs=pl.BlockSpec((tm, D), lambda i, j: (i, 0)),
        scratch_shapes=[pltpu.VMEM((tm, D), BF16)],
        compiler_params=_cp(("parallel", "arbitrary")),
        name="ffn_block",
    )(x, g_in.reshape(1, D), wg, wu, wd, g_out.reshape(1, D))


HALO = 16


def _norm_mm_conv_kernel(xp_ref, xc_ref, xn_ref, g_ref, w_ref, cw_ref, cb_ref, o_ref, xs_ref, *, blocks_per_seq):
    tm = xc_ref.shape[0]

    @pl.when(pl.program_id(1) == 0)
    def _():
        li = pl.program_id(0) % blocks_per_seq
        g = g_ref[...]
        keep_prev = jnp.where(li == 0, 0.0, 1.0)
        keep_next = jnp.where(li == blocks_per_seq - 1, 0.0, 1.0)
        xs_ref[0:HALO] = (_rms(xp_ref[...], g) * keep_prev).astype(BF16)
        rc = min(tm, 256)
        for r in range(0, tm, rc):
            xs_ref[HALO + r:HALO + r + rc] = _rms(xc_ref[r:r + rc, :], g).astype(BF16)
        xs_ref[HALO + tm:] = (_rms(xn_ref[...], g) * keep_next).astype(BF16)

    n = tm + 2 * HALO
    tn = o_ref.shape[1]
    hw = tn // 2 if tn % (2 * LANES) == 0 else tn
    for c in range(0, tn, hw):
        cs = slice(c, c + hw)
        y = jnp.dot(xs_ref[...], w_ref[:, cs], preferred_element_type=F32)
        up = pltpu.roll(y, 1, axis=0)[HALO:HALO + tm]
        dn = pltpu.roll(y, n - 1, axis=0)[HALO:HALO + tm]
        o_ref[:, cs] = (up * cw_ref[0:1, cs] + y[HALO:HALO + tm] * cw_ref[1:2, cs] + dn * cw_ref[2:3, cs]
                        + cb_ref[:, cs])


def norm_matmul_conv(x, g, w, cw, cb, L, layer=0, tm=1024, tn=1024):
    M, K = x.shape
    w = _stacked(w)
    N = w.shape[2]
    tm, tn = min(tm, L), min(tn, N)
    assert L % tm == 0 and N % tn == 0 and tm % HALO == 0
    hb = tm // HALO
    nhb = M // HALO
    return pl.pallas_call(
        functools.partial(_norm_mm_conv_kernel, blocks_per_seq=L // tm),
        out_shape=jax.ShapeDtypeStruct((M, N), F32),
        grid=(M // tm, N // tn),
        in_specs=[pl.BlockSpec((HALO, K), lambda i, j: (jnp.maximum(i * hb - 1, 0), 0)),
                  pl.BlockSpec((tm, K), lambda i, j: (i, 0)),
                  pl.BlockSpec((HALO, K), lambda i, j: (jnp.minimum((i + 1) * hb, nhb - 1), 0)),
                  pl.BlockSpec((1, K), lambda i, j: (0, 0)),
                  pl.BlockSpec((None, K, tn), lambda i, j: (layer, 0, j)),
                  pl.BlockSpec((3, tn), lambda i, j: (0, j)),
                  pl.BlockSpec((1, tn), lambda i, j: (0, j))],
        out_specs=pl.BlockSpec((tm, tn), lambda i, j: (i, j)),
        scratch_shapes=[pltpu.VMEM((tm + 2 * HALO, K), BF16)],
        compiler_params=_cp(("parallel", "arbitrary")),
        name="even_in_proj_conv",
    )(x, x, x, g.reshape(1, K), w, cw, cb.reshape(1, N))


def _fft_tables(L):
    N2 = FFT_N2
    N = 2 * L
    N1 = N // N2
    N1h = N1 // 2
    n2 = jnp.arange(N2, dtype=jnp.int32)[:, None, None]
    k1 = jnp.arange(N1, dtype=jnp.int32)[None, :, None]
    n1 = jnp.arange(N1h, dtype=jnp.int32)[None, None, :]
    ph = ((n1 * N2 + n2) * k1) % N
    ang = ph.astype(F32) * (2.0 * math.pi / N)
    gr, gi = jnp.cos(ang), -jnp.sin(ang)
    g_fwd = jnp.concatenate([jnp.concatenate([gr, -gi], 2), jnp.concatenate([gi, gr], 2)], 1).astype(BF16)
    sgn = jnp.where(k1 % 2 == 0, 1.0, -1.0).astype(F32)
    g_flt = jnp.concatenate([jnp.concatenate([gr, sgn * gr], 2), jnp.concatenate([gi, sgn * gi], 2)], 1).astype(BF16)
    er = jnp.swapaxes(gr, 1, 2) / N
    ei = -jnp.swapaxes(gi, 1, 2) / N
    g_inv = jnp.concatenate([jnp.concatenate([er, -ei], 2), jnp.concatenate([ei, er], 2)], 1).astype(BF16)
    a2 = (jnp.arange(N2, dtype=jnp.int32)[:, None] * jnp.arange(N2, dtype=jnp.int32)[None, :]) % N2
    ang2 = a2.astype(F32) * (2.0 * math.pi / N2)
    fr, fi = jnp.cos(ang2), -jnp.sin(ang2)
    f_mid = jnp.concatenate([jnp.concatenate([fr, -fi], 1), jnp.concatenate([fi, fr], 1)], 0).astype(BF16)
    f_mid_inv = jnp.concatenate([jnp.concatenate([fr, fi], 1), jnp.concatenate([-fi, fr], 1)], 0).astype(BF16)
    return dict(g_fwd=g_fwd, g_flt=g_flt, g_inv=g_inv, f_mid=f_mid, f_mid_inv=f_mid_inv, N1=N1)


def _pack_complex(re, im):
    r = lax.bitcast_convert_type(re.astype(BF16).astype(F32), jnp.uint32)
    i = lax.bitcast_convert_type(im.astype(BF16).astype(F32), jnp.uint32)
    return r | (i >> 16)


def _unpack_complex(w):
    re = lax.bitcast_convert_type(w & jnp.uint32(0xFFFF0000), F32)
    im = lax.bitcast_convert_type(w << 16, F32)
    return re, im


def _to_slabs(src_ref, slab_ref, lead=()):
    ns, rows, _ = slab_ref.shape
    for s in range(ns):
        slab_ref[s] = src_ref[lead + (slice(None), slice(None), slice(s * LANES, (s + 1) * LANES))].reshape(
            rows, LANES)


def _slab_rows(slab_ref, r, n):
    return jnp.concatenate([slab_ref.at[s][pl.ds(r, n, stride=FFT_RB), :] for s in range(slab_ref.shape[0])],
                           axis=1)


def _set_slab_rows(slab_ref, r, val):
    n = val.shape[0]
    for s in range(slab_ref.shape[0]):
        slab_ref.at[s][pl.ds(r, n, stride=FFT_RB), :] = val[:, s * LANES:(s + 1) * LANES]


def _from_slabs(slab_ref, dst_ref, lead=()):
    ns, rows, _ = slab_ref.shape
    for s in range(ns):
        dst_ref[lead + (slice(None), slice(None), slice(s * LANES, (s + 1) * LANES))] = (
            slab_ref[s].reshape(rows // FFT_RB, FFT_RB, LANES))


def _fft_a_kernel(x_ref, g_ref, z_ref, xa_ref, xb_ref, zs_ref):
    n1 = z_ref.shape[0]
    _to_slabs(x_ref, xa_ref, (0,))
    _to_slabs(x_ref, xb_ref, (1,))
    for r in range(FFT_RB):
        xs = jnp.concatenate([_slab_rows(xa_ref, r, n1 // 2), _slab_rows(xb_ref, r, n1 // 2)], axis=0).astype(BF16)
        a = jnp.dot(g_ref[r], xs, preferred_element_type=F32)
        _set_slab_rows(zs_ref, r, _pack_complex(a[:n1], a[n1:]))
    _from_slabs(zs_ref, z_ref)


def fft_stage_a(x4, col0, C, g, W=512):
    B, N1h, N2, _ = x4.shape
    N1 = 2 * N1h
    W = min(W, C)
    assert B % 2 == 0 and C % W == 0 and col0 % W == 0 and N2 % FFT_RB == 0
    cb0 = col0 // W
    return pl.pallas_call(
        _fft_a_kernel,
        out_shape=jax.ShapeDtypeStruct((B // 2, N1, N2, C), jnp.uint32),
        grid=(B // 2, N2 // FFT_RB, C // W),
        in_specs=[pl.BlockSpec((2, N1h, FFT_RB, W), lambda p, j, c: (p, 0, j, cb0 + c)),
                  pl.BlockSpec((FFT_RB, 2 * N1, N1), lambda p, j, c: (j, 0, 0))],
        out_specs=pl.BlockSpec((None, N1, FFT_RB, W), lambda p, j, c: (p, 0, j, c)),
        scratch_shapes=[pltpu.VMEM((W // LANES, N1h * FFT_RB, LANES), F32),
                        pltpu.VMEM((W // LANES, N1h * FFT_RB, LANES), F32),
                        pltpu.VMEM((W // LANES, N1 * FFT_RB, LANES), jnp.uint32)],
        compiler_params=_cp(("parallel", "parallel", "parallel")),
        name="hyena_fft_a",
    )(x4, g)


def _fft_mid_kernel(*refs, kb, with_filter):
    if with_filter:
        z_ref, h_ref, f_ref, fi_ref, y_ref = refs
    else:
        z_ref, f_ref, y_ref = refs
    n2 = z_ref.shape[1]
    for k in range(kb):
        zr, zi = _unpack_complex(z_ref[k])
        zs = jnp.concatenate([zr, zi], axis=0).astype(BF16)
        x = jnp.dot(f_ref[...], zs, preferred_element_type=F32)
        if with_filter:
            xr, xi = x[:n2], x[n2:]
            hr, hi = _unpack_complex(h_ref[k])
            ys = jnp.concatenate([xr * hr - xi * hi, xr * hi + xi * hr], axis=0).astype(BF16)
            x = jnp.dot(fi_ref[...], ys, preferred_element_type=F32)
        y_ref[k] = _pack_complex(x[:n2], x[n2:])


def fft_stage_mid(z, tabs, h=None, order=0, kb=16, W=512):
    P, N1, N2, C = z.shape
    kb, W = min(kb, N1), min(W, C)
    assert N1 % kb == 0 and C % W == 0
    blk = pl.BlockSpec((None, kb, N2, W), lambda k, c, p: (p, k, 0, c))
    mat = pl.BlockSpec((2 * N2, 2 * N2), lambda k, c, p: (0, 0))
    if h is None:
        args, specs = (z, tabs["f_mid"]), [blk, mat]
    else:
        hblk = pl.BlockSpec((None, kb, N2, W), lambda k, c, p: (order, k, 0, c))
        args, specs = (z, h, tabs["f_mid"], tabs["f_mid_inv"]), [blk, hblk, mat, mat]
    return pl.pallas_call(
        functools.partial(_fft_mid_kernel, kb=kb, with_filter=h is not None),
        out_shape=jax.ShapeDtypeStruct(z.shape, jnp.uint32),
        grid=(N1 // kb, C // W, P),
        in_specs=specs,
        out_specs=blk,
        compiler_params=_cp(("parallel", "parallel", "parallel")),
        name="hyena_fft_mid",
    )(*args)


def _fft_c_kernel(y_ref, g_ref, x_ref, o_ref, ys_ref, xa_ref, xb_ref):
    n1h = x_ref.shape[1]
    _to_slabs(y_ref, ys_ref)
    _to_slabs(x_ref, xa_ref, (0,))
    _to_slabs(x_ref, xb_ref, (1,))
    for r in range(FFT_RB):
        yr, yi = _unpack_complex(_slab_rows(ys_ref, r, 2 * n1h))
        ys = jnp.concatenate([yr, yi], axis=0).astype(BF16)
        c = jnp.dot(g_ref[r], ys, preferred_element_type=F32)
        _set_slab_rows(xa_ref, r, _slab_rows(xa_ref, r, n1h) * c[:n1h])
        _set_slab_rows(xb_ref, r, _slab_rows(xb_ref, r, n1h) * c[n1h:])
    _from_slabs(xa_ref, o_ref, (0,))
    _from_slabs(xb_ref, o_ref, (1,))


def fft_stage_c(y, g, x4, xcol0, W=512):
    P, N1, N2, C = y.shape
    N1h = N1 // 2
    W = min(W, C)
    assert C % W == 0 and xcol0 % W == 0
    xb = xcol0 // W
    slab = lambda n, dt: pltpu.VMEM((W // LANES, n * FFT_RB, LANES), dt)
    return pl.pallas_call(
        _fft_c_kernel,
        out_shape=jax.ShapeDtypeStruct((2 * P, N1h, N2, C), F32),
        grid=(P, N2 // FFT_RB, C // W),
        in_specs=[pl.BlockSpec((None, N1, FFT_RB, W), lambda p, j, c: (p, 0, j, c)),
                  pl.BlockSpec((FFT_RB, N1, 2 * N1), lambda p, j, c: (j, 0, 0)),
                  pl.BlockSpec((2, N1h, FFT_RB, W), lambda p, j, c: (p, 0, j, xb + c))],
        out_specs=pl.BlockSpec((2, N1h, FFT_RB, W), lambda p, j, c: (p, 0, j, c)),
        scratch_shapes=[slab(N1, jnp.uint32), slab(N1h, F32), slab(N1h, F32)],
        compiler_params=_cp(("parallel", "parallel", "parallel")),
        name="hyena_fft_c",
    )(y, g, x4)


def _fft_ca_kernel(y_ref, gi_ref, x_ref, gf_ref, z_ref, ys_ref, xa_ref, xb_ref, zs_ref):
    n1h = x_ref.shape[1]
    n1 = 2 * n1h
    _to_slabs(y_ref, ys_ref)
    _to_slabs(x_ref, xa_ref, (0,))
    _to_slabs(x_ref, xb_ref, (1,))
    for r in range(FFT_RB):
        yr, yi = _unpack_complex(_slab_rows(ys_ref, r, n1))
        c = jnp.dot(gi_ref[r], jnp.concatenate([yr, yi], axis=0).astype(BF16), preferred_element_type=F32)
        gated = jnp.concatenate([_slab_rows(xa_ref, r, n1h) * c[:n1h], _slab_rows(xb_ref, r, n1h) * c[n1h:]], axis=0)
        a = jnp.dot(gf_ref[r], gated.astype(BF16), preferred_element_type=F32)
        _set_slab_rows(zs_ref, r, _pack_complex(a[:n1], a[n1:]))
    _from_slabs(zs_ref, z_ref)


def fft_stage_ca(y, g_inv, x4, xcol0, g_fwd, W=512):
    P, N1, N2, C = y.shape
    N1h = N1 // 2
    W = min(W, C)
    assert C % W == 0 and xcol0 % W == 0
    xb = xcol0 // W
    slab = lambda n, dt: pltpu.VMEM((W // LANES, n * FFT_RB, LANES), dt)
    zblk = pl.BlockSpec((None, N1, FFT_RB, W), lambda p, j, c: (p, 0, j, c))
    return pl.pallas_call(
        _fft_ca_kernel,
        out_shape=jax.ShapeDtypeStruct((P, N1, N2, C), jnp.uint32),
        grid=(P, N2 // FFT_RB, C // W),
        in_specs=[zblk,
                  pl.BlockSpec((FFT_RB, N1, 2 * N1), lambda p, j, c: (j, 0, 0)),
                  pl.BlockSpec((2, N1h, FFT_RB, W), lambda p, j, c: (p, 0, j, xb + c)),
                  pl.BlockSpec((FFT_RB, 2 * N1, N1), lambda p, j, c: (j, 0, 0))],
        out_specs=zblk,
        scratch_shapes=[slab(N1, jnp.uint32), slab(N1h, F32), slab(N1h, F32), slab(N1, jnp.uint32)],
        compiler_params=_cp(("parallel", "parallel", "parallel")),
        name="hyena_fft_ca",
    )(y, g_inv, x4, g_fwd)


def _filter_a_kernel(bands_ref, w1t_ref, w1c_ref, w1s_ref, b1_ref, f_ref, w2_ref, b2_ref, wfh_ref, wfl_ref, wbh_ref,
                     wbl_ref, dl_ref, sk_ref, g_ref, z_ref, hid_ref, flt_ref, zs_ref, *, L):
    hi = lax.Precision.HIGHEST
    j = pl.program_id(0)
    n1 = z_ref.shape[0]
    n1h = n1 // 2
    rows = FFT_RB * n1h

    def positions(shape, axis):
        q = lax.broadcasted_iota(jnp.int32, shape, axis)
        return (q % n1h) * FFT_N2 + j * FFT_RB + q // n1h

    @pl.when(pl.program_id(1) + pl.program_id(2) == 0)
    def _():
        m_row = positions((1, rows), 1)
        for d, pos in enumerate((m_row, L - m_row)):
            t = pos.astype(F32) * (1.0 / L)
            ang = (2.0 * math.pi) * bands_ref[...] * t
            pre = (w1t_ref[...] * t
                   + jnp.dot(w1c_ref[...], jnp.cos(ang), preferred_element_type=F32, precision=hi)
                   + jnp.dot(w1s_ref[...], jnp.sin(ang), preferred_element_type=F32, precision=hi)
                   + b1_ref[...])
            h = jnp.sin(f_ref[0] * pre)
            h = jnp.sin(f_ref[1] * (jnp.dot(w2_ref[...], h, preferred_element_type=F32, precision=hi) + b2_ref[...]))
            h = h.T
            h_hi = h.astype(BF16)
            hid_ref[d, 0] = h_hi
            hid_ref[d, 1] = (h - h_hi.astype(F32)).astype(BF16)

    f1 = lax.broadcasted_iota(jnp.int32, (n1h, 1), 0).astype(F32) * (FFT_N2 / L)
    f2 = (j * FFT_RB + lax.broadcasted_iota(jnp.int32, (FFT_RB, 1), 0)).astype(F32) * (1.0 / L)
    for d, (wh_ref, wl_ref) in enumerate(((wfh_ref, wfl_ref), (wbh_ref, wbl_ref))):
        e1 = jnp.exp((-f1 if d == 0 else f1) * dl_ref[...])
        e2 = jnp.exp((-f2 if d == 0 else f2 - 1.0) * dl_ref[...])
        out = (jnp.dot(hid_ref[d, 0], wh_ref[...], preferred_element_type=F32)
               + jnp.dot(hid_ref[d, 0], wl_ref[...], preferred_element_type=F32)
               + jnp.dot(hid_ref[d, 1], wh_ref[...], preferred_element_type=F32))
        for r in range(FFT_RB):
            rs = slice(r * n1h, (r + 1) * n1h)
            flt_ref[d, rs, :] = out[rs] * (e1 * e2[r:r + 1])

    @pl.when(j == 0)
    def _():
        flt_ref[0, 0:1, :] = flt_ref[0, 0:1, :] + sk_ref[...]
        flt_ref[1, 0:1, :] = jnp.zeros_like(sk_ref)

    for r in range(FFT_RB):
        xs = jnp.concatenate([flt_ref[0, r * n1h:(r + 1) * n1h, :], flt_ref[1, r * n1h:(r + 1) * n1h, :]],
                             axis=0).astype(BF16)
        a = jnp.dot(g_ref[r], xs, preferred_element_type=F32)
        _set_slab_rows(zs_ref, r, _pack_complex(a[:n1], a[n1:]))
    _from_slabs(zs_ref, z_ref)


def filter_stage_a(w1, b1, freq, w2, b2, w3, skip, L, g_flt, W=512):
    C, Hh, nb, N2 = HY_WIDTH, HY_FILTER_HIDDEN, HY_BANDS, FFT_N2
    N1h = L // N2
    N1 = 2 * N1h
    col = lambda v: v.astype(F32).reshape(-1, 1)
    w1 = w1.astype(F32)
    w3_hi = w3.astype(BF16)
    w3_lo = (w3.astype(F32) - w3_hi.astype(F32)).astype(BF16)
    deltas = jnp.abs(jnp.linspace(math.log(HY_DECAY_TARGET) / HY_LONG_DECAY_PCT,
                                  math.log(HY_DECAY_TARGET) / HY_SHORT_DECAY_PCT, C, dtype=F32)).reshape(1, C)
    ncb = C // W
    small = lambda *shape: pl.BlockSpec(shape, lambda j, o, c: (0,) * len(shape))
    w3spec = lambda d: pl.BlockSpec((Hh, W), lambda j, o, c: (0, (2 * o + d) * ncb + c))
    rows = FFT_RB * N1h
    return pl.pallas_call(
        functools.partial(_filter_a_kernel, L=L),
        out_shape=jax.ShapeDtypeStruct((HY_ORDER, N1, N2, C), jnp.uint32),
        grid=(N2 // FFT_RB, HY_ORDER, ncb),
        in_specs=[small(nb, 1), small(Hh, 1), small(Hh, nb), small(Hh, nb), small(Hh, 1),
                  small(2, Hh, 1), small(Hh, Hh), small(Hh, 1),
                  w3spec(0), w3spec(0), w3spec(1), w3spec(1),
                  pl.BlockSpec((1, W), lambda j, o, c: (0, c)),
                  pl.BlockSpec((None, 1, W), lambda j, o, c: (o, 0, c)),
                  pl.BlockSpec((FFT_RB, 2 * N1, N1), lambda j, o, c: (j, 0, 0))],
        out_specs=pl.BlockSpec((None, N1, FFT_RB, W), lambda j, o, c: (o, 0, j, c)),
        scratch_shapes=[pltpu.VMEM((2, 2, rows, Hh), BF16), pltpu.VMEM((2, rows, W), F32),
                        pltpu.VMEM((W // LANES, N1 * FFT_RB, LANES), jnp.uint32)],
        compiler_params=_cp(("parallel", "arbitrary", "arbitrary")),
        name="hyena_filter_fft_a",
    )(col(jnp.arange(1, nb + 1, dtype=F32)), col(w1[0]), w1[1:1 + nb].T, w1[1 + nb:].T, col(b1),
      freq.astype(F32)[:, :, None], w2.astype(F32).T, col(b2), w3_hi, w3_lo, w3_hi, w3_lo, deltas,
      skip.astype(F32).reshape(HY_ORDER, 1, C), g_flt)


def hyena_mixer(uc, B, L, w1, b1, freq, w2, b2, w3, skip):
    C = HY_WIDTH
    N2 = FFT_N2
    N1h = L // N2
    tabs = _fft_tables(L)
    uc4 = uc.reshape(B, N1h, N2, uc.shape[1])
    spec = fft_stage_mid(filter_stage_a(w1, b1, freq, w2, b2, w3, skip, L, tabs["g_flt"]), tabs)
    y = fft_stage_mid(fft_stage_a(uc4, 0, C, tabs["g_fwd"]), tabs, h=spec, order=0)
    zz = fft_stage_ca(y, tabs["g_inv"], uc4, C, tabs["g_fwd"])
    y = fft_stage_mid(zz, tabs, h=spec, order=1)
    z = fft_stage_c(y, tabs["g_inv"], uc4, 2 * C)
    return z.reshape(B * L, C)


def _s5_tables(a_re, a_im, log_dt, b_re, b_im, c_re, c_im, n_chunks):
    T = S5_T
    G, P, I = S5_GROUPS, S5_STATE, S5_GROUP
    f32 = lambda a: a.astype(F32)
    a_re, a_im, b_re, b_im, c_re, c_im = map(f32, (a_re, a_im, b_re, b_im, c_re, c_im))
    step = jnp.exp(f32(log_dt))[..., None]
    lr, li = a_re * step, a_im * step
    mag = jnp.exp(lr)
    br_, bi_ = mag * jnp.cos(li), mag * jnp.sin(li)
    den = a_re * a_re + a_im * a_im
    qr = ((br_ - 1.0) * a_re + bi_ * a_im) / den
    qi = (bi_ * a_re - (br_ - 1.0) * a_im) / den
    bbr = qr[..., None] * b_re - qi[..., None] * b_im
    bbi = qr[..., None] * b_im + qi[..., None] * b_re

    def lam_pow(k):
        k = k.astype(F32)
        m = jnp.exp(lr[..., None] * k)
        return m * jnp.cos(li[..., None] * k), m * jnp.sin(li[..., None] * k)

    lags = jnp.arange(T + 1)
    pr, pi_ = lam_pow(lags)
    cpr = c_re[..., None] * pr[:, :, None] - c_im[..., None] * pi_[:, :, None]
    cpi = c_re[..., None] * pi_[:, :, None] + c_im[..., None] * pr[:, :, None]
    kern = jnp.einsum("dgjpk,dgpi->dgkji", cpr, bbr) - jnp.einsum("dgjpk,dgpi->dgkji", cpi, bbi)
    tt = jnp.arange(T)
    lag = tt[:, None] - tt[None, :]
    sel_f = (lag[:, :, None] == lags).astype(F32)
    sel_b = (-lag[:, :, None] == lags).astype(F32)
    toep = (jnp.einsum("tuk,gkji->gtjui", sel_f, kern[0], precision=lax.Precision.HIGHEST)
            + jnp.einsum("tuk,gkji->gtjui", sel_b, kern[1], precision=lax.Precision.HIGHEST))
    toep = toep.reshape(G, T * I, T * I)
    ef = T - 1 - tt
    eb = tt

    def bst(d, e):
        wr, wi = pr[d][..., e], pi_[d][..., e]
        re = wr[..., None] * bbr[d][:, :, None] - wi[..., None] * bbi[d][:, :, None]
        im = wr[..., None] * bbi[d][:, :, None] + wi[..., None] * bbr[d][:, :, None]
        return jnp.concatenate([re, im], axis=1).reshape(G, 2 * P, T * I)

    bst_all = jnp.stack([bst(0, ef), bst(1, eb)], axis=1)

    def cst(d, e):
        xr, xi = cpr[d][..., e], cpi[d][..., e]
        m = jnp.concatenate([xr, -xi], axis=2)
        return m.transpose(0, 3, 1, 2).reshape(G, T * I, 2 * P)

    cst_all = jnp.stack([cst(0, tt + 1), cst(1, T - tt)], axis=1)
    nsteps = max(1, int(math.log2(n_chunks)))
    e2 = T * (2 ** jnp.arange(nsteps))
    ar, ai = lam_pow(e2)
    ap = jnp.stack([ar, ai], axis=-1).transpose(1, 0, 3, 4, 2)
    ap = jnp.broadcast_to(ap[..., None], ap.shape + (LANES,))
    return toep.astype(BF16), bst_all.astype(BF16), cst_all.astype(BF16), ap, nsteps


def _s5_kernel(u_ref, toep_ref, bst_ref, cst_ref, ap_ref, y_ref, *, n_chunks, nsteps):
    T, I, cols = u_ref.shape
    P = S5_STATE
    u = u_ref[...].reshape(T * I, cols).astype(BF16)
    y = jnp.dot(toep_ref[...], u, preferred_element_type=F32)
    cidx = lax.broadcasted_iota(jnp.int32, (P, cols), 1) % n_chunks

    def shifted(x, sh, d):
        if d == 0:
            return jnp.where(cidx >= sh, pltpu.roll(x, sh, axis=1), 0.0)
        return jnp.where(cidx < n_chunks - sh, pltpu.roll(x, cols - sh, axis=1), 0.0)

    for d in range(2):
        v = jnp.dot(bst_ref[d], u, preferred_element_type=F32)
        sr, si = v[:P], v[P:]
        for j in range(nsteps):
            if (1 << j) >= n_chunks:
                break
            ar, ai = ap_ref[d, j, 0][:, :1], ap_ref[d, j, 1][:, :1]
            rr, ri = shifted(sr, 1 << j, d), shifted(si, 1 << j, d)
            sr, si = sr + ar * rr - ai * ri, si + ar * ri + ai * rr
        s_in = jnp.concatenate([shifted(sr, 1, d), shifted(si, 1, d)], axis=0).astype(BF16)
        y = y + jnp.dot(cst_ref[d], s_in, preferred_element_type=F32)
    y_ref[...] = y.reshape(T, I, cols)


def _s5_out_kernel(y_ref, u_ref, d_ref, w_ref, o_ref):
    y = y_ref[...] + d_ref[...] * u_ref[...]
    y = jax.nn.gelu(y, approximate=True)
    z = jnp.dot(y.astype(BF16), w_ref[...], preferred_element_type=F32)
    o_ref[...] = (y * jax.nn.sigmoid(z)).astype(o_ref.dtype)


def s5_mixer(p, col0, B, L, a_re, a_im, log_dt, b_re, b_im, c_re, c_im, d, w_glu, tm=1024):
    T, Wd, I, G = S5_T, S5_WIDTH, S5_GROUP, S5_GROUPS
    nC = L // T
    cols = B * nC
    assert L % T == 0 and nC & (nC - 1) == 0 and cols % LANES == 0
    toep, bst, cst, ap, nsteps = _s5_tables(a_re, a_im, log_dt, b_re, b_im, c_re, c_im, nC)
    u = p[:, col0:col0 + Wd]
    ut = u.reshape(B, nC, T, Wd).transpose(2, 3, 0, 1).reshape(T, Wd, cols)
    blk = pl.BlockSpec((T, I, cols), lambda g: (0, g, 0))
    yt = pl.pallas_call(
        functools.partial(_s5_kernel, n_chunks=nC, nsteps=nsteps),
        out_shape=jax.ShapeDtypeStruct((T, Wd, cols), F32),
        grid=(G,),
        in_specs=[blk,
                  pl.BlockSpec((None, T * I, T * I), lambda g: (g, 0, 0)),
                  pl.BlockSpec((None, 2, 2 * S5_STATE, T * I), lambda g: (g, 0, 0, 0)),
                  pl.BlockSpec((None, 2, T * I, 2 * S5_STATE), lambda g: (g, 0, 0, 0)),
                  pl.BlockSpec((None, 2, nsteps, 2, S5_STATE, LANES), lambda g: (g, 0, 0, 0, 0, 0))],
        out_specs=blk,
        compiler_params=_cp(("parallel",)),
        name="s5_scan",
    )(ut, toep, bst, cst, ap)
    y = yt.reshape(T, Wd, B, nC).transpose(2, 3, 0, 1).reshape(B * L, Wd)
    M = B * L
    tm = min(tm, M)
    cb = col0 // Wd
    assert col0 % Wd == 0 and M % tm == 0
    return pl.pallas_call(
        _s5_out_kernel,
        out_shape=jax.ShapeDtypeStruct((M, Wd), BF16),
        grid=(M // tm,),
        in_specs=[pl.BlockSpec((tm, Wd), lambda i: (i, 0)),
                  pl.BlockSpec((tm, Wd), lambda i: (i, cb)),
                  pl.BlockSpec((1, Wd), lambda i: (0, 0)),
                  pl.BlockSpec((Wd, Wd), lambda i: (0, 0))],
        out_specs=pl.BlockSpec((tm, Wd), lambda i: (i, 0)),
        compiler_params=_cp(("parallel",)),
        name="s5_glu",
    )(y, p, d.astype(F32).reshape(1, Wd), w_glu.astype(BF16))


def _ret_kernel(q_ref, k_ref, v_ref, g_ref, cos_ref, sin_ref, dm_ref, qs_ref, ks_ref, cd_ref, o_ref, st_ref,
                carry_ref, *, n_chunks):
    c = pl.program_id(2)
    nC = n_chunks
    nS = nC // RET_SUB
    Cc = dm_ref.shape[0]
    dh = k_ref.shape[1]
    h2 = dh // 2

    def rows(ref, u):
        return ref[u * Cc:(u + 1) * Cc, :].astype(F32)

    blk = jnp.where(c < nS, c, c - nS)

    def rot(x, u):
        r0 = pl.multiple_of((blk * RET_SUB + u) * Cc, Cc)
        cos, sin = cos_ref[pl.ds(r0, Cc), :], sin_ref[pl.ds(r0, Cc), :]
        x1, x2 = x[:, :h2], x[:, h2:]
        return jnp.concatenate([x1 * cos - x2 * sin, x1 * sin + x2 * cos], axis=-1)

    @pl.when(c < nS)
    def _():
        for u in range(RET_SUB):
            kt = (rot(rows(k_ref, u), u) * (dh ** -0.5)).T
            v = v_ref[u * Cc:(u + 1) * Cc, :].astype(BF16)
            for d in range(2):
                st_ref[d, c * RET_SUB + u] = jnp.dot((kt * ks_ref[d]).astype(BF16), v, preferred_element_type=F32)

    @pl.when(c == nS)
    def _():
        for d in range(2):
            cd = cd_ref[d, 0:1, 0:1]
            carry_ref[...] = jnp.zeros_like(carry_ref)

            def body(i, carry, d=d, cd=cd):
                idx = i if d == 0 else nC - 1 - i
                t = st_ref[d, idx]
                st_ref[d, idx] = carry_ref[...]
                carry_ref[...] = carry_ref[...] * cd + t
                return carry

            lax.fori_loop(0, nC, body, 0)

    @pl.when(c >= nS)
    def _():
        for u in range(RET_SUB):
            ch = (c - nS) * RET_SUB + u
            qr = rot(rows(q_ref, u), u)
            kr = rot(rows(k_ref, u), u) * (dh ** -0.5)
            v = v_ref[u * Cc:(u + 1) * Cc, :].astype(BF16)
            s = lax.dot_general(qr.astype(BF16), kr.astype(BF16), (((1,), (1,)), ((), ())),
                                preferred_element_type=F32) * dm_ref[...]
            o = jnp.dot(s.astype(BF16), v, preferred_element_type=F32)
            for d in range(2):
                o = o + jnp.dot((qr * qs_ref[d]).astype(BF16), st_ref[d, ch].astype(BF16),
                                preferred_element_type=F32)
            mu = jnp.mean(o, axis=-1, keepdims=True)
            oc = o - mu
            var = jnp.mean(oc * oc, axis=-1, keepdims=True)
            g = rows(g_ref, u)
            o_ref[u * Cc:(u + 1) * Cc, :] = (oc * lax.rsqrt(var + GN_EPS)
                                             * (g * jax.nn.sigmoid(g))).astype(o_ref.dtype)


def retention_mixer(p, B, L, ret_decay):
    H, dh = RET_HEADS, RET_HEAD_DIM
    Cc = min(RET_CC, L)
    nC = L // Cc
    assert L % Cc == 0
    lg = -jnp.exp(ret_decay.astype(F32))
    pos = jnp.arange(Cc, dtype=F32)
    rel = pos[:, None] - pos[None, :]
    lf, lb = lg[0][:, None, None], lg[1][:, None, None]
    dm = jnp.where(rel >= 0, jnp.exp(jnp.maximum(rel, 0.0) * lf), jnp.exp(jnp.maximum(-rel, 0.0) * lb))
    qs = jnp.stack([jnp.exp((pos + 1.0)[None] * lg[0][:, None]),
                    jnp.exp((Cc - pos)[None] * lg[1][:, None])], axis=1)[..., None]
    ks = jnp.stack([jnp.exp((Cc - 1.0 - pos)[None] * lg[0][:, None]),
                    jnp.exp(pos[None] * lg[1][:, None])], axis=1)[:, :, None, :]
    cd = jnp.broadcast_to(jnp.exp(Cc * lg).T[:, :, None, None], (H, 2, SUBLANES, LANES))
    inv = ROPE_BASE ** (-jnp.arange(0, dh, 2, dtype=F32) / dh)
    ang = jnp.arange(L, dtype=F32)[:, None] * inv[None, :]
    cos, sin = jnp.cos(ang), jnp.sin(ang)

    assert nC % RET_SUB == 0
    nS = nC // RET_SUB
    rb = RET_SUB * Cc

    def kch(c):
        return jnp.where(c < nS, c, c - nS)

    def qch(c):
        return jnp.maximum(c - nS, 0)

    kv = lambda off: pl.BlockSpec((rb, dh), lambda b, h, c: (b * nS + kch(c), off * H + h))
    qo = lambda off: pl.BlockSpec((rb, dh), lambda b, h, c: (b * nS + qch(c), off * H + h))
    tab = pl.BlockSpec((L, dh // 2), lambda b, h, c: (0, 0), pipeline_mode=pl.Buffered(1))
    return pl.pallas_call(
        functools.partial(_ret_kernel, n_chunks=nC),
        out_shape=jax.ShapeDtypeStruct((B * L, H * dh), BF16),
        grid=(B, H, 2 * nS),
        in_specs=[qo(0), kv(1), kv(2), qo(3), tab, tab,
                  pl.BlockSpec((None, Cc, Cc), lambda b, h, c: (h, 0, 0)),
                  pl.BlockSpec((None, 2, Cc, 1), lambda b, h, c: (h, 0, 0, 0)),
                  pl.BlockSpec((None, 2, 1, Cc), lambda b, h, c: (h, 0, 0, 0)),
                  pl.BlockSpec((None, 2, SUBLANES, LANES), lambda b, h, c: (h, 0, 0, 0))],
        out_specs=qo(0),
        scratch_shapes=[pltpu.VMEM((2, nC, dh, dh), F32), pltpu.VMEM((dh, dh), F32)],
        compiler_params=_cp(("parallel", "parallel", "arbitrary")),
        name="retention",
    )(p, p, p, p, cos, sin, dm, qs, ks, cd)


def _na_bias_tables(rpb):
    Wc, WR, WC = GRID_W, NA_WIN_ROWS, NA_WIN_COLS
    hi = lax.Precision.HIGHEST
    rpb = rpb.astype(F32)
    c = jnp.arange(Wc)[:, None]
    kc = jnp.arange(Wc)[None, :]
    cs = jnp.clip(c - WC // 2, 0, Wc - WC)
    valid = (kc >= cs) & (kc < cs + WC)
    csel = (((kc - c + (WC - 1))[:, :, None] == jnp.arange(2 * WC - 1)) & valid[:, :, None]).astype(F32)
    dj = jnp.arange(WR)[None, :] - jnp.arange(WR)[:, None] + (WR - 1)
    rsel = (dj[:, :, None] == jnp.arange(2 * WR - 1)).astype(F32)
    t = jnp.einsum("hrs,cks->hrck", rpb, csel, precision=hi)
    t = jnp.einsum("djr,hrck->hdcjk", rsel, t, precision=hi)
    t = jnp.where(valid[None, None, :, None, :], t, NEG_INF)
    t = t.reshape(NA_HEADS // NA_HG, NA_HG, WR, Wc, WR * Wc).transpose(0, 2, 1, 3, 4)
    return t.reshape(NA_HEADS // NA_HG, WR, NA_HG * Wc, WR * Wc)


def _na_kernel(q_ref, k_ref, v_ref, b_ref, o_ref, *, rows):
    Wc, WR = GRID_W, NA_WIN_ROWS
    hw = NA_HG * NA_HEAD_DIM
    hq = NA_HG * Wc
    scale = NA_HEAD_DIM ** -0.5
    assert math.frexp(scale)[0] == 0.5
    rb = pl.program_id(2)
    own = (lax.broadcasted_iota(jnp.int32, (hq, hw), 0) // Wc
           == lax.broadcasted_iota(jnp.int32, (hq, hw), 1) // NA_HEAD_DIM)

    for i in range(NA_RB):
        r = rb * NA_RB + i
        rs = jnp.clip(r - WR // 2, 0, rows - WR)
        q = q_ref[i * Wc:(i + 1) * Wc, :] * jnp.asarray(scale, q_ref.dtype)
        qs = jnp.where(own, jnp.concatenate([q] * NA_HG, axis=0), jnp.zeros((), q.dtype))
        k0 = pl.multiple_of(rs * Wc, Wc)
        kw = k_ref[pl.ds(k0, WR * Wc), :]
        vw = v_ref[pl.ds(k0, WR * Wc), :]
        s = lax.dot_general(qs, kw, (((1,), (1,)), ((), ())), preferred_element_type=F32) + b_ref[r - rs]
        m = jnp.max(s, axis=-1, keepdims=True)
        e = jnp.exp(s - m)
        l = jnp.sum(e, axis=-1, keepdims=True)
        o = jnp.dot(e.astype(BF16), vw, preferred_element_type=F32) / l
        o = jnp.where(own, o, 0.0)
        out = o[0:Wc]
        for h in range(1, NA_HG):
            out = out + o[h * Wc:(h + 1) * Wc]
        o_ref[i * Wc:(i + 1) * Wc, :] = out.astype(o_ref.dtype)


def neighborhood_mixer(qkv, col0, B, L, rpb):
    Wc = GRID_W
    rows = L // Wc
    assert rows >= NA_WIN_ROWS and rows % NA_RB == 0
    hw = NA_HG * NA_HEAD_DIM
    nhg = NA_HEADS // NA_HG
    assert col0 % hw == 0
    c0 = col0 // hw
    bias = _na_bias_tables(rpb)
    nrb = rows // NA_RB
    return pl.pallas_call(
        functools.partial(_na_kernel, rows=rows),
        out_shape=jax.ShapeDtypeStruct((B * L, NA_WIDTH), BF16),
        grid=(B, nhg, nrb),
        in_specs=[pl.BlockSpec((NA_RB * Wc, hw), lambda b, g, r: (b * nrb + r, c0 + g)),
                  pl.BlockSpec((L, hw), lambda b, g, r: (b, c0 + nhg + g)),
                  pl.BlockSpec((L, hw), lambda b, g, r: (b, c0 + 2 * nhg + g)),
                  pl.BlockSpec((None, NA_WIN_ROWS, NA_HG * Wc, NA_WIN_ROWS * Wc), lambda b, g, r: (g, 0, 0, 0))],
        out_specs=pl.BlockSpec((NA_RB * Wc, hw), lambda b, g, r: (b * nrb + r, g)),
        compiler_params=_cp(("parallel", "parallel", "arbitrary")),
        name="neighborhood_attention",
    )(qkv, qkv, qkv, bias)


def _trunk(x, mem, B, L, prm, wb):
    n_mem = mem.shape[0] // B
    depth = prm["norm_g"].shape[0]
    hw3 = 3 * HY_WIDTH
    for layer in range(depth):
        i = layer // 2
        g = prm["norm_g"][layer]
        wo = wb["mix_wo"]
        if layer % 2 == 0:
            ident = jnp.zeros((3, S5_WIDTH), F32).at[1].set(1.0)
            cw = jnp.concatenate([prm["hy_short_w"][i].astype(F32), ident], axis=1)
            cb = jnp.concatenate([prm["hy_short_b"][i].astype(F32), jnp.zeros((S5_WIDTH,), F32)])
            uc = norm_matmul_conv(x, g[0], wb["ev_w_in"], cw, cb, L, layer=i)
            z = hyena_mixer(uc, B, L, prm["hy_w1"][i], prm["hy_b1"][i], prm["hy_freq"][i], prm["hy_w2"][i],
                            prm["hy_b2"][i], prm["hy_w3"][i], prm["hy_skip"][i])
            ss = s5_mixer(uc, hw3, B, L, prm["s5_a_re"][i], prm["s5_a_im"][i], prm["s5_log_dt"][i],
                          prm["s5_b_re"][i], prm["s5_b_im"][i], prm["s5_c_re"][i], prm["s5_c_im"][i],
                          prm["s5_d"][i], prm["s5_w_glu"][i])
            ops = [(z, wo, prm["hy_out_g"][i], 0), (ss, wo, None, HY_WIDTH)]
        else:
            p = norm_matmul(x, g[0], wb["od_w_in"], BF16, layer=i, name="odd_in_proj")
            ret = retention_mixer(p, B, L, prm["ret_decay"][i])
            na = neighborhood_mixer(p, 4 * RET_WIDTH, B, L, prm["na_rpb"][i])
            ops = [(ret, wo, None, 0), (na, wo, None, RET_WIDTH)]
        x, xn = matmul_norm_residual(ops, g[1], x, g[2], layer=layer, name="mix_out_proj")
        kv = norm_matmul(mem, prm["mem_norm_g"][layer], wb["xa_wkv"], BF16, layer=layer, name="xattn_kv_proj")
        x = xattn_block(xn, x, wb["xa_wq"], kv, wb["xa_wo"], g[3], B, L, n_mem, layer=layer)
        x = ffn_block(x, g[4], wb["ffn_wg"], wb["ffn_wu"], wb["ffn_wd"], g[5], layer=layer)
    return x


def kernel(x_prompt, x_sample, mem_prompt, mem_sample, norm_g, mix_wo, ev_w_in, hy_short_w, hy_short_b, hy_w1, hy_b1, hy_freq, hy_w2, hy_b2, hy_w3, hy_skip, hy_out_g, s5_a_re, s5_a_im, s5_log_dt, s5_b_re, s5_b_im, s5_c_re, s5_c_im, s5_d, s5_w_glu, od_w_in, ret_decay, na_rpb, mem_norm_g, xa_wq, xa_wkv, xa_wo, ffn_wg, ffn_wu, ffn_wd):
    prm = dict(norm_g=norm_g, hy_short_w=hy_short_w, hy_short_b=hy_short_b, hy_w1=hy_w1, hy_b1=hy_b1,
               hy_freq=hy_freq, hy_w2=hy_w2, hy_b2=hy_b2, hy_w3=hy_w3, hy_skip=hy_skip, hy_out_g=hy_out_g,
               s5_a_re=s5_a_re, s5_a_im=s5_a_im, s5_log_dt=s5_log_dt, s5_b_re=s5_b_re, s5_b_im=s5_b_im,
               s5_c_re=s5_c_re, s5_c_im=s5_c_im, s5_d=s5_d, s5_w_glu=s5_w_glu, ret_decay=ret_decay,
               na_rpb=na_rpb, mem_norm_g=mem_norm_g)
    wb = {k: v.astype(BF16) for k, v in dict(mix_wo=mix_wo, ev_w_in=ev_w_in, od_w_in=od_w_in, xa_wq=xa_wq,
                                             xa_wkv=xa_wkv, xa_wo=xa_wo, ffn_wg=ffn_wg, ffn_wu=ffn_wu,
                                             ffn_wd=ffn_wd).items()}
    outs = []
    for x, mem in ((x_prompt, mem_prompt), (x_sample, mem_sample)):
        B, L, D = x.shape
        y = _trunk(x.reshape(B * L, D), mem.reshape(-1, D), B, L, prm, wb)
        outs.append(y.reshape(B, L, D))
    return tuple(outs)
```
